```python
import math
import jax, jax.numpy as jnp
from jax import lax
import numpy as np

D_MODEL = 4096
BATCH = 1
SEQ = 8192
DEPTH = 2

A_HEADS = 16
A_KV_GROUPS = 4
A_HEADS_PER_GROUP = A_HEADS // A_KV_GROUPS
HEAD_DIM = 128
A_WIDTH = A_HEADS * HEAD_DIM
A_KV_WIDTH = A_KV_GROUPS * HEAD_DIM
L_CMP = 32
STRIDE_CMP = 16
L_SEL = 64
N_SEL = 16
WINDOW = 512
Q_BLOCK = 128
BIG = 1e9
B_GROUPS = 16
CHUNK = 128
B_WIDTH = 2048
B_GROUP_DIM = B_WIDTH // B_GROUPS
NUM_BUCKETS = 32
MAX_DISTANCE = 128
NORM_EPS = 1e-6
NEG_INF = -1e30
SPLIT_SIZES = (A_WIDTH, 6 * A_KV_WIDTH, 3 * A_HEADS, A_WIDTH, B_WIDTH, B_WIDTH, B_WIDTH, D_MODEL, D_MODEL)
IN_WIDTH = A_WIDTH + 6 * A_KV_WIDTH + 3 * A_HEADS + A_WIDTH + 3 * B_WIDTH + 2 * D_MODEL

kernel_name = "hybrid_nsa_gmlp_gated_parallel"


def rms_norm(x, g):
    xf = x.astype(jnp.float32)
    y = xf * lax.rsqrt(jnp.mean(xf * xf, axis=-1, keepdims=True) + NORM_EPS)
    return (y * g.astype(jnp.float32)).astype(x.dtype)


def masked_softmax(logits, mask):
    lf = jnp.where(mask, logits.astype(jnp.float32), NEG_INF)
    return jax.nn.softmax(lf, axis=-1)


def t5_bucket(dist):
    n = jnp.maximum(dist, 0)
    max_exact = NUM_BUCKETS // 2
    nf = jnp.maximum(n, 1).astype(jnp.float32)
    large = max_exact + (jnp.log(nf / max_exact) / math.log(MAX_DISTANCE / max_exact)
                         * (NUM_BUCKETS - max_exact)).astype(jnp.int32)
    large = jnp.minimum(large, NUM_BUCKETS - 1)
    return jnp.where(n < max_exact, n, large)


def compress(kv, pe, w1, w2):
    b, t, g, dh = kv.shape
    nc = (t - L_CMP) // STRIDE_CMP + 1
    idx = jnp.arange(nc)[:, None] * STRIDE_CMP + jnp.arange(L_CMP)[None, :]
    blk = kv[:, idx] + pe[None, None, :, None, :]
    blk = jnp.transpose(blk, (0, 1, 3, 2, 4)).reshape(b, nc, g, L_CMP * dh)
    return jax.nn.silu(blk @ w1) @ w2


def nsa_compressed(q, kc, vc, rel_bias, t_pos):
    b, t, g, r, dh = q.shape
    nc = kc.shape[1]
    blk_end = jnp.arange(nc, dtype=jnp.int32) * STRIDE_CMP + (L_CMP - 1)
    dist = t_pos[:, None] - blk_end[None, :]
    valid = dist >= 0
    bias = rel_bias[t5_bucket(dist)].reshape(t, nc, g, r).transpose(2, 3, 0, 1)
    s = jnp.einsum('btgrd,bngd->bgrtn', q, kc) * (dh ** -0.5) + bias
    any_valid = jnp.any(valid, axis=-1)[:, None].astype(jnp.float32)
    p = masked_softmax(s, valid) * any_valid
    o = jnp.einsum('bgrtn,bngd->btgrd', p.astype(vc.dtype), vc)
    return o, p


def select_blocks(p_cmp, t_pos):
    pg = p_cmp.sum(axis=2)
    nc = pg.shape[-1]
    t = t_pos.shape[0]
    ns = t // L_SEL
    m_rng = L_SEL // STRIDE_CMP
    n_rng = L_CMP // STRIDE_CMP
    offs = (jnp.arange(m_rng)[:, None] - jnp.arange(n_rng)[None, :]).reshape(-1)
    idx = m_rng * jnp.arange(ns)[:, None] + offs[None, :]
    ok = (idx >= 0) & (idx < nc)
    gathered = jnp.take(pg, jnp.clip(idx, 0, nc - 1), axis=-1)
    imp = jnp.where(ok, gathered, 0.0).sum(-1)
    jt = t_pos // L_SEL
    j = jnp.arange(ns, dtype=jnp.int32)[None, :]
    forced = (j == 0) | (j == jt[:, None]) | (j == jt[:, None] - 1)
    future = j > jt[:, None]
    score = jnp.where(forced, BIG, jnp.where(future, -1.0, imp))
    _, sel = lax.top_k(score, min(N_SEL, ns))
    return sel


def nsa_selected(q, ks, vs, sel, rel_bias, t_pos):
    b, t, g, r, dh = q.shape
    ns = t // L_SEL
    nq = t // Q_BLOCK
    n = sel.shape[-1]
    kb = ks.reshape(b, ns, L_SEL, g, dh).transpose(0, 3, 1, 2, 4)
    vb = vs.reshape(b, ns, L_SEL, g, dh).transpose(0, 3, 1, 2, 4)
    tbl = rel_bias.reshape(NUM_BUCKETS, g, r).transpose(1, 0, 2)
    qc = q.reshape(b, nq, Q_BLOCK, g, r, dh).transpose(1, 0, 2, 3, 4, 5)
    selc = sel.reshape(b, g, nq, Q_BLOCK, n).transpose(2, 0, 1, 3, 4)
    tc = t_pos.reshape(nq, Q_BLOCK)
    bi = jnp.arange(b)[:, None, None, None]
    gi = jnp.arange(g)[None, :, None, None]
    key_off = jnp.arange(L_SEL, dtype=jnp.int32)

    def block(args):
        q_b, sel_b, t_b = args
        kg = kb[bi, gi, sel_b]
        vg = vb[bi, gi, sel_b]
        pos = sel_b[..., None] * L_SEL + key_off
        dist = t_b[None, None, :, None, None] - pos
        bias = tbl[gi[..., None], t5_bucket(dist)]
        bias = jnp.moveaxis(bias, -1, 2).reshape(b, g, r, Q_BLOCK, n * L_SEL)
        s = jnp.einsum('bqgrd,bgqnld->bgrqnl', q_b, kg).reshape(b, g, r, Q_BLOCK, n * L_SEL)
        s = s * (dh ** -0.5) + bias
        valid = (dist >= 0).reshape(b, g, 1, Q_BLOCK, n * L_SEL)
        p = masked_softmax(s, valid).astype(vg.dtype)
        return jnp.einsum('bgrqk,bgqkd->bqgrd', p, vg.reshape(b, g, Q_BLOCK, n * L_SEL, dh))

    o = lax.map(block, (qc, selc, tc))
    return o.transpose(1, 0, 2, 3, 4, 5).reshape(b, t, g, r, dh)


def nsa_window(q, kw, vw, rel_bias, t_pos):
    b, t, g, r, dh = q.shape
    nb = t // Q_BLOCK
    nw = WINDOW // Q_BLOCK
    kk = (nw + 1) * Q_BLOCK
    pad = ((0, 0), (WINDOW, 0), (0, 0), (0, 0))
    kp = jnp.pad(kw, pad).reshape(b, nb + nw, Q_BLOCK, g, dh)
    vp = jnp.pad(vw, pad).reshape(b, nb + nw, Q_BLOCK, g, dh)
    blk = jnp.arange(nb)[:, None] + jnp.arange(nw + 1)[None, :]
    kband = kp[:, blk].reshape(b, nb, kk, g, dh)
    vband = vp[:, blk].reshape(b, nb, kk, g, dh)
    pos_k = ((blk - nw)[..., None] * Q_BLOCK + jnp.arange(Q_BLOCK)).reshape(nb, kk).astype(jnp.int32)
    t_q = t_pos.reshape(nb, Q_BLOCK)
    dist = t_q[:, :, None] - pos_k[:, None, :]
    valid = (dist >= 0) & (dist < WINDOW) & (pos_k >= 0)[:, None, :]
    bias = rel_bias[t5_bucket(dist)].reshape(nb, Q_BLOCK, kk, g, r).transpose(3, 4, 0, 1, 2)
    qb = q.reshape(b, nb, Q_BLOCK, g, r, dh)
    s = jnp.einsum('bnqgrd,bnkgd->bgrnqk', qb, kband) * (dh ** -0.5) + bias
    p = masked_softmax(s, valid).astype(vband.dtype)
    o = jnp.einsum('bgrnqk,bnkgd->bnqgrd', p, vband)
    return o.reshape(b, t, g, r, dh)


def gmlp_chunked(u, v, norm_g, w_s, b_s):
    b, t, _ = u.shape
    nc = t // CHUNK
    u = jax.nn.gelu(u)
    v = rms_norm(jax.nn.gelu(v), norm_g)
    vc = v.reshape(b, nc, CHUNK, B_GROUPS, B_GROUP_DIM)
    causal = jnp.tril(jnp.ones((CHUNK, CHUNK), dtype=bool))
    w = jnp.where(causal[None], w_s, 0)
    f = jnp.einsum('gpq,bcqgd->bcpgd', w, vc) + b_s.T[None, None, :, :, None]
    return u * f.reshape(b, t, B_WIDTH)


def setup_inputs(seed: int = 0) -> dict:
    key = jax.random.key(seed)
    ks = jax.random.split(key, 17)

    def nrm(k, shape, scale):
        return jax.random.normal(k, shape, jnp.float32) * scale

    lc = L_CMP * HEAD_DIM
    return {
        "x": nrm(ks[0], (BATCH, SEQ, D_MODEL), 1.0),
        "rel_bias": nrm(ks[1], (NUM_BUCKETS, A_HEADS), 0.5),
        "pre_norm": 1.0 + nrm(ks[2], (DEPTH, D_MODEL), 0.01),
        "w_in": nrm(ks[3], (DEPTH, D_MODEL, IN_WIDTH), D_MODEL ** -0.5),
        "cmp_pe_k": nrm(ks[4], (DEPTH, L_CMP, HEAD_DIM), 0.1),
        "cmp_w1_k": nrm(ks[5], (DEPTH, lc, HEAD_DIM), lc ** -0.5),
        "cmp_w2_k": nrm(ks[6], (DEPTH, HEAD_DIM, HEAD_DIM), HEAD_DIM ** -0.5),
        "cmp_pe_v": nrm(ks[7], (DEPTH, L_CMP, HEAD_DIM), 0.1),
        "cmp_w1_v": nrm(ks[8], (DEPTH, lc, HEAD_DIM), lc ** -0.5),
        "cmp_w2_v": nrm(ks[9], (DEPTH, HEAD_DIM, HEAD_DIM), HEAD_DIM ** -0.5),
        "w_out_a": nrm(ks[10], (DEPTH, A_WIDTH, D_MODEL), A_WIDTH ** -0.5),
        "sgu_norm": 1.0 + nrm(ks[11], (DEPTH, B_WIDTH), 0.01),
        "sgu_w": nrm(ks[12], (DEPTH, B_GROUPS, CHUNK, CHUNK), CHUNK ** -0.5),
        "sgu_b": 1.0 + nrm(ks[13], (DEPTH, B_GROUPS, CHUNK), 0.1),
        "w_out_b": nrm(ks[14], (DEPTH, B_WIDTH, D_MODEL), B_WIDTH ** -0.5),
        "w_out": nrm(ks[15], (DEPTH, D_MODEL, D_MODEL), D_MODEL ** -0.5),
        "post_norm": 1.0 + nrm(ks[16], (DEPTH, D_MODEL), 0.01),
    }


def reference(x, rel_bias, pre_norm, w_in, cmp_pe_k, cmp_w1_k, cmp_w2_k, cmp_pe_v,
              cmp_w1_v, cmp_w2_v, w_out_a, sgu_norm, sgu_w, sgu_b, w_out_b, w_out, post_norm):
    b, t, _ = x.shape
    g, r, dh = A_KV_GROUPS, A_HEADS_PER_GROUP, HEAD_DIM
    t_pos = jnp.arange(t, dtype=jnp.int32)
    offsets = [int(o) for o in np.cumsum(SPLIT_SIZES)[:-1]]
    for layer in range(DEPTH):
        h = rms_norm(x, pre_norm[layer])
        proj = h @ w_in[layer]
        q, kv, gates_a, z_a, u_b, v_b, z_b, m_a, m_b = jnp.split(proj, offsets, axis=-1)
        q = q.reshape(b, t, g, r, dh)
        kv = kv.reshape(b, t, 6, g, dh)
        kc = compress(kv[:, :, 0], cmp_pe_k[layer], cmp_w1_k[layer], cmp_w2_k[layer])
        vc = compress(kv[:, :, 1], cmp_pe_v[layer], cmp_w1_v[layer], cmp_w2_v[layer])
        o_cmp, p_cmp = nsa_compressed(q, kc, vc, rel_bias, t_pos)
        sel = select_blocks(p_cmp, t_pos)
        o_sel = nsa_selected(q, kv[:, :, 2], kv[:, :, 3], sel, rel_bias, t_pos)
        o_win = nsa_window(q, kv[:, :, 4], kv[:, :, 5], rel_bias, t_pos)
        ga = jax.nn.sigmoid(gates_a).reshape(b, t, g, r, 3)
        o_a = (ga[..., 0:1] * o_cmp + ga[..., 1:2] * o_sel + ga[..., 2:3] * o_win).reshape(b, t, A_WIDTH)
        y_a = (o_a * jax.nn.silu(z_a)) @ w_out_a[layer]
        o_b = gmlp_chunked(u_b, v_b, sgu_norm[layer], sgu_w[layer], sgu_b[layer])
        y_b = (o_b * jax.nn.silu(z_b)) @ w_out_b[layer]
        merged = jax.nn.sigmoid(m_a) * y_a + jax.nn.sigmoid(m_b) * y_b
        x = x + rms_norm(merged @ w_out[layer], post_norm[layer])
    return x
```

```python
import functools
import math

import numpy as np
import jax
import jax.numpy as jnp
from jax import lax
from jax.experimental import pallas as pl
from jax.experimental.pallas import tpu as pltpu

F32 = jnp.float32
BF16 = jnp.bfloat16

A_HEADS = 16
A_KV_GROUPS = 4
HEADS_PER_GROUP = A_HEADS // A_KV_GROUPS
HEAD_DIM = 128
A_WIDTH = A_HEADS * HEAD_DIM
A_KV_WIDTH = A_KV_GROUPS * HEAD_DIM
GROUP_Q_WIDTH = HEADS_PER_GROUP * HEAD_DIM
L_CMP = 32
STRIDE_CMP = 16
L_SEL = 64
N_SEL = 16
WINDOW = 512
B_GROUPS = 16
CHUNK = 128
B_WIDTH = 2048
NUM_BUCKETS = 32
MAX_DISTANCE = 128
NORM_EPS = 1e-6
NEG_INF = -1e30
SCALE = HEAD_DIM ** -0.5
N_GATE_COLS = 3 * A_HEADS

V7X_LANES = 128
V7X_SUBLANES = 8
V7X_SCOPED_VMEM_CAP_BYTES = 60000 * 1024

ATTN_TILE = 512
CMP_TQ = 256
CMP_ROWS_PER_TQ = CMP_TQ // STRIDE_CMP
SEL_PER_TILE = ATTN_TILE // L_SEL


def _vmem_limit(block_bytes, temp_bytes=0):
    need = 2 * sum(block_bytes) + temp_bytes + (4 << 20)
    return int(min(max(need, 16 << 20), V7X_SCOPED_VMEM_CAP_BYTES))


def _params(sem, block_bytes, temp_bytes=0):
    return pltpu.CompilerParams(dimension_semantics=sem,
                                vmem_limit_bytes=_vmem_limit(block_bytes, temp_bytes))


def _t5_thresholds():
    n = np.arange(0, 4 * MAX_DISTANCE)
    max_exact = NUM_BUCKETS // 2
    nf = np.maximum(n, 1).astype(np.float32)
    large = max_exact + (np.log(nf / np.float32(max_exact)) / np.float32(math.log(MAX_DISTANCE / max_exact))
                         * np.float32(NUM_BUCKETS - max_exact)).astype(np.int32)
    bucket = np.where(n < max_exact, n, np.minimum(large, NUM_BUCKETS - 1))
    assert np.all(np.diff(bucket) >= 0) and bucket[-1] == NUM_BUCKETS - 1
    return [int(np.argmax(bucket >= b)) for b in range(1, NUM_BUCKETS)]


_T5_THR = _t5_thresholds()


def _sigmoid(x):
    return 1.0 / (1.0 + jnp.exp(-x))


def _silu(x):
    return x * _sigmoid(x)


def _gelu(x):
    return jax.nn.gelu(x, approximate=True)


_ACTS = {"none": lambda v: v, "silu": _silu, "gelu": _gelu, "sigmoid": _sigmoid}


def _rmsnorm_kernel(x_ref, g_ref, o_ref):
    x = x_ref[...]
    ms = jnp.mean(x * x, axis=-1, keepdims=True)
    o_ref[...] = (x * lax.rsqrt(ms + NORM_EPS) * g_ref[...]).astype(o_ref.dtype)


def _rmsnorm(x, g, out_dtype):
    t, d = x.shape
    tm = min(256, t)
    return pl.pallas_call(
        _rmsnorm_kernel,
        grid=(t // tm,),
        in_specs=[pl.BlockSpec((tm, d), lambda i: (i, 0)), pl.BlockSpec((1, d), lambda i: (0, 0))],
        out_specs=pl.BlockSpec((tm, d), lambda i: (i, 0)),
        out_shape=jax.ShapeDtypeStruct((t, d), out_dtype),
        compiler_params=_params(("parallel",), [tm * d * 4, tm * d * 4]),
        name="rmsnorm",
    )(x, g.reshape(1, d))


def _post_kernel(x_ref, y_ref, g_ref, o_ref):
    y = y_ref[...]
    ms = jnp.mean(y * y, axis=-1, keepdims=True)
    o_ref[...] = x_ref[...] + y * lax.rsqrt(ms + NORM_EPS) * g_ref[...]


def _post_norm_residual(x, y, g):
    t, d = x.shape
    tm = min(256, t)
    return pl.pallas_call(
        _post_kernel,
        grid=(t // tm,),
        in_specs=[pl.BlockSpec((tm, d), lambda i: (i, 0)), pl.BlockSpec((tm, d), lambda i: (i, 0)),
                  pl.BlockSpec((1, d), lambda i: (0, 0))],
        out_specs=pl.BlockSpec((tm, d), lambda i: (i, 0)),
        out_shape=jax.ShapeDtypeStruct((t, d), F32),
        compiler_params=_params(("parallel",), [tm * d * 4] * 3),
        name="post_norm_residual",
    )(x, y, g.reshape(1, d))


def _mm_kernel(a_ref, b_ref, o_ref, *, act_ranges):
    acc = jnp.dot(a_ref[...], b_ref[...], preferred_element_type=F32)
    if len(act_ranges) == 1:
        o_ref[...] = _ACTS[act_ranges[0][2]](acc).astype(o_ref.dtype)
        return
    j = pl.program_id(1)
    for lo, hi, act in act_ranges:
        @pl.when((j >= lo) & (j < hi))
        def _(act=act):
            o_ref[...] = _ACTS[act](acc).astype(o_ref.dtype)


def _mm(a, b, out_dtype, act="none", col_acts=None, name="mm"):
    m, k = a.shape
    n = b.shape[1]
    tm = min(1024, m)
    tn = next(c for c in (1024, 512, 256, 128) if n % c == 0)
    if col_acts is None:
        act_ranges = ((0, n // tn, act),)
    else:
        assert all(lo % tn == 0 and hi % tn == 0 for lo, hi, _ in col_acts)
        act_ranges = tuple((lo // tn, hi // tn, ac) for lo, hi, ac in col_acts)
    osz = jnp.dtype(out_dtype).itemsize
    return pl.pallas_call(
        functools.partial(_mm_kernel, act_ranges=act_ranges),
        grid=(m // tm, n // tn),
        in_specs=[pl.BlockSpec((tm, k), lambda i, j: (i, 0)), pl.BlockSpec((k, tn), lambda i, j: (0, j))],
        out_specs=pl.BlockSpec((tm, tn), lambda i, j: (i, j)),
        out_shape=jax.ShapeDtypeStruct((m, n), out_dtype),
        compiler_params=_params(("parallel", "arbitrary"), [tm * k * 2, k * tn * 2, tm * tn * osz],
                                2 * tm * tn * 4),
        name=name,
    )(a, b)


def _t5_bias(dist, rb_ref, h):
    val = rb_ref[NUM_BUCKETS - 1, h]
    for b in range(NUM_BUCKETS - 2, -1, -1):
        val = jnp.where(dist < _T5_THR[b], rb_ref[b, h], val)
    return val


def _bias_tile_kernel(rb_ref, o_ref):
    h = pl.program_id(0)
    kind = pl.program_id(1)
    shape = (ATTN_TILE, ATTN_TILE)
    key = lax.broadcasted_iota(jnp.int32, shape, 0)
    qry = lax.broadcasted_iota(jnp.int32, shape, 1)
    dist = jnp.where(kind == 0, 0, ATTN_TILE) + qry - key
    limit = jnp.where(kind == 2, WINDOW, 4 * ATTN_TILE)
    val = _t5_bias(dist, rb_ref, h)
    o_ref[0, 0] = jnp.where((dist >= 0) & (dist < limit), val, NEG_INF)


def _bias_tiles(rel_bias):
    return pl.pallas_call(
        _bias_tile_kernel,
        grid=(A_HEADS, 3),
        in_specs=[pl.BlockSpec(memory_space=pltpu.SMEM)],
        out_specs=pl.BlockSpec((1, 1, ATTN_TILE, ATTN_TILE), lambda h, k: (h, k, 0, 0)),
        out_shape=jax.ShapeDtypeStruct((A_HEADS, 3, ATTN_TILE, ATTN_TILE), F32),
        compiler_params=_params(("parallel", "parallel"), [ATTN_TILE * ATTN_TILE * 4]),
        name="bias_tiles",
    )(rel_bias)


def _bias_strip_kernel(rb_ref, o_ref):
    h = pl.program_id(0)
    shape = (2 * CMP_ROWS_PER_TQ, CMP_TQ)
    c = lax.broadcasted_iota(jnp.int32, shape, 0)
    a = lax.broadcasted_iota(jnp.int32, shape, 1)
    dist = a - STRIDE_CMP * c + (CMP_TQ - (L_CMP - 1))
    o_ref[0] = jnp.where(dist >= 0, _t5_bias(dist, rb_ref, h), NEG_INF)


def _bias_strip(rel_bias):
    rows = 2 * CMP_ROWS_PER_TQ
    return pl.pallas_call(
        _bias_strip_kernel,
        grid=(A_HEADS,),
        in_specs=[pl.BlockSpec(memory_space=pltpu.SMEM)],
        out_specs=pl.BlockSpec((1, rows, CMP_TQ), lambda h: (h, 0, 0)),
        out_shape=jax.ShapeDtypeStruct((A_HEADS, rows, CMP_TQ), F32),
        compiler_params=_params(("parallel",), [rows * CMP_TQ * 4]),
        name="bias_strip",
    )(rel_bias)


def _compress_kernel(x_ref, pe_ref, w1_ref, w2_ref, o_ref, ot_ref):
    x = x_ref[0].astype(F32)
    half = x.shape[1]
    lo = (x + pe_ref[0, 0:1, :]).astype(BF16)
    hi = (x + pe_ref[0, 1:2, :]).astype(BF16)
    a = jnp.dot(lo, w1_ref[0, :half, :], preferred_element_type=F32)
    b = jnp.dot(hi, w1_ref[0, half:, :], preferred_element_type=F32)
    n_rows = x.shape[0]
    hidden = a + pltpu.roll(b, n_rows - 1, 0)
    out = jnp.dot(_silu(hidden).astype(BF16), w2_ref[0], preferred_element_type=F32)
    o_ref[0] = out.astype(o_ref.dtype)
    ot_ref[0] = out.T.astype(ot_ref.dtype)


def _compress(x_slabs, pe, w1, w2):
    ng, n_rows, half = x_slabs.shape
    g = A_KV_GROUPS
    return pl.pallas_call(
        _compress_kernel,
        grid=(2, g),
        in_specs=[pl.BlockSpec((1, n_rows, half), lambda s, i: (s * g + i, 0, 0)),
                  pl.BlockSpec((1, 2, half), lambda s, i: (s, 0, 0)),
                  pl.BlockSpec((1, 2 * half, HEAD_DIM), lambda s, i: (s, 0, 0)),
                  pl.BlockSpec((1, HEAD_DIM, HEAD_DIM), lambda s, i: (s, 0, 0))],
        out_specs=[pl.BlockSpec((1, n_rows, HEAD_DIM), lambda s, i: (s * g + i, 0, 0)),
                   pl.BlockSpec((1, HEAD_DIM, n_rows), lambda s, i: (s * g + i, 0, 0))],
        out_shape=[jax.ShapeDtypeStruct((ng, n_rows, HEAD_DIM), BF16),
                   jax.ShapeDtypeStruct((ng, HEAD_DIM, n_rows), BF16)],
        compiler_params=_params(("parallel", "parallel"),
                                [n_rows * half * 2, 2 * half * 4, 2 * half * HEAD_DIM * 2],
                                4 * n_rows * half * 4),
        name="compress",
    )(x_slabs, pe, w1, w2)


def _cmp_attn_kernel(q_ref, kc_ref, vct_ref, strip_ref, rb_ref, gate_ref, o_ref, sel_ref, pg_scr):
    g = pl.program_id(0)
    qi = pl.program_id(1)
    n_rows = kc_ref.shape[1]
    n_sel = n_rows // (L_SEL // STRIDE_CMP)
    tq = CMP_TQ
    kc = kc_ref[0]
    vct = vct_ref[0]
    t_row = qi * tq + lax.broadcasted_iota(jnp.int32, (1, tq), 1)
    any_valid = jnp.where(t_row >= L_CMP - 1, 1.0, 0.0)
    rpt = CMP_ROWS_PER_TQ
    pg = jnp.zeros((n_rows, tq), F32)
    for r in range(HEADS_PER_GROUP):
        h = g * HEADS_PER_GROUP + r
        q = q_ref[:, r * HEAD_DIM:(r + 1) * HEAD_DIM]
        st = lax.dot_general(kc, q, (((1,), (1,)), ((), ())), preferred_element_type=F32)
        far = rb_ref[NUM_BUCKETS - 1, h]
        strip_prev = strip_ref[r, 0:rpt, :]
        strip_diag = strip_ref[r, rpt:2 * rpt, :]
        pieces = []
        for ci in range(n_rows // rpt):
            rest = jnp.where(ci < qi - 1, far, NEG_INF)
            pieces.append(jnp.where(ci == qi - 1, strip_prev, jnp.where(ci == qi, strip_diag, rest)))
        s = st * SCALE + jnp.concatenate(pieces, axis=0)
        m = jnp.max(s, axis=0, keepdims=True)
        e = jnp.exp(s - m)
        p = e * (any_valid / jnp.sum(e, axis=0, keepdims=True))
        pg = pg + p
        ot = jnp.dot(vct, p.astype(BF16), preferred_element_type=F32) * gate_ref[0, 0, r:r + 1, :]
        o_ref[:, r * HEAD_DIM:(r + 1) * HEAD_DIM] = ot.T.astype(o_ref.dtype)

    pad = V7X_SUBLANES
    lane_tiles = tq // V7X_LANES
    for c in range(lane_tiles):
        pg_scr[c, 0:pad, :] = jnp.zeros((pad, V7X_LANES), F32)
        pg_scr[c, pad:pad + n_rows, :] = pg[:, c * V7X_LANES:(c + 1) * V7X_LANES]
    per_sel = L_SEL // STRIDE_CMP

    def rows(off):
        return jnp.concatenate(
            [pg_scr[c, pl.ds(pad + off, n_sel, stride=per_sel), :] for c in range(lane_tiles)], axis=1)

    imp = (rows(-1) + rows(3)) + 2.0 * (rows(0) + rows(1) + rows(2))

    j = lax.broadcasted_iota(jnp.int32, (n_sel, tq), 0)
    jf = j.astype(F32)
    jt = jnp.right_shift(t_row, int(math.log2(L_SEL)))
    forced = (j == 0) | (j == jt) | (j == jt - 1)
    cand = (j >= 1) & (j <= jt - 2)
    score = jnp.where(cand, imp, -1.0)
    sel = jnp.where(forced, 1.0, 0.0)
    for _ in range(N_SEL - 3):
        mx = jnp.max(score, axis=0, keepdims=True)
        first = jnp.min(jnp.where(score == mx, jf, float(n_sel)), axis=0, keepdims=True)
        pick = (jf == first) & (mx >= 0.0)
        sel = jnp.where(pick, 1.0, sel)
        score = jnp.where(pick, -2.0, score)
    sel_ref[0] = sel


def _cmp_attn(proj, kc, vct, strip, rel_bias, gates_t, t):
    n_rows = kc.shape[1]
    n_sel = t // L_SEL
    g = A_KV_GROUPS
    tq = CMP_TQ
    return pl.pallas_call(
        _cmp_attn_kernel,
        grid=(g, t // tq),
        in_specs=[pl.BlockSpec((tq, GROUP_Q_WIDTH), lambda gi, qi: (qi, gi)),
                  pl.BlockSpec((1, n_rows, HEAD_DIM), lambda gi, qi: (gi, 0, 0)),
                  pl.BlockSpec((1, HEAD_DIM, n_rows), lambda gi, qi: (g + gi, 0, 0)),
                  pl.BlockSpec((HEADS_PER_GROUP, 2 * CMP_ROWS_PER_TQ, tq), lambda gi, qi: (gi, 0, 0)),
                  pl.BlockSpec(memory_space=pltpu.SMEM),
                  pl.BlockSpec((1, 1, V7X_SUBLANES, tq), lambda gi, qi: (0, gi, 0, qi))],
        out_specs=[pl.BlockSpec((tq, GROUP_Q_WIDTH), lambda gi, qi: (qi, gi)),
                   pl.BlockSpec((1, n_sel, tq), lambda gi, qi: (gi, 0, qi))],
        out_shape=[jax.ShapeDtypeStruct((t, A_WIDTH), BF16),
                   jax.ShapeDtypeStruct((g, n_sel, t), F32)],
        scratch_shapes=[pltpu.VMEM((tq // V7X_LANES, n_rows + V7X_SUBLANES, V7X_LANES), F32)],
        compiler_params=_params(("parallel", "arbitrary"),
                                [tq * GROUP_Q_WIDTH * 2 * 2, n_rows * HEAD_DIM * 2 * 2, n_sel * tq * 4],
                                12 * n_rows * tq * 4),
        name="cmp_attn",
    )(proj, kc, vct, strip, rel_bias, gates_t)


def _softmax_update(r, s, vt, m_scr, l_scr, acc_scr):
    m_prev = m_scr[r]
    m_new = jnp.maximum(m_prev, jnp.max(s, axis=0, keepdims=True))
    alpha = jnp.exp(m_prev - m_new)
    p = jnp.exp(s - m_new)
    l_scr[r] = alpha * l_scr[r] + jnp.sum(p, axis=0, keepdims=True)
    acc_scr[r] = acc_scr[r] * alpha + jnp.dot(vt, p.astype(BF16), preferred_element_type=F32)
    m_scr[r] = m_new


def _softmax_init(m_scr, l_scr, acc_scr):
    m_scr[...] = jnp.full(m_scr.shape, NEG_INF, F32)
    l_scr[...] = jnp.zeros(l_scr.shape, F32)
    acc_scr[...] = jnp.zeros(acc_scr.shape, F32)


def _softmax_finish(gate_ref, o_ref, l_scr, acc_scr):
    for r in range(HEADS_PER_GROUP):
        ot = acc_scr[r] * (gate_ref[0, 0, r:r + 1, :] / l_scr[r])
        o_ref[:, r * HEAD_DIM:(r + 1) * HEAD_DIM] = ot.T.astype(o_ref.dtype)


def _scores(k, q_ref, r):
    q = q_ref[:, r * HEAD_DIM:(r + 1) * HEAD_DIM]
    return lax.dot_general(k, q, (((1,), (1,)), ((), ())), preferred_element_type=F32) * SCALE


def _sel_attn_kernel(q_ref, k_ref, vt_ref, sel_ref, bt_diag_ref, bt_prev_ref, rb_ref, gate_ref, o_ref,
                     m_scr, l_scr, acc_scr):
    g = pl.program_id(0)
    qi = pl.program_id(1)
    tk = ATTN_TILE
    _softmax_init(m_scr, l_scr, acc_scr)

    def tile(kvj, bt_ref):
        k = k_ref[pl.ds(pl.multiple_of(kvj * tk, tk), tk), :]
        vt = vt_ref[0, kvj]
        add_rows = jnp.where(sel_ref[0, kvj] > 0.5, 0.0, NEG_INF)
        for r in range(HEADS_PER_GROUP):
            st = _scores(k, q_ref, r)
            if bt_ref is None:
                row_bias = add_rows + rb_ref[NUM_BUCKETS - 1, g * HEADS_PER_GROUP + r]
            else:
                row_bias = add_rows
                st = st + bt_ref[r, 0]
            s = jnp.concatenate(
                [st[L_SEL * jj:L_SEL * (jj + 1), :] + row_bias[jj:jj + 1, :] for jj in range(SEL_PER_TILE)],
                axis=0)
            _softmax_update(r, s, vt, m_scr, l_scr, acc_scr)

    tile(qi, bt_diag_ref)

    @pl.when(qi >= 1)
    def _():
        tile(qi - 1, bt_prev_ref)

    def far_tile(kvj, carry):
        tile(kvj, None)
        return carry

    lax.fori_loop(0, jnp.maximum(qi - 1, 0), far_tile, 0)
    _softmax_finish(gate_ref, o_ref, l_scr, acc_scr)


def _win_attn_kernel(q_ref, k_ref, vt_ref, bt_diag_ref, bt_prev_ref, gate_ref, o_ref, m_scr, l_scr, acc_scr):
    qi = pl.program_id(1)
    tk = ATTN_TILE
    _softmax_init(m_scr, l_scr, acc_scr)

    def tile(kvj, bt_ref):
        k = k_ref[pl.ds(pl.multiple_of(kvj * tk, tk), tk), :]
        vt = vt_ref[0, kvj]
        for r in range(HEADS_PER_GROUP):
            _softmax_update(r, _scores(k, q_ref, r) + bt_ref[r, 0], vt, m_scr, l_scr, acc_scr)

    tile(qi, bt_diag_ref)

    @pl.when(qi >= 1)
    def _():
        tile(qi - 1, bt_prev_ref)

    _softmax_finish(gate_ref, o_ref, l_scr, acc_scr)


def _attn_scratch(tq):
    return [pltpu.VMEM((HEADS_PER_GROUP, 1, tq), F32), pltpu.VMEM((HEADS_PER_GROUP, 1, tq), F32),
            pltpu.VMEM((HEADS_PER_GROUP, HEAD_DIM, tq), F32)]


def _attn_common_specs(t, col_k, branch):
    tq = ATTN_TILE
    k_col0 = col_k // HEAD_DIM
    q_spec = pl.BlockSpec((tq, GROUP_Q_WIDTH), lambda gi, qi: (qi, gi))
    k_spec = pl.BlockSpec((t, HEAD_DIM), lambda gi, qi: (0, k_col0 + gi))
    vt_spec = pl.BlockSpec((1, t // ATTN_TILE, HEAD_DIM, ATTN_TILE), lambda gi, qi: (gi, 0, 0, 0))
    gate_spec = pl.BlockSpec((1, 1, V7X_SUBLANES, tq), lambda gi, qi: (branch, gi, 0, qi))
    out_spec = pl.BlockSpec((tq, GROUP_Q_WIDTH), lambda gi, qi: (qi, gi))
    return q_spec, k_spec, vt_spec, gate_spec, out_spec


def _bt_spec(kind):
    return pl.BlockSpec((HEADS_PER_GROUP, 1, ATTN_TILE, ATTN_TILE), lambda gi, qi: (gi, kind, 0, 0))


def _attn_block_bytes(t):
    tq = ATTN_TILE
    return [tq * GROUP_Q_WIDTH * 2 * 2, t * HEAD_DIM * 2 * 2, 2 * HEADS_PER_GROUP * tq * tq * 4]


def _sel_attn(proj, col_k, vt, sel_t, bt, rel_bias, gates_t, t):
    tq = ATTN_TILE
    q_spec, k_spec, vt_spec, gate_spec, out_spec = _attn_common_specs(t, col_k, 1)
    sel_spec = pl.BlockSpec((1, t // ATTN_TILE, SEL_PER_TILE, tq), lambda gi, qi: (gi, 0, 0, qi))
    return pl.pallas_call(
        _sel_attn_kernel,
        grid=(A_KV_GROUPS, t // tq),
        in_specs=[q_spec, k_spec, vt_spec, sel_spec, _bt_spec(0), _bt_spec(1),
                  pl.BlockSpec(memory_space=pltpu.SMEM), gate_spec],
        out_specs=out_spec,
        out_shape=jax.ShapeDtypeStruct((t, A_WIDTH), BF16),
        scratch_shapes=_attn_scratch(tq),
        compiler_params=_params(("parallel", "arbitrary"), _attn_block_bytes(t) + [t // L_SEL * tq * 4],
                                8 * tq * tq * 4),
        name="sel_attn",
    )(proj, proj, vt, sel_t, bt, bt, rel_bias, gates_t)


def _win_attn(proj, col_k, vt, bt, gates_t, t):
    tq = ATTN_TILE
    q_spec, k_spec, vt_spec, gate_spec, out_spec = _attn_common_specs(t, col_k, 2)
    return pl.pallas_call(
        _win_attn_kernel,
        grid=(A_KV_GROUPS, t // tq),
        in_specs=[q_spec, k_spec, vt_spec, _bt_spec(0), _bt_spec(2), gate_spec],
        out_specs=out_spec,
        out_shape=jax.ShapeDtypeStruct((t, A_WIDTH), BF16),
        scratch_shapes=_attn_scratch(tq),
        compiler_params=_params(("parallel", "arbitrary"), _attn_block_bytes(t), 8 * tq * tq * 4),
        name="win_attn",
    )(proj, proj, vt, bt, bt, gates_t)


def _gmlp_kernel(u_ref, v_ref, z_ref, ng_ref, w_ref, bt_ref, o_ref):
    v = v_ref[...].astype(F32)
    ms = jnp.mean(v * v, axis=-1, keepdims=True)
    vn = (v * lax.rsqrt(ms + NORM_EPS) * ng_ref[...]).astype(BF16)
    p_idx = lax.broadcasted_iota(jnp.int32, (CHUNK, CHUNK), 0)
    q_idx = lax.broadcasted_iota(jnp.int32, (CHUNK, CHUNK), 1)
    causal = q_idx <= p_idx
    gd = B_WIDTH // B_GROUPS
    for gg in range(B_GROUPS):
        cols = slice(gg * gd, (gg + 1) * gd)
        w = jnp.where(causal, w_ref[gg], jnp.zeros((), w_ref.dtype))
        f = jnp.dot(w, vn[:, cols], preferred_element_type=F32) + bt_ref[:, gg:gg + 1]
        o_ref[:, cols] = (u_ref[:, cols].astype(F32) * f * z_ref[:, cols].astype(F32)).astype(o_ref.dtype)


def _gmlp(proj, col_u, col_v, col_z, norm_g, w_s, b_t, t):
    bw = B_WIDTH

    def col_spec(col):
        return pl.BlockSpec((CHUNK, bw), lambda i: (i, col // bw))

    return pl.pallas_call(
        _gmlp_kernel,
        grid=(t // CHUNK,),
        in_specs=[col_spec(col_u), col_spec(col_v), col_spec(col_z),
                  pl.BlockSpec((1, bw), lambda i: (0, 0)),
                  pl.BlockSpec((B_GROUPS, CHUNK, CHUNK), lambda i: (0, 0, 0)),
                  pl.BlockSpec((CHUNK, B_GROUPS), lambda i: (0, 0))],
        out_specs=pl.BlockSpec((CHUNK, bw), lambda i: (i, 0)),
        out_shape=jax.ShapeDtypeStruct((t, bw), BF16),
        compiler_params=_params(("parallel",), [CHUNK * bw * 2] * 4 + [B_GROUPS * CHUNK * CHUNK * 2],
                                4 * CHUNK * bw * 4),
        name="gmlp",
    )(proj, proj, proj, norm_g.reshape(1, bw), w_s, b_t)


def _merge_kernel(oc_ref, os_ref, ow_ref, za_ref, ob_ref, wa_ref, wb_ref, ma_ref, mb_ref, o_ref, lhs_scr):
    @pl.when(pl.program_id(1) == 0)
    def _():
        o_a = oc_ref[...].astype(F32) + os_ref[...].astype(F32) + ow_ref[...].astype(F32)
        lhs_scr[...] = (o_a * za_ref[...].astype(F32)).astype(BF16)

    ya = jnp.dot(lhs_scr[...], wa_ref[...], preferred_element_type=F32)
    yb = jnp.dot(ob_ref[...], wb_ref[...], preferred_element_type=F32)
    o_ref[...] = (ma_ref[...].astype(F32) * ya + mb_ref[...].astype(F32) * yb).astype(o_ref.dtype)


def _merge(o_cmp, o_sel, o_win, proj, col_za, col_ma, col_mb, o_b, w_a, w_b, t, d):
    tm = min(512, t)
    tn = min(512, d)
    aw = A_WIDTH
    bw = B_WIDTH
    row_a = pl.BlockSpec((tm, aw), lambda i, j: (i, 0))
    return pl.pallas_call(
        _merge_kernel,
        grid=(t // tm, d // tn),
        in_specs=[row_a, row_a, row_a,
                  pl.BlockSpec((tm, aw), lambda i, j: (i, col_za // aw)),
                  pl.BlockSpec((tm, bw), lambda i, j: (i, 0)),
                  pl.BlockSpec((aw, tn), lambda i, j: (0, j)),
                  pl.BlockSpec((bw, tn), lambda i, j: (0, j)),
                  pl.BlockSpec((tm, tn), lambda i, j: (i, col_ma // tn + j)),
                  pl.BlockSpec((tm, tn), lambda i, j: (i, col_mb // tn + j))],
        out_specs=pl.BlockSpec((tm, tn), lambda i, j: (i, j)),
        out_shape=jax.ShapeDtypeStruct((t, d), BF16),
        scratch_shapes=[pltpu.VMEM((tm, aw), BF16)],
        compiler_params=_params(("parallel", "arbitrary"),
                                [tm * aw * 2] * 5 + [aw * tn * 2, bw * tn * 2] + [tm * tn * 2] * 3,
                                tm * aw * 2 + 4 * tm * tn * 4),
        name="merge",
    )(o_cmp, o_sel, o_win, proj, o_b, w_a, w_b, proj, proj)


def kernel(x, rel_bias, pre_norm, w_in, cmp_pe_k, cmp_w1_k, cmp_w2_k, cmp_pe_v, cmp_w1_v, cmp_w2_v,
           w_out_a, sgu_norm, sgu_w, sgu_b, w_out_b, w_out, post_norm):
    batch, t, d = x.shape
    assert batch == 1 and t % ATTN_TILE == 0 and t // L_SEL >= N_SEL
    depth = w_in.shape[0]
    g = A_KV_GROUPS
    xs = x.reshape(t, d)

    src_kv = A_WIDTH
    src_gates = src_kv + 6 * A_KV_WIDTH
    src_za = src_gates + N_GATE_COLS
    src_ma = src_za + A_WIDTH + 3 * B_WIDTH
    col_za = A_WIDTH
    col_u = col_za + A_WIDTH
    col_v = col_u + B_WIDTH
    col_zb = col_v + B_WIDTH
    col_kv = col_zb + B_WIDTH
    col_ma = col_kv + 6 * A_KV_WIDTH
    col_mb = col_ma + d
    n_main = col_mb + d
    col_acts = [(0, col_za, "none"), (col_za, col_u, "silu"), (col_u, col_zb, "gelu"),
                (col_zb, col_kv, "silu"), (col_kv, col_ma, "none"), (col_ma, n_main, "sigmoid")]

    bias_tiles = _bias_tiles(rel_bias)
    bias_strip = _bias_strip(rel_bias)
    slab = STRIDE_CMP * HEAD_DIM

    for layer in range(depth):
        w = w_in[layer]
        w_main = jnp.concatenate([w[:, :src_kv], w[:, src_za:src_ma], w[:, src_kv:src_gates], w[:, src_ma:]],
                                 axis=1).astype(BF16)
        w_gate = jnp.pad(w[:, src_gates:src_za], ((0, 0), (0, V7X_LANES - N_GATE_COLS))).astype(BF16)

        h = _rmsnorm(xs, pre_norm[layer], BF16)
        proj = _mm(h, w_main, BF16, col_acts=col_acts, name="in_proj")
        gates = _mm(h, w_gate, F32, act="sigmoid", name="gate_proj")
        gates_t = gates[:, :N_GATE_COLS].reshape(t, g, HEADS_PER_GROUP, 3).transpose(3, 1, 2, 0)
        gates_t = jnp.pad(gates_t, ((0, 0), (0, 0), (0, V7X_SUBLANES - HEADS_PER_GROUP), (0, 0)))

        kv = proj[:, col_kv:col_ma].reshape(t, 6, g, HEAD_DIM)
        x_slabs = kv[:, 0:2].reshape(t // STRIDE_CMP, STRIDE_CMP, 2 * g, HEAD_DIM)
        x_slabs = x_slabs.transpose(2, 0, 1, 3).reshape(2 * g, t // STRIDE_CMP, slab)
        pe = jnp.stack([cmp_pe_k[layer], cmp_pe_v[layer]]).reshape(2, 2, slab)
        w1 = jnp.stack([cmp_w1_k[layer], cmp_w1_v[layer]]).astype(BF16)
        w2 = jnp.stack([cmp_w2_k[layer], cmp_w2_v[layer]]).astype(BF16)
        kc, kct = _compress(x_slabs, pe, w1, w2)

        def v_tiles(s):
            vt = kv[:, s].reshape(t // ATTN_TILE, ATTN_TILE, g, HEAD_DIM)
            return vt.transpose(2, 0, 3, 1)

        o_cmp, sel_t = _cmp_attn(proj, kc, kct, bias_strip, rel_bias, gates_t, t)
        sel_t = sel_t.reshape(g, t // ATTN_TILE, SEL_PER_TILE, t)
        o_sel = _sel_attn(proj, col_kv + 2 * A_KV_WIDTH, v_tiles(3), sel_t, bias_tiles, rel_bias, gates_t, t)
        o_win = _win_attn(proj, col_kv + 4 * A_KV_WIDTH, v_tiles(5), bias_tiles, gates_t, t)

        o_b = _gmlp(proj, col_u, col_v, col_zb, sgu_norm[layer], sgu_w[layer].astype(BF16),
                    sgu_b[layer].T, t)
        merged = _merge(o_cmp, o_sel, o_win, proj, col_za, col_ma, col_mb, o_b,
                        w_out_a[layer].astype(BF16), w_out_b[layer].astype(BF16), t, d)
        y = _mm(merged, w_out[layer].astype(BF16), F32, name="out_proj")
        xs = _post_norm_residual(xs, y, post_norm[layer])
    return xs.reshape(batch, t, d)
```

```python
import functools
import math

import numpy as np
import jax
import jax.numpy as jnp
from jax import lax
from jax.experimental import pallas as pl
from jax.experimental.pallas import tpu as pltpu

F32 = jnp.float32
BF16 = jnp.bfloat16

A_HEADS = 16
A_KV_GROUPS = 4
HEADS_PER_GROUP = A_HEADS // A_KV_GROUPS
HEAD_DIM = 128
A_WIDTH = A_HEADS * HEAD_DIM
A_KV_WIDTH = A_KV_GROUPS * HEAD_DIM
GROUP_Q_WIDTH = HEADS_PER_GROUP * HEAD_DIM
L_CMP = 32
STRIDE_CMP = 16
L_SEL = 64
N_SEL = 16
WINDOW = 512
B_GROUPS = 16
CHUNK = 128
B_WIDTH = 2048
NUM_BUCKETS = 32
MAX_DISTANCE = 128
NORM_EPS = 1e-6
NEG_INF = -1e30
SCALE = HEAD_DIM ** -0.5
LOG2_E = math.log2(math.e)
SCALE_LOG2 = SCALE * LOG2_E
N_GATE_COLS = 3 * A_HEADS

V7X_LANES = 128
V7X_SUBLANES = 8
V7X_SCOPED_VMEM_CAP_BYTES = 60000 * 1024

ATTN_TILE = 512
CMP_TQ = 256
CMP_ROWS_PER_TQ = CMP_TQ // STRIDE_CMP
SEL_PER_TILE = ATTN_TILE // L_SEL


def _vmem_limit(block_bytes, temp_bytes=0):
    need = 2 * sum(block_bytes) + temp_bytes + (4 << 20)
    return int(min(max(need, 16 << 20), V7X_SCOPED_VMEM_CAP_BYTES))


def _params(sem, block_bytes, temp_bytes=0):
    return pltpu.CompilerParams(dimension_semantics=sem,
                                vmem_limit_bytes=_vmem_limit(block_bytes, temp_bytes))


def _t5_thresholds():
    n = np.arange(0, 4 * MAX_DISTANCE)
    max_exact = NUM_BUCKETS // 2
    nf = np.maximum(n, 1).astype(np.float32)
    large = max_exact + (np.log(nf / np.float32(max_exact)) / np.float32(math.log(MAX_DISTANCE / max_exact))
                         * np.float32(NUM_BUCKETS - max_exact)).astype(np.int32)
    bucket = np.where(n < max_exact, n, np.minimum(large, NUM_BUCKETS - 1))
    assert np.all(np.diff(bucket) >= 0) and bucket[-1] == NUM_BUCKETS - 1
    return [int(np.argmax(bucket >= b)) for b in range(1, NUM_BUCKETS)]


_T5_THR = _t5_thresholds()


def _sigmoid(x):
    return 1.0 / (1.0 + jnp.exp(-x))


def _silu(x):
    return x * _sigmoid(x)


def _gelu(x):
    return jax.nn.gelu(x, approximate=True)


_ACTS = {"none": lambda v: v, "silu": _silu, "gelu": _gelu, "sigmoid": _sigmoid}


def _rmsnorm_kernel(x_ref, g_ref, o_ref):
    x = x_ref[...]
    ms = jnp.mean(x * x, axis=-1, keepdims=True)
    o_ref[...] = (x * lax.rsqrt(ms + NORM_EPS) * g_ref[...]).astype(o_ref.dtype)


def _rmsnorm(x, g, out_dtype):
    t, d = x.shape
    tm = min(256, t)
    return pl.pallas_call(
        _rmsnorm_kernel,
        grid=(t // tm,),
        in_specs=[pl.BlockSpec((tm, d), lambda i: (i, 0)), pl.BlockSpec((1, d), lambda i: (0, 0))],
        out_specs=pl.BlockSpec((tm, d), lambda i: (i, 0)),
        out_shape=jax.ShapeDtypeStruct((t, d), out_dtype),
        compiler_params=_params(("parallel",), [tm * d * 4, tm * d * 4]),
        name="rmsnorm",
    )(x, g.reshape(1, d))


def _post_kernel(x_ref, y_ref, g_ref, o_ref):
    y = y_ref[...]
    ms = jnp.mean(y * y, axis=-1, keepdims=True)
    o_ref[...] = x_ref[...] + y * lax.rsqrt(ms + NORM_EPS) * g_ref[...]


def _post_norm_residual(x, y, g):
    t, d = x.shape
    tm = min(256, t)
    return pl.pallas_call(
        _post_kernel,
        grid=(t // tm,),
        in_specs=[pl.BlockSpec((tm, d), lambda i: (i, 0)), pl.BlockSpec((tm, d), lambda i: (i, 0)),
                  pl.BlockSpec((1, d), lambda i: (0, 0))],
        out_specs=pl.BlockSpec((tm, d), lambda i: (i, 0)),
        out_shape=jax.ShapeDtypeStruct((t, d), F32),
        compiler_params=_params(("parallel",), [tm * d * 4] * 3),
        name="post_norm_residual",
    )(x, y, g.reshape(1, d))


def _mm_kernel(a_ref, b_ref, o_ref, *, act_ranges):
    acc = jnp.dot(a_ref[...], b_ref[...], preferred_element_type=F32)
    if len(act_ranges) == 1:
        o_ref[...] = _ACTS[act_ranges[0][2]](acc).astype(o_ref.dtype)
        return
    j = pl.program_id(1)
    for lo, hi, act in act_ranges:
        @pl.when((j >= lo) & (j < hi))
        def _(act=act):
            o_ref[...] = _ACTS[act](acc).astype(o_ref.dtype)


def _mm(a, b, out_dtype, act="none", col_acts=None, name="mm"):
    m, k = a.shape
    n = b.shape[1]
    tm = min(1024, m)
    tn = next(c for c in (1024, 512, 256, 128) if n % c == 0)
    if col_acts is None:
        act_ranges = ((0, n // tn, act),)
    else:
        assert all(lo % tn == 0 and hi % tn == 0 for lo, hi, _ in col_acts)
        act_ranges = tuple((lo // tn, hi // tn, ac) for lo, hi, ac in col_acts)
    osz = jnp.dtype(out_dtype).itemsize
    return pl.pallas_call(
        functools.partial(_mm_kernel, act_ranges=act_ranges),
        grid=(m // tm, n // tn),
        in_specs=[pl.BlockSpec((tm, k), lambda i, j: (i, 0)), pl.BlockSpec((k, tn), lambda i, j: (0, j))],
        out_specs=pl.BlockSpec((tm, tn), lambda i, j: (i, j)),
        out_shape=jax.ShapeDtypeStruct((m, n), out_dtype),
        compiler_params=_params(("parallel", "arbitrary"), [tm * k * 2, k * tn * 2, tm * tn * osz],
                                2 * tm * tn * 4),
        name=name,
    )(a, b)


def _t5_bias(dist, rb_ref, h):
    val = rb_ref[NUM_BUCKETS - 1, h]
    for b in range(NUM_BUCKETS - 2, -1, -1):
        val = jnp.where(dist < _T5_THR[b], rb_ref[b, h], val)
    return val


def _bias_tile_kernel(rb_ref, o_ref):
    h = pl.program_id(0)
    kind = pl.program_id(1)
    shape = (ATTN_TILE, ATTN_TILE)
    key = lax.broadcasted_iota(jnp.int32, shape, 0)
    qry = lax.broadcasted_iota(jnp.int32, shape, 1)
    dist = jnp.where(kind == 0, 0, ATTN_TILE) + qry - key
    limit = jnp.where(kind == 2, WINDOW, 4 * ATTN_TILE)
    val = _t5_bias(dist, rb_ref, h) * LOG2_E
    o_ref[0, 0] = jnp.where((dist >= 0) & (dist < limit), val, NEG_INF)


def _bias_tiles(rel_bias):
    return pl.pallas_call(
        _bias_tile_kernel,
        grid=(A_HEADS, 3),
        in_specs=[pl.BlockSpec(memory_space=pltpu.SMEM)],
        out_specs=pl.BlockSpec((1, 1, ATTN_TILE, ATTN_TILE), lambda h, k: (h, k, 0, 0)),
        out_shape=jax.ShapeDtypeStruct((A_HEADS, 3, ATTN_TILE, ATTN_TILE), F32),
        compiler_params=_params(("parallel", "parallel"), [ATTN_TILE * ATTN_TILE * 4]),
        name="bias_tiles",
    )(rel_bias)


def _bias_strip_kernel(rb_ref, o_ref):
    h = pl.program_id(0)
    shape = (2 * CMP_ROWS_PER_TQ, CMP_TQ)
    c = lax.broadcasted_iota(jnp.int32, shape, 0)
    a = lax.broadcasted_iota(jnp.int32, shape, 1)
    dist = a - STRIDE_CMP * c + (CMP_TQ - (L_CMP - 1))
    o_ref[0] = jnp.where(dist >= 0, _t5_bias(dist, rb_ref, h), NEG_INF)


def _bias_strip(rel_bias):
    rows = 2 * CMP_ROWS_PER_TQ
    return pl.pallas_call(
        _bias_strip_kernel,
        grid=(A_HEADS,),
        in_specs=[pl.BlockSpec(memory_space=pltpu.SMEM)],
        out_specs=pl.BlockSpec((1, rows, CMP_TQ), lambda h: (h, 0, 0)),
        out_shape=jax.ShapeDtypeStruct((A_HEADS, rows, CMP_TQ), F32),
        compiler_params=_params(("parallel",), [rows * CMP_TQ * 4]),
        name="bias_strip",
    )(rel_bias)


def _compress_kernel(x_ref, pe_ref, w1_ref, w2_ref, o_ref, ot_ref):
    x = x_ref[0].astype(F32)
    half = x.shape[1]
    lo = (x + pe_ref[0, 0:1, :]).astype(BF16)
    hi = (x + pe_ref[0, 1:2, :]).astype(BF16)
    a = jnp.dot(lo, w1_ref[0, :half, :], preferred_element_type=F32)
    b = jnp.dot(hi, w1_ref[0, half:, :], preferred_element_type=F32)
    n_rows = x.shape[0]
    hidden = a + pltpu.roll(b, n_rows - 1, 0)
    out = jnp.dot(_silu(hidden).astype(BF16), w2_ref[0], preferred_element_type=F32)
    o_ref[0] = out.astype(o_ref.dtype)
    ot_ref[0] = out.T.astype(ot_ref.dtype)


def _compress(x_slabs, pe, w1, w2):
    ng, n_rows, half = x_slabs.shape
    g = A_KV_GROUPS
    return pl.pallas_call(
        _compress_kernel,
        grid=(2, g),
        in_specs=[pl.BlockSpec((1, n_rows, half), lambda s, i: (s * g + i, 0, 0)),
                  pl.BlockSpec((1, 2, half), lambda s, i: (s, 0, 0)),
                  pl.BlockSpec((1, 2 * half, HEAD_DIM), lambda s, i: (s, 0, 0)),
                  pl.BlockSpec((1, HEAD_DIM, HEAD_DIM), lambda s, i: (s, 0, 0))],
        out_specs=[pl.BlockSpec((1, n_rows, HEAD_DIM), lambda s, i: (s * g + i, 0, 0)),
                   pl.BlockSpec((1, HEAD_DIM, n_rows), lambda s, i: (s * g + i, 0, 0))],
        out_shape=[jax.ShapeDtypeStruct((ng, n_rows, HEAD_DIM), BF16),
                   jax.ShapeDtypeStruct((ng, HEAD_DIM, n_rows), BF16)],
        compiler_params=_params(("parallel", "parallel"),
                                [n_rows * half * 2, 2 * half * 4, 2 * half * HEAD_DIM * 2],
                                4 * n_rows * half * 4),
        name="compress",
    )(x_slabs, pe, w1, w2)


def _cmp_attn_kernel(q_ref, kc_ref, vct_ref, strip_ref, rb_ref, gate_ref, o_ref, sel_ref, pg_scr):
    g = pl.program_id(0)
    qi = pl.program_id(1)
    n_rows = kc_ref.shape[1]
    n_sel = n_rows // (L_SEL // STRIDE_CMP)
    tq = CMP_TQ
    kc = kc_ref[0]
    vct = vct_ref[0]
    t_row = qi * tq + lax.broadcasted_iota(jnp.int32, (1, tq), 1)
    any_valid = jnp.where(t_row >= L_CMP - 1, 1.0, 0.0)
    rpt = CMP_ROWS_PER_TQ
    pg = jnp.zeros((n_rows, tq), F32)
    for r in range(HEADS_PER_GROUP):
        h = g * HEADS_PER_GROUP + r
        q = q_ref[:, r * HEAD_DIM:(r + 1) * HEAD_DIM]
        st = lax.dot_general(kc, q, (((1,), (1,)), ((), ())), preferred_element_type=F32)
        far = rb_ref[NUM_BUCKETS - 1, h]
        strip_prev = strip_ref[r, 0:rpt, :]
        strip_diag = strip_ref[r, rpt:2 * rpt, :]
        pieces = []
        for ci in range(n_rows // rpt):
            rest = jnp.where(ci < qi - 1, far, NEG_INF)
            pieces.append(jnp.where(ci == qi - 1, strip_prev, jnp.where(ci == qi, strip_diag, rest)))
        s = st * SCALE + jnp.concatenate(pieces, axis=0)
        m = jnp.max(s, axis=0, keepdims=True)
        e = jnp.exp(s - m)
        p = e * (any_valid / jnp.sum(e, axis=0, keepdims=True))
        pg = pg + p
        ot = jnp.dot(vct, p.astype(BF16), preferred_element_type=F32) * gate_ref[0, 0, r:r + 1, :]
        o_ref[:, r * HEAD_DIM:(r + 1) * HEAD_DIM] = ot.T.astype(o_ref.dtype)

    pad = V7X_SUBLANES
    lane_tiles = tq // V7X_LANES
    for c in range(lane_tiles):
        pg_scr[c, 0:pad, :] = jnp.zeros((pad, V7X_LANES), F32)
        pg_scr[c, pad:pad + n_rows, :] = pg[:, c * V7X_LANES:(c + 1) * V7X_LANES]
    per_sel = L_SEL // STRIDE_CMP

    def rows(off):
        return jnp.concatenate(
            [pg_scr[c, pl.ds(pad + off, n_sel, stride=per_sel), :] for c in range(lane_tiles)], axis=1)

    imp = (rows(-1) + rows(3)) + 2.0 * (rows(0) + rows(1) + rows(2))

    j = lax.broadcasted_iota(jnp.int32, (n_sel, tq), 0)
    jf = j.astype(F32)
    jt = jnp.right_shift(t_row, int(math.log2(L_SEL)))
    forced = (j == 0) | (j == jt) | (j == jt - 1)
    cand = (j >= 1) & (j <= jt - 2)
    score = jnp.where(cand, imp, -1.0)
    sel = jnp.where(forced, 1.0, 0.0)
    for _ in range(N_SEL - 3):
        mx = jnp.max(score, axis=0, keepdims=True)
        first = jnp.min(jnp.where(score == mx, jf, float(n_sel)), axis=0, keepdims=True)
        pick = (jf == first) & (mx >= 0.0)
        sel = jnp.where(pick, 1.0, sel)
        score = jnp.where(pick, -2.0, score)
    neg = jnp.where(sel > 0.5, 0.0, NEG_INF)
    if n_sel < V7X_LANES:
        neg = jnp.concatenate([neg, jnp.zeros((V7X_LANES - n_sel, tq), F32)], axis=0)
    sel_ref[0] = neg.T.astype(sel_ref.dtype)


def _cmp_attn(proj, kc, vct, strip, rel_bias, gates_t, t):
    n_rows = kc.shape[1]
    n_sel = t // L_SEL
    g = A_KV_GROUPS
    tq = CMP_TQ
    return pl.pallas_call(
        _cmp_attn_kernel,
        grid=(g, t // tq),
        in_specs=[pl.BlockSpec((tq, GROUP_Q_WIDTH), lambda gi, qi: (qi, gi)),
                  pl.BlockSpec((1, n_rows, HEAD_DIM), lambda gi, qi: (gi, 0, 0)),
                  pl.BlockSpec((1, HEAD_DIM, n_rows), lambda gi, qi: (g + gi, 0, 0)),
                  pl.BlockSpec((HEADS_PER_GROUP, 2 * CMP_ROWS_PER_TQ, tq), lambda gi, qi: (gi, 0, 0)),
                  pl.BlockSpec(memory_space=pltpu.SMEM),
                  pl.BlockSpec((1, 1, V7X_SUBLANES, tq), lambda gi, qi: (0, gi, 0, qi))],
        out_specs=[pl.BlockSpec((tq, GROUP_Q_WIDTH), lambda gi, qi: (qi, gi)),
                   pl.BlockSpec((1, tq, V7X_LANES), lambda gi, qi: (gi, qi, 0))],
        out_shape=[jax.ShapeDtypeStruct((t, A_WIDTH), BF16),
                   jax.ShapeDtypeStruct((g, t, V7X_LANES), BF16)],
        scratch_shapes=[pltpu.VMEM((tq // V7X_LANES, n_rows + V7X_SUBLANES, V7X_LANES), F32)],
        compiler_params=_params(("parallel", "arbitrary"),
                                [tq * GROUP_Q_WIDTH * 2 * 2, n_rows * HEAD_DIM * 2 * 2, n_sel * tq * 4],
                                12 * n_rows * tq * 4),
        name="cmp_attn",
    )(proj, kc, vct, strip, rel_bias, gates_t)


def _softmax_step(r, z, z_max, shift2, vt, m_scr, l_scr, acc_scr):
    m_prev = m_scr[r]
    m_new = jnp.maximum(m_prev, z_max + shift2)
    p = jnp.exp2(z - (m_new - shift2))
    alpha = jnp.exp2(m_prev - m_new)
    l_scr[r] = alpha * l_scr[r] + jnp.sum(p, axis=0, keepdims=True)
    acc_scr[r] = acc_scr[r] * alpha + jnp.dot(vt, p.astype(BF16), preferred_element_type=F32)
    m_scr[r] = m_new


def _pipelined_tile(kvj, kvj_next, logits_head, softmax_head):
    last = HEADS_PER_GROUP - 1
    for r in range(HEADS_PER_GROUP):
        if r < last:
            logits_head(r + 1, kvj)
        elif kvj_next is not None:
            logits_head(0, kvj_next)
        softmax_head(r, kvj)


def _softmax_init(m_scr, l_scr, acc_scr):
    m_scr[...] = jnp.full(m_scr.shape, NEG_INF, F32)
    l_scr[...] = jnp.zeros(l_scr.shape, F32)
    acc_scr[...] = jnp.zeros(acc_scr.shape, F32)


def _softmax_finish(gate_ref, o_ref, l_scr, acc_scr):
    for r in range(HEADS_PER_GROUP):
        ot = acc_scr[r] * (gate_ref[0, 0, r:r + 1, :] / l_scr[r])
        o_ref[:, r * HEAD_DIM:(r + 1) * HEAD_DIM] = ot.T.astype(o_ref.dtype)


def _nt_dot(k, q):
    return lax.dot_general(k, q, (((1,), (1,)), ((), ())), preferred_element_type=F32)


def _sel_attn_kernel(q_ref, k_ref, e_ref, vt_ref, sel_ref, bt_diag_ref, bt_prev_ref, rb_ref, gate_ref, o_ref,
                     qaug_scr, z_scr, zmax_scr, m_scr, l_scr, acc_scr):
    g = pl.program_id(0)
    qi = pl.program_id(1)
    tk = ATTN_TILE
    _softmax_init(m_scr, l_scr, acc_scr)
    for r in range(HEADS_PER_GROUP):
        qaug_scr[r, :, 0:HEAD_DIM] = q_ref[:, r * HEAD_DIM:(r + 1) * HEAD_DIM]
        qaug_scr[r, :, HEAD_DIM:2 * HEAD_DIM] = sel_ref[0]

    def logits_head(r, kvj):
        rows = pl.ds(pl.multiple_of(kvj * tk, tk), tk)
        k_aug = jnp.concatenate([k_ref[rows, :], e_ref[rows, :]], axis=1)
        z = _nt_dot(k_aug, qaug_scr[r]) * SCALE_LOG2
        z_scr[r] = z
        zmax_scr[r] = jnp.max(z, axis=0, keepdims=True)

    def softmax_near(bt_ref):
        def softmax_head(r, kvj):
            z = z_scr[r] + bt_ref[r, 0]
            _softmax_step(r, z, jnp.max(z, axis=0, keepdims=True), 0.0, vt_ref[0, kvj], m_scr, l_scr, acc_scr)
        return softmax_head

    def softmax_far(r, kvj):
        far2 = rb_ref[NUM_BUCKETS - 1, g * HEADS_PER_GROUP + r] * LOG2_E
        _softmax_step(r, z_scr[r], zmax_scr[r], far2, vt_ref[0, kvj], m_scr, l_scr, acc_scr)

    logits_head(0, qi)
    _pipelined_tile(qi, jnp.maximum(qi - 1, 0), logits_head, softmax_near(bt_diag_ref))

    @pl.when(qi >= 1)
    def _():
        _pipelined_tile(qi - 1, 0, logits_head, softmax_near(bt_prev_ref))

    n_far = jnp.maximum(qi - 1, 0)

    def far_tile(kvj, carry):
        _pipelined_tile(kvj, jnp.minimum(kvj + 1, n_far - 1), logits_head, softmax_far)
        return carry

    lax.fori_loop(0, n_far, far_tile, 0)
    _softmax_finish(gate_ref, o_ref, l_scr, acc_scr)


def _win_attn_kernel(q_ref, k_ref, vt_ref, bt_diag_ref, bt_prev_ref, gate_ref, o_ref,
                     z_scr, zmax_scr, m_scr, l_scr, acc_scr):
    qi = pl.program_id(1)
    tk = ATTN_TILE
    _softmax_init(m_scr, l_scr, acc_scr)
    del zmax_scr

    def logits_head(r, kvj):
        k = k_ref[pl.ds(pl.multiple_of(kvj * tk, tk), tk), :]
        z_scr[r] = _nt_dot(k, q_ref[:, r * HEAD_DIM:(r + 1) * HEAD_DIM]) * SCALE_LOG2

    def softmax_near(bt_ref):
        def softmax_head(r, kvj):
            z = z_scr[r] + bt_ref[r, 0]
            _softmax_step(r, z, jnp.max(z, axis=0, keepdims=True), 0.0, vt_ref[0, kvj], m_scr, l_scr, acc_scr)
        return softmax_head

    logits_head(0, qi)
    _pipelined_tile(qi, jnp.maximum(qi - 1, 0), logits_head, softmax_near(bt_diag_ref))

    @pl.when(qi >= 1)
    def _():
        _pipelined_tile(qi - 1, None, logits_head, softmax_near(bt_prev_ref))

    _softmax_finish(gate_ref, o_ref, l_scr, acc_scr)


def _attn_scratch(tq):
    return [pltpu.VMEM((HEADS_PER_GROUP, ATTN_TILE, tq), F32), pltpu.VMEM((HEADS_PER_GROUP, 1, tq), F32),
            pltpu.VMEM((HEADS_PER_GROUP, 1, tq), F32), pltpu.VMEM((HEADS_PER_GROUP, 1, tq), F32),
            pltpu.VMEM((HEADS_PER_GROUP, HEAD_DIM, tq), F32)]


def _attn_common_specs(t, col_k, branch):
    tq = ATTN_TILE
    k_col0 = col_k // HEAD_DIM
    q_spec = pl.BlockSpec((tq, GROUP_Q_WIDTH), lambda gi, qi: (qi, gi))
    k_spec = pl.BlockSpec((t, HEAD_DIM), lambda gi, qi: (0, k_col0 + gi))
    vt_spec = pl.BlockSpec((1, t // ATTN_TILE, HEAD_DIM, ATTN_TILE), lambda gi, qi: (gi, 0, 0, 0))
    gate_spec = pl.BlockSpec((1, 1, V7X_SUBLANES, tq), lambda gi, qi: (branch, gi, 0, qi))
    out_spec = pl.BlockSpec((tq, GROUP_Q_WIDTH), lambda gi, qi: (qi, gi))
    return q_spec, k_spec, vt_spec, gate_spec, out_spec


def _bt_spec(kind):
    return pl.BlockSpec((HEADS_PER_GROUP, 1, ATTN_TILE, ATTN_TILE), lambda gi, qi: (gi, kind, 0, 0))


def _attn_block_bytes(t):
    tq = ATTN_TILE
    return [tq * GROUP_Q_WIDTH * 2 * 2, t * HEAD_DIM * 2 * 2, 2 * HEADS_PER_GROUP * tq * tq * 4]


def _sel_attn(proj, col_k, vt, sel_neg, bt, rel_bias, gates_t, t):
    tq = ATTN_TILE
    q_spec, k_spec, vt_spec, gate_spec, out_spec = _attn_common_specs(t, col_k, 1)
    sel_spec = pl.BlockSpec((1, tq, V7X_LANES), lambda gi, qi: (gi, qi, 0))
    e_onehot = (jnp.arange(t, dtype=jnp.int32)[:, None] // L_SEL
                == jnp.arange(V7X_LANES, dtype=jnp.int32)[None, :]).astype(BF16)
    e_spec = pl.BlockSpec((t, V7X_LANES), lambda gi, qi: (0, 0))
    return pl.pallas_call(
        _sel_attn_kernel,
        grid=(A_KV_GROUPS, t // tq),
        in_specs=[q_spec, k_spec, e_spec, vt_spec, sel_spec, _bt_spec(0), _bt_spec(1),
                  pl.BlockSpec(memory_space=pltpu.SMEM), gate_spec],
        out_specs=out_spec,
        out_shape=jax.ShapeDtypeStruct((t, A_WIDTH), BF16),
        scratch_shapes=[pltpu.VMEM((HEADS_PER_GROUP, tq, 2 * HEAD_DIM), BF16)] + _attn_scratch(tq),
        compiler_params=_params(("parallel", "arbitrary"),
                                _attn_block_bytes(t) + [t * V7X_LANES * 2, tq * V7X_LANES * 2],
                                (8 + HEADS_PER_GROUP) * tq * tq * 4),
        name="sel_attn",
    )(proj, proj, e_onehot, vt, sel_neg, bt, bt, rel_bias, gates_t)


def _win_attn(proj, col_k, vt, bt, gates_t, t):
    tq = ATTN_TILE
    q_spec, k_spec, vt_spec, gate_spec, out_spec = _attn_common_specs(t, col_k, 2)
    return pl.pallas_call(
        _win_attn_kernel,
        grid=(A_KV_GROUPS, t // tq),
        in_specs=[q_spec, k_spec, vt_spec, _bt_spec(0), _bt_spec(2), gate_spec],
        out_specs=out_spec,
        out_shape=jax.ShapeDtypeStruct((t, A_WIDTH), BF16),
        scratch_shapes=_attn_scratch(tq),
        compiler_params=_params(("parallel", "arbitrary"), _attn_block_bytes(t), (8 + HEADS_PER_GROUP) * tq * tq * 4),
        name="win_attn",
    )(proj, proj, vt, bt, bt, gates_t)


def _gmlp_kernel(u_ref, v_ref, z_ref, ng_ref, w_ref, bt_ref, o_ref):
    v = v_ref[...].astype(F32)
    ms = jnp.mean(v * v, axis=-1, keepdims=True)
    vn = (v * lax.rsqrt(ms + NORM_EPS) * ng_ref[...]).astype(BF16)
    p_idx = lax.broadcasted_iota(jnp.int32, (CHUNK, CHUNK), 0)
    q_idx = lax.broadcasted_iota(jnp.int32, (CHUNK, CHUNK), 1)
    causal = q_idx <= p_idx
    gd = B_WIDTH // B_GROUPS
    for gg in range(B_GROUPS):
        cols = slice(gg * gd, (gg + 1) * gd)
        w = jnp.where(causal, w_ref[gg], jnp.zeros((), w_ref.dtype))
        f = jnp.dot(w, vn[:, cols], preferred_element_type=F32) + bt_ref[:, gg:gg + 1]
        o_ref[:, cols] = (u_ref[:, cols].astype(F32) * f * z_ref[:, cols].astype(F32)).astype(o_ref.dtype)


def _gmlp(proj, col_u, col_v, col_z, norm_g, w_s, b_t, t):
    bw = B_WIDTH

    def col_spec(col):
        return pl.BlockSpec((CHUNK, bw), lambda i: (i, col // bw))

    return pl.pallas_call(
        _gmlp_kernel,
        grid=(t // CHUNK,),
        in_specs=[col_spec(col_u), col_spec(col_v), col_spec(col_z),
                  pl.BlockSpec((1, bw), lambda i: (0, 0)),
                  pl.BlockSpec((B_GROUPS, CHUNK, CHUNK), lambda i: (0, 0, 0)),
                  pl.BlockSpec((CHUNK, B_GROUPS), lambda i: (0, 0))],
        out_specs=pl.BlockSpec((CHUNK, bw), lambda i: (i, 0)),
        out_shape=jax.ShapeDtypeStruct((t, bw), BF16),
        compiler_params=_params(("parallel",), [CHUNK * bw * 2] * 4 + [B_GROUPS * CHUNK * CHUNK * 2],
                                4 * CHUNK * bw * 4),
        name="gmlp",
    )(proj, proj, proj, norm_g.reshape(1, bw), w_s, b_t)


def _merge_kernel(oc_ref, os_ref, ow_ref, za_ref, ob_ref, wa_ref, wb_ref, ma_ref, mb_ref, o_ref, lhs_scr):
    @pl.when(pl.program_id(1) == 0)
    def _():
        o_a = oc_ref[...].astype(F32) + os_ref[...].astype(F32) + ow_ref[...].astype(F32)
        lhs_scr[...] = (o_a * za_ref[...].astype(F32)).astype(BF16)

    ya = jnp.dot(lhs_scr[...], wa_ref[...], preferred_element_type=F32)
    yb = jnp.dot(ob_ref[...], wb_ref[...], preferred_element_type=F32)
    o_ref[...] = (ma_ref[...].astype(F32) * ya + mb_ref[...].astype(F32) * yb).astype(o_ref.dtype)


def _merge(o_cmp, o_sel, o_win, proj, col_za, col_ma, col_mb, o_b, w_a, w_b, t, d):
    tm = min(512, t)
    tn = min(512, d)
    aw = A_WIDTH
    bw = B_WIDTH
    row_a = pl.BlockSpec((tm, aw), lambda i, j: (i, 0))
    return pl.pallas_call(
        _merge_kernel,
        grid=(t // tm, d // tn),
        in_specs=[row_a, row_a, row_a,
                  pl.BlockSpec((tm, aw), lambda i, j: (i, col_za // aw)),
                  pl.BlockSpec((tm, bw), lambda i, j: (i, 0)),
                  pl.BlockSpec((aw, tn), lambda i, j: (0, j)),
                  pl.BlockSpec((bw, tn), lambda i, j: (0, j)),
                  pl.BlockSpec((tm, tn), lambda i, j: (i, col_ma // tn + j)),
                  pl.BlockSpec((tm, tn), lambda i, j: (i, col_mb // tn + j))],
        out_specs=pl.BlockSpec((tm, tn), lambda i, j: (i, j)),
        out_shape=jax.ShapeDtypeStruct((t, d), BF16),
        scratch_shapes=[pltpu.VMEM((tm, aw), BF16)],
        compiler_params=_params(("parallel", "arbitrary"),
                                [tm * aw * 2] * 5 + [aw * tn * 2, bw * tn * 2] + [tm * tn * 2] * 3,
                                tm * aw * 2 + 4 * tm * tn * 4),
        name="merge",
    )(o_cmp, o_sel, o_win, proj, o_b, w_a, w_b, proj, proj)


def kernel(x, rel_bias, pre_norm, w_in, cmp_pe_k, cmp_w1_k, cmp_w2_k, cmp_pe_v, cmp_w1_v, cmp_w2_v,
           w_out_a, sgu_norm, sgu_w, sgu_b, w_out_b, w_out, post_norm):
    batch, t, d = x.shape
    assert batch == 1 and t % ATTN_TILE == 0 and t // L_SEL >= N_SEL
    depth = w_in.shape[0]
    g = A_KV_GROUPS
    xs = x.reshape(t, d)

    src_kv = A_WIDTH
    src_gates = src_kv + 6 * A_KV_WIDTH
    src_za = src_gates + N_GATE_COLS
    src_ma = src_za + A_WIDTH + 3 * B_WIDTH
    col_za = A_WIDTH
    col_u = col_za + A_WIDTH
    col_v = col_u + B_WIDTH
    col_zb = col_v + B_WIDTH
    col_kv = col_zb + B_WIDTH
    col_ma = col_kv + 6 * A_KV_WIDTH
    col_mb = col_ma + d
    n_main = col_mb + d
    col_acts = [(0, col_za, "none"), (col_za, col_u, "silu"), (col_u, col_zb, "gelu"),
                (col_zb, col_kv, "silu"), (col_kv, col_ma, "none"), (col_ma, n_main, "sigmoid")]

    bias_tiles = _bias_tiles(rel_bias)
    bias_strip = _bias_strip(rel_bias)
    slab = STRIDE_CMP * HEAD_DIM

    for layer in range(depth):
        w = w_in[layer]
        w_main = jnp.concatenate([w[:, :src_kv], w[:, src_za:src_ma], w[:, src_kv:src_gates], w[:, src_ma:]],
                                 axis=1).astype(BF16)
        w_gate = jnp.pad(w[:, src_gates:src_za], ((0, 0), (0, V7X_LANES - N_GATE_COLS))).astype(BF16)

        h = _rmsnorm(xs, pre_norm[layer], BF16)
        proj = _mm(h, w_main, BF16, col_acts=col_acts, name="in_proj")
        gates = _mm(h, w_gate, F32, act="sigmoid", name="gate_proj")
        gates_t = gates[:, :N_GATE_COLS].reshape(t, g, HEADS_PER_GROUP, 3).transpose(3, 1, 2, 0)
        gates_t = jnp.pad(gates_t, ((0, 0), (0, 0), (0, V7X_SUBLANES - HEADS_PER_GROUP), (0, 0)))

        kv = proj[:, col_kv:col_ma].reshape(t, 6, g, HEAD_DIM)
        x_slabs = kv[:, 0:2].reshape(t // STRIDE_CMP, STRIDE_CMP, 2 * g, HEAD_DIM)
        x_slabs = x_slabs.transpose(2, 0, 1, 3).reshape(2 * g, t // STRIDE_CMP, slab)
        pe = jnp.stack([cmp_pe_k[layer], cmp_pe_v[layer]]).reshape(2, 2, slab)
        w1 = jnp.stack([cmp_w1_k[layer], cmp_w1_v[layer]]).astype(BF16)
        w2 = jnp.stack([cmp_w2_k[layer], cmp_w2_v[layer]]).astype(BF16)
        kc, kct = _compress(x_slabs, pe, w1, w2)

        def v_tiles(s):
            vt = kv[:, s].reshape(t // ATTN_TILE, ATTN_TILE, g, HEAD_DIM)
            return vt.transpose(2, 0, 3, 1)

        o_cmp, sel_neg = _cmp_attn(proj, kc, kct, bias_strip, rel_bias, gates_t, t)
        o_sel = _sel_attn(proj, col_kv + 2 * A_KV_WIDTH, v_tiles(3), sel_neg, bias_tiles, rel_bias, gates_t, t)
        o_win = _win_attn(proj, col_kv + 4 * A_KV_WIDTH, v_tiles(5), bias_tiles, gates_t, t)

        o_b = _gmlp(proj, col_u, col_v, col_zb, sgu_norm[layer], sgu_w[layer].astype(BF16),
                    sgu_b[layer].T, t)
        merged = _merge(o_cmp, o_sel, o_win, proj, col_za, col_ma, col_mb, o_b,
                        w_out_a[layer].astype(BF16), w_out_b[layer].astype(BF16), t, d)
        y = _mm(merged, w_out[layer].astype(BF16), F32, name="out_proj")
        xs = _post_norm_residual(xs, y, post_norm[layer])
    return xs.reshape(batch, t, d)
```

```python
import functools
import math

import numpy as np
import jax
import jax.numpy as jnp
from jax import lax
from jax.experimental import pallas as pl
from jax.experimental.pallas import tpu as pltpu

F32 = jnp.float32
BF16 = jnp.bfloat16

A_HEADS = 16
A_KV_GROUPS = 4
HEADS_PER_GROUP = A_HEADS // A_KV_GROUPS
HEAD_DIM = 128
A_WIDTH = A_HEADS * HEAD_DIM
A_KV_WIDTH = A_KV_GROUPS * HEAD_DIM
GROUP_Q_WIDTH = HEADS_PER_GROUP * HEAD_DIM
L_CMP = 32
STRIDE_CMP = 16
L_SEL = 64
N_SEL = 16
WINDOW = 512
B_GROUPS = 16
CHUNK = 128
B_WIDTH = 2048
NUM_BUCKETS = 32
MAX_DISTANCE = 128
NORM_EPS = 1e-6
NEG_INF = -1e30
SCALE = HEAD_DIM ** -0.5
LOG2_E = math.log2(math.e)
SCALE_LOG2 = SCALE * LOG2_E
N_GATE_COLS = 3 * A_HEADS

V7X_LANES = 128
V7X_SUBLANES = 8
V7X_SCOPED_VMEM_CAP_BYTES = 60000 * 1024

ATTN_TILE = 512
CMP_TQ = 256
CMP_ROWS_PER_TQ = CMP_TQ // STRIDE_CMP
MM_TM = 1024
MM_TN = 1024

SRC_KV = A_WIDTH
SRC_GATES = SRC_KV + 6 * A_KV_WIDTH
SRC_REST = SRC_GATES + N_GATE_COLS
COL_ZA = 0
COL_U = COL_ZA + A_WIDTH
COL_V = COL_U + B_WIDTH
COL_ZB = COL_V + B_WIDTH
COL_MA = COL_ZB + B_WIDTH


def _vmem_limit(block_bytes, temp_bytes=0):
    need = 2 * sum(block_bytes) + temp_bytes + (4 << 20)
    return int(min(max(need, 16 << 20), V7X_SCOPED_VMEM_CAP_BYTES))


def _params(sem, block_bytes, temp_bytes=0):
    return pltpu.CompilerParams(dimension_semantics=sem,
                                vmem_limit_bytes=_vmem_limit(block_bytes, temp_bytes))


def _t5_thresholds():
    n = np.arange(0, 4 * MAX_DISTANCE)
    max_exact = NUM_BUCKETS // 2
    nf = np.maximum(n, 1).astype(np.float32)
    large = max_exact + (np.log(nf / np.float32(max_exact)) / np.float32(math.log(MAX_DISTANCE / max_exact))
                         * np.float32(NUM_BUCKETS - max_exact)).astype(np.int32)
    bucket = np.where(n < max_exact, n, np.minimum(large, NUM_BUCKETS - 1))
    assert np.all(np.diff(bucket) >= 0) and bucket[-1] == NUM_BUCKETS - 1
    return [int(np.argmax(bucket >= b)) for b in range(1, NUM_BUCKETS)]


_T5_THR = _t5_thresholds()


def _sigmoid(x):
    return 1.0 / (1.0 + jnp.exp(-x))


def _silu(x):
    return x * _sigmoid(x)


def _gelu(x):
    return jax.nn.gelu(x, approximate=True)


_ACTS = {"none": lambda v: v, "silu": _silu, "gelu": _gelu, "sigmoid": _sigmoid,
         "log2_scale": lambda v: v * SCALE_LOG2}


def _rmsnorm_kernel(x_ref, g_ref, o_ref):
    x = x_ref[...]
    ms = jnp.mean(x * x, axis=-1, keepdims=True)
    o_ref[...] = (x * lax.rsqrt(ms + NORM_EPS) * g_ref[...]).astype(o_ref.dtype)


def _rmsnorm(x, g, out_dtype):
    t, d = x.shape
    tm = min(256, t)
    return pl.pallas_call(
        _rmsnorm_kernel,
        grid=(t // tm,),
        in_specs=[pl.BlockSpec((tm, d), lambda i: (i, 0)), pl.BlockSpec((1, d), lambda i: (0, 0))],
        out_specs=pl.BlockSpec((tm, d), lambda i: (i, 0)),
        out_shape=jax.ShapeDtypeStruct((t, d), out_dtype),
        compiler_params=_params(("parallel",), [tm * d * 4, tm * d * 4]),
        name="rmsnorm",
    )(x, g.reshape(1, d))


def _post_kernel(x_ref, y_ref, g_ref, gn_ref, o_ref, h_ref):
    y = y_ref[...]
    ms = jnp.mean(y * y, axis=-1, keepdims=True)
    x = x_ref[...] + y * lax.rsqrt(ms + NORM_EPS) * g_ref[...]
    o_ref[...] = x
    ms_x = jnp.mean(x * x, axis=-1, keepdims=True)
    h_ref[...] = (x * lax.rsqrt(ms_x + NORM_EPS) * gn_ref[...]).astype(h_ref.dtype)


def _post_norm_residual(x, y, g, g_next):
    t, d = x.shape
    tm = min(256, t)
    row = pl.BlockSpec((tm, d), lambda i: (i, 0))
    vec = pl.BlockSpec((1, d), lambda i: (0, 0))
    return pl.pallas_call(
        _post_kernel,
        grid=(t // tm,),
        in_specs=[row, row, vec, vec],
        out_specs=[row, row],
        out_shape=[jax.ShapeDtypeStruct((t, d), F32), jax.ShapeDtypeStruct((t, d), BF16)],
        compiler_params=_params(("parallel",), [tm * d * 4] * 4),
        name="post_norm_residual",
    )(x, y, g.reshape(1, d), g_next.reshape(1, d))


def _cast_kernel(w_ref, o_ref):
    o_ref[...] = w_ref[...].astype(o_ref.dtype)


def _shift_cast_kernel(w_ref, tail_ref, o_ref, *, shift):
    x = jnp.concatenate([w_ref[0], tail_ref[0]], axis=-1)
    o_ref[0] = x[:, shift:shift + o_ref.shape[2]].astype(o_ref.dtype)


def _cast_cols(w, col0, ncols, name):
    n_layers, k, _ = w.shape
    tk = min(1024, k)
    tn = next(c for c in (1024, 512, 256, 128) if ncols % c == 0 and (col0 - col0 % V7X_LANES) % c == 0)
    shift = col0 % V7X_LANES
    base = (col0 - shift) // tn
    grid = (n_layers, k // tk, ncols // tn)
    main = pl.BlockSpec((1, tk, tn), lambda l, i, j: (l, i, base + j))
    out = pl.BlockSpec((1, tk, tn), lambda l, i, j: (l, i, j))
    common = dict(grid=grid, out_specs=out, out_shape=jax.ShapeDtypeStruct((n_layers, k, ncols), BF16),
                  compiler_params=_params(("parallel",) * 3, [tk * tn * 4, tk * V7X_LANES * 4, tk * tn * 2],
                                          2 * tk * tn * 4),
                  name=name)
    if shift == 0:
        return pl.pallas_call(_cast_kernel, in_specs=[main], **common)(w)
    lanes_per_block = tn // V7X_LANES
    tail = pl.BlockSpec((1, tk, V7X_LANES), lambda l, i, j: (l, i, (base + j + 1) * lanes_per_block))
    return pl.pallas_call(functools.partial(_shift_cast_kernel, shift=shift),
                          in_specs=[main, tail], **common)(w, w)


def _mm_kernel(a_ref, b_ref, o_ref, *, act_ranges):
    acc = jnp.dot(a_ref[...], b_ref[0], preferred_element_type=F32)
    if len(act_ranges) == 1:
        o_ref[...] = _ACTS[act_ranges[0][2]](acc).astype(o_ref.dtype)
        return
    j = pl.program_id(1)
    for lo, hi, act in act_ranges:
        @pl.when((j >= lo) & (j < hi))
        def _(act=act):
            o_ref[...] = _ACTS[act](acc).astype(o_ref.dtype)


def _mm(a, b, layer, out_dtype, act="none", col_acts=None, name="mm"):
    m, k = a.shape
    n = b.shape[2]
    tm = min(MM_TM, m)
    tn = next(c for c in (MM_TN, 512, 256, 128) if n % c == 0)
    if col_acts is None:
        act_ranges = ((0, n // tn, act),)
    else:
        assert all(lo % tn == 0 and hi % tn == 0 for lo, hi, _ in col_acts)
        act_ranges = tuple((lo // tn, hi // tn, ac) for lo, hi, ac in col_acts)
    osz = jnp.dtype(out_dtype).itemsize
    return pl.pallas_call(
        functools.partial(_mm_kernel, act_ranges=act_ranges),
        grid=(m // tm, n // tn),
        in_specs=[pl.BlockSpec((tm, k), lambda i, j: (i, 0)),
                  pl.BlockSpec((1, k, tn), lambda i, j: (layer, 0, j))],
        out_specs=pl.BlockSpec((tm, tn), lambda i, j: (i, j)),
        out_shape=jax.ShapeDtypeStruct((m, n), out_dtype),
        compiler_params=_params(("parallel", "arbitrary"), [tm * k * 2, k * tn * 2, tm * tn * osz],
                                2 * tm * tn * 4),
        name=name,
    )(a, b)


def _gate_proj_kernel(a_ref, b_ref, o_ref):
    acc = jnp.dot(a_ref[...], b_ref[0], preferred_element_type=F32)
    o_ref[...] = _sigmoid(acc).T


def _gate_proj(h, w_gate, layer):
    t, k = h.shape
    tm = min(MM_TM, t)
    return pl.pallas_call(
        _gate_proj_kernel,
        grid=(t // tm,),
        in_specs=[pl.BlockSpec((tm, k), lambda i: (i, 0)),
                  pl.BlockSpec((1, k, V7X_LANES), lambda i: (layer, 0, 0))],
        out_specs=pl.BlockSpec((V7X_LANES, tm), lambda i: (0, i)),
        out_shape=jax.ShapeDtypeStruct((V7X_LANES, t), F32),
        compiler_params=_params(("parallel",), [tm * k * 2, k * V7X_LANES * 2, tm * V7X_LANES * 4]),
        name="gate_proj",
    )(h, w_gate)


def _cmp_proj_kernel(a_ref, b_ref, o_ref, scr):
    acc = jnp.dot(a_ref[...], b_ref[0], preferred_element_type=F32)
    n_sg = scr.shape[0]
    slab_rows = o_ref.shape[1]
    for sg in range(n_sg):
        scr[sg] = acc[:, sg * HEAD_DIM:(sg + 1) * HEAD_DIM]
    for sg in range(n_sg):
        for l in range(STRIDE_CMP):
            o_ref[sg, :, l * HEAD_DIM:(l + 1) * HEAD_DIM] = (
                scr[sg, pl.ds(l, slab_rows, stride=STRIDE_CMP), :].astype(o_ref.dtype))


def _cmp_proj(h, w_cmp, layer):
    t, k = h.shape
    tm = min(MM_TM, t)
    n_sg = 2 * A_KV_GROUPS
    n = n_sg * HEAD_DIM
    slab = STRIDE_CMP * HEAD_DIM
    return pl.pallas_call(
        _cmp_proj_kernel,
        grid=(t // tm,),
        in_specs=[pl.BlockSpec((tm, k), lambda i: (i, 0)), pl.BlockSpec((1, k, n), lambda i: (layer, 0, 0))],
        out_specs=pl.BlockSpec((n_sg, tm // STRIDE_CMP, slab), lambda i: (0, i, 0)),
        out_shape=jax.ShapeDtypeStruct((n_sg, t // STRIDE_CMP, slab), BF16),
        scratch_shapes=[pltpu.VMEM((n_sg, tm, HEAD_DIM), F32)],
        compiler_params=_params(("parallel",), [tm * k * 2, k * n * 2, tm * n * 2], 3 * tm * n * 4),
        name="cmp_proj",
    )(h, w_cmp)


def _kv_proj_kernel(a_ref, b_ref, k_ref, vt_ref):
    acc = jnp.dot(a_ref[...], b_ref[0], preferred_element_type=F32)
    k_ref[...] = acc[:, :A_KV_WIDTH].astype(k_ref.dtype)
    for g in range(A_KV_GROUPS):
        for s in range(vt_ref.shape[2]):
            v = acc[s * ATTN_TILE:(s + 1) * ATTN_TILE, A_KV_WIDTH + g * HEAD_DIM:A_KV_WIDTH + (g + 1) * HEAD_DIM]
            vt_ref[0, g, s] = v.T.astype(vt_ref.dtype)


def _kv_proj(h, w_kv, layer, n_sets):
    t, k = h.shape
    tm = min(MM_TM, t)
    tiles = tm // ATTN_TILE
    return pl.pallas_call(
        _kv_proj_kernel,
        grid=(t // tm, n_sets),
        in_specs=[pl.BlockSpec((tm, k), lambda i, j: (i, 0)),
                  pl.BlockSpec((1, k, 2 * A_KV_WIDTH), lambda i, j: (layer, 0, j))],
        out_specs=[pl.BlockSpec((tm, A_KV_WIDTH), lambda i, j: (i, j)),
                   pl.BlockSpec((1, A_KV_GROUPS, tiles, HEAD_DIM, ATTN_TILE), lambda i, j: (j, 0, i, 0, 0))],
        out_shape=[jax.ShapeDtypeStruct((t, n_sets * A_KV_WIDTH), BF16),
                   jax.ShapeDtypeStruct((n_sets, A_KV_GROUPS, t // ATTN_TILE, HEAD_DIM, ATTN_TILE), BF16)],
        compiler_params=_params(("parallel", "arbitrary"),
                                [tm * k * 2, k * 2 * A_KV_WIDTH * 2, tm * 2 * A_KV_WIDTH * 2],
                                3 * tm * 2 * A_KV_WIDTH * 4),
        name="kv_proj",
    )(h, w_kv)


def _t5_bias_rel(dist, rb_ref, h):
    far = rb_ref[NUM_BUCKETS - 1, h]
    val = jnp.zeros(dist.shape, F32)
    for b in range(NUM_BUCKETS - 2, -1, -1):
        val = jnp.where(dist < _T5_THR[b], (rb_ref[b, h] - far) * LOG2_E, val)
    return val


def _bias_tile_kernel(rb_ref, o_ref):
    h = pl.program_id(0)
    kind = pl.program_id(1)
    shape = (ATTN_TILE, ATTN_TILE)
    key = lax.broadcasted_iota(jnp.int32, shape, 0)
    qry = lax.broadcasted_iota(jnp.int32, shape, 1)
    dist = jnp.where(kind == 0, 0, ATTN_TILE) + qry - key
    limit = jnp.where(kind == 2, WINDOW, 4 * ATTN_TILE)
    o_ref[0, 0] = jnp.where((dist >= 0) & (dist < limit), _t5_bias_rel(dist, rb_ref, h), NEG_INF)


def _bias_tiles(rel_bias):
    return pl.pallas_call(
        _bias_tile_kernel,
        grid=(A_HEADS, 3),
        in_specs=[pl.BlockSpec(memory_space=pltpu.SMEM)],
        out_specs=pl.BlockSpec((1, 1, ATTN_TILE, ATTN_TILE), lambda h, k: (h, k, 0, 0)),
        out_shape=jax.ShapeDtypeStruct((A_HEADS, 3, ATTN_TILE, ATTN_TILE), F32),
        compiler_params=_params(("parallel", "parallel"), [ATTN_TILE * ATTN_TILE * 4]),
        name="bias_tiles",
    )(rel_bias)


def _bias_strip_kernel(rb_ref, o_ref):
    h = pl.program_id(0)
    shape = (2 * CMP_ROWS_PER_TQ, CMP_TQ)
    c = lax.broadcasted_iota(jnp.int32, shape, 0)
    a = lax.broadcasted_iota(jnp.int32, shape, 1)
    dist = a - STRIDE_CMP * c + (CMP_TQ - (L_CMP - 1))
    o_ref[0] = jnp.where(dist >= 0, _t5_bias_rel(dist, rb_ref, h), NEG_INF)


def _bias_strip(rel_bias):
    rows = 2 * CMP_ROWS_PER_TQ
    return pl.pallas_call(
        _bias_strip_kernel,
        grid=(A_HEADS,),
        in_specs=[pl.BlockSpec(memory_space=pltpu.SMEM)],
        out_specs=pl.BlockSpec((1, rows, CMP_TQ), lambda h: (h, 0, 0)),
        out_shape=jax.ShapeDtypeStruct((A_HEADS, rows, CMP_TQ), F32),
        compiler_params=_params(("parallel",), [rows * CMP_TQ * 4]),
        name="bias_strip",
    )(rel_bias)


def _compress_kernel(x_ref, pe_ref, w1_ref, w2_ref, o_ref, ot_ref):
    x = x_ref[0].astype(F32)
    half = x.shape[1]
    lo = (x + pe_ref[0, 0:1, :]).astype(BF16)
    hi = (x + pe_ref[0, 1:2, :]).astype(BF16)
    a = jnp.dot(lo, w1_ref[0, :half, :], preferred_element_type=F32)
    b = jnp.dot(hi, w1_ref[0, half:, :], preferred_element_type=F32)
    n_rows = x.shape[0]
    hidden = a + pltpu.roll(b, n_rows - 1, 0)
    out = jnp.dot(_silu(hidden).astype(BF16), w2_ref[0], preferred_element_type=F32)
    o_ref[0] = out.astype(o_ref.dtype)
    ot_ref[0] = out.T.astype(ot_ref.dtype)


def _compress(x_slabs, pe, w1, w2):
    ng, n_rows, half = x_slabs.shape
    g = A_KV_GROUPS
    return pl.pallas_call(
        _compress_kernel,
        grid=(2, g),
        in_specs=[pl.BlockSpec((1, n_rows, half), lambda s, i: (s * g + i, 0, 0)),
                  pl.BlockSpec((1, 2, half), lambda s, i: (s, 0, 0)),
                  pl.BlockSpec((1, 2 * half, HEAD_DIM), lambda s, i: (s, 0, 0)),
                  pl.BlockSpec((1, HEAD_DIM, HEAD_DIM), lambda s, i: (s, 0, 0))],
        out_specs=[pl.BlockSpec((1, n_rows, HEAD_DIM), lambda s, i: (s * g + i, 0, 0)),
                   pl.BlockSpec((1, HEAD_DIM, n_rows), lambda s, i: (s * g + i, 0, 0))],
        out_shape=[jax.ShapeDtypeStruct((ng, n_rows, HEAD_DIM), BF16),
                   jax.ShapeDtypeStruct((ng, HEAD_DIM, n_rows), BF16)],
        compiler_params=_params(("parallel", "parallel"),
                                [n_rows * half * 2, 2 * half * 4, 2 * half * HEAD_DIM * 2],
                                4 * n_rows * half * 4),
        name="compress",
    )(x_slabs, pe, w1, w2)


def _nt_dot(k, q):
    return lax.dot_general(k, q, (((1,), (1,)), ((), ())), preferred_element_type=F32)


def _cmp_attn_kernel(q_ref, kc_ref, vct_ref, strip_ref, gate_ref, o_ref, sel_ref, s_scr, pg_scr):
    qi = pl.program_id(1)
    n_rows = kc_ref.shape[1]
    n_sel = n_rows // (L_SEL // STRIDE_CMP)
    tq = CMP_TQ
    rpt = CMP_ROWS_PER_TQ
    t_row = qi * tq + lax.broadcasted_iota(jnp.int32, (1, tq), 1)
    any_valid = jnp.where(t_row >= L_CMP - 1, 1.0, 0.0)
    key_live = lax.broadcasted_iota(jnp.int32, (n_rows, tq), 0) < (qi + 1) * rpt
    pg = jnp.zeros((n_rows, tq), F32)
    for r in range(HEADS_PER_GROUP):
        q = q_ref[:, r * HEAD_DIM:(r + 1) * HEAD_DIM]
        s_scr[...] = jnp.where(key_live, _nt_dot(kc_ref[0], q), NEG_INF)
        first = pl.multiple_of(jnp.maximum(qi - 1, 0) * rpt, rpt)

        @pl.when(qi >= 1)
        def _():
            s_scr[pl.ds(first, 2 * rpt), :] = s_scr[pl.ds(first, 2 * rpt), :] + strip_ref[r]

        @pl.when(qi == 0)
        def _():
            s_scr[0:rpt, :] = s_scr[0:rpt, :] + strip_ref[r, rpt:2 * rpt, :]

        s = s_scr[...]
        m = jnp.max(s, axis=0, keepdims=True)
        e = jnp.exp2(s - m)
        p = e * (any_valid / jnp.sum(e, axis=0, keepdims=True))
        pg = pg + p
        ot = jnp.dot(vct_ref[0], p.astype(BF16), preferred_element_type=F32) * gate_ref[r:r + 1, :]
        o_ref[:, r * HEAD_DIM:(r + 1) * HEAD_DIM] = ot.T.astype(o_ref.dtype)

    pad = V7X_SUBLANES
    lane_tiles = tq // V7X_LANES
    for c in range(lane_tiles):
        pg_scr[c, 0:pad, :] = jnp.zeros((pad, V7X_LANES), F32)
        pg_scr[c, pad:pad + n_rows, :] = pg[:, c * V7X_LANES:(c + 1) * V7X_LANES]
    per_sel = L_SEL // STRIDE_CMP

    def rows(off):
        return jnp.concatenate(
            [pg_scr[c, pl.ds(pad + off, n_sel, stride=per_sel), :] for c in range(lane_tiles)], axis=1)

    imp = (rows(-1) + rows(3)) + 2.0 * (rows(0) + rows(1) + rows(2))

    j = lax.broadcasted_iota(jnp.int32, (n_sel, tq), 0)
    jf = j.astype(F32)
    jt = jnp.right_shift(t_row, int(math.log2(L_SEL)))
    forced = (j == 0) | (j == jt) | (j == jt - 1)
    cand = (j >= 1) & (j <= jt - 2)
    score = jnp.where(cand, imp, -1.0)
    sel = jnp.where(forced, 1.0, 0.0)
    for _ in range(N_SEL - 3):
        mx = jnp.max(score, axis=0, keepdims=True)
        first_j = jnp.min(jnp.where(score == mx, jf, float(n_sel)), axis=0, keepdims=True)
        pick = (jf == first_j) & (mx >= 0.0)
        sel = jnp.where(pick, 1.0, sel)
        score = jnp.where(pick, -2.0, score)
    neg = jnp.where(sel > 0.5, 0.0, NEG_INF)
    if n_sel < V7X_LANES:
        neg = jnp.concatenate([neg, jnp.zeros((V7X_LANES - n_sel, tq), F32)], axis=0)
    sel_ref[0] = neg.T.astype(sel_ref.dtype)


def _cmp_attn(q2, kc, vct, strip, gates_t, t):
    n_rows = kc.shape[1]
    g = A_KV_GROUPS
    tq = CMP_TQ
    return pl.pallas_call(
        _cmp_attn_kernel,
        grid=(g, t // tq),
        in_specs=[pl.BlockSpec((tq, GROUP_Q_WIDTH), lambda gi, qi: (qi, gi)),
                  pl.BlockSpec((1, n_rows, HEAD_DIM), lambda gi, qi: (gi, 0, 0)),
                  pl.BlockSpec((1, HEAD_DIM, n_rows), lambda gi, qi: (g + gi, 0, 0)),
                  pl.BlockSpec((HEADS_PER_GROUP, 2 * CMP_ROWS_PER_TQ, tq), lambda gi, qi: (gi, 0, 0)),
                  _gate_spec(0, tq)],
        out_specs=[pl.BlockSpec((tq, GROUP_Q_WIDTH), lambda gi, qi: (qi, gi)),
                   pl.BlockSpec((1, tq, V7X_LANES), lambda gi, qi: (gi, qi, 0))],
        out_shape=[jax.ShapeDtypeStruct((t, A_WIDTH), BF16),
                   jax.ShapeDtypeStruct((g, t, V7X_LANES), BF16)],
        scratch_shapes=[pltpu.VMEM((n_rows, tq), F32),
                        pltpu.VMEM((tq // V7X_LANES, n_rows + V7X_SUBLANES, V7X_LANES), F32)],
        compiler_params=_params(("parallel", "arbitrary"),
                                [tq * GROUP_Q_WIDTH * 2 * 2, n_rows * HEAD_DIM * 2 * 2, tq * V7X_LANES * 2],
                                12 * n_rows * tq * 4),
        name="cmp_attn",
    )(q2, kc, vct, strip, gates_t)


def _gate_spec(branch, tq):
    return pl.BlockSpec((V7X_SUBLANES, tq), lambda gi, qi: (branch * A_KV_GROUPS + gi, qi))


def _softmax_step(r, z, z_max, vt, m_scr, l_scr, acc_scr):
    m_prev = m_scr[r]
    m_new = jnp.maximum(m_prev, z_max)
    p = jnp.exp2(z - m_new)
    alpha = jnp.exp2(m_prev - m_new)
    l_scr[r] = alpha * l_scr[r] + jnp.sum(p, axis=0, keepdims=True)
    acc_scr[r] = acc_scr[r] * alpha + jnp.dot(vt, p.astype(BF16), preferred_element_type=F32)
    m_scr[r] = m_new


def _pipelined_tile(kvj, kvj_next, logits_head, softmax_head):
    last = HEADS_PER_GROUP - 1
    for r in range(HEADS_PER_GROUP):
        if r < last:
            logits_head(r + 1, kvj)
        elif kvj_next is not None:
            logits_head(0, kvj_next)
        softmax_head(r, kvj)


def _softmax_init(m_scr, l_scr, acc_scr):
    m_scr[...] = jnp.full(m_scr.shape, NEG_INF, F32)
    l_scr[...] = jnp.zeros(l_scr.shape, F32)
    acc_scr[...] = jnp.zeros(acc_scr.shape, F32)


def _softmax_finish(gate_ref, o_ref, l_scr, acc_scr):
    for r in range(HEADS_PER_GROUP):
        ot = acc_scr[r] * (gate_ref[r:r + 1, :] / l_scr[r])
        o_ref[:, r * HEAD_DIM:(r + 1) * HEAD_DIM] = ot.T.astype(o_ref.dtype)


def _sel_attn_kernel(q_ref, k_ref, e_ref, vt_ref, sel_ref, bt_diag_ref, bt_prev_ref, gate_ref, o_ref,
                     qaug_scr, z_scr, zmax_scr, m_scr, l_scr, acc_scr):
    qi = pl.program_id(1)
    tk = ATTN_TILE
    _softmax_init(m_scr, l_scr, acc_scr)
    for r in range(HEADS_PER_GROUP):
        qaug_scr[r, :, 0:HEAD_DIM] = q_ref[:, r * HEAD_DIM:(r + 1) * HEAD_DIM]
        qaug_scr[r, :, HEAD_DIM:2 * HEAD_DIM] = sel_ref[0]

    def logits_head(r, kvj):
        rows = pl.ds(pl.multiple_of(kvj * tk, tk), tk)
        k_aug = jnp.concatenate([k_ref[rows, :], e_ref[rows, :]], axis=1)
        z = _nt_dot(k_aug, qaug_scr[r])
        z_scr[r] = z
        zmax_scr[r] = jnp.max(z, axis=0, keepdims=True)

    def softmax_near(bt_ref):
        def softmax_head(r, kvj):
            z = z_scr[r] + bt_ref[r, 0]
            _softmax_step(r, z, jnp.max(z, axis=0, keepdims=True), vt_ref[0, 0, kvj], m_scr, l_scr, acc_scr)
        return softmax_head

    def softmax_far(r, kvj):
        _softmax_step(r, z_scr[r], zmax_scr[r], vt_ref[0, 0, kvj], m_scr, l_scr, acc_scr)

    logits_head(0, qi)
    _pipelined_tile(qi, jnp.maximum(qi - 1, 0), logits_head, softmax_near(bt_diag_ref))

    @pl.when(qi >= 1)
    def _():
        _pipelined_tile(qi - 1, 0, logits_head, softmax_near(bt_prev_ref))

    n_far = jnp.maximum(qi - 1, 0)

    def far_tile(kvj, carry):
        _pipelined_tile(kvj, jnp.minimum(kvj + 1, n_far - 1), logits_head, softmax_far)
        return carry

    lax.fori_loop(0, n_far, far_tile, 0)
    _softmax_finish(gate_ref, o_ref, l_scr, acc_scr)


def _win_attn_kernel(q_ref, k_ref, vt_ref, bt_diag_ref, bt_prev_ref, gate_ref, o_ref,
                     z_scr, m_scr, l_scr, acc_scr):
    qi = pl.program_id(1)
    tk = ATTN_TILE
    _softmax_init(m_scr, l_scr, acc_scr)

    def logits_head(r, kvj):
        k = k_ref[pl.ds(pl.multiple_of(kvj * tk, tk), tk), :]
        z_scr[r] = _nt_dot(k, q_ref[:, r * HEAD_DIM:(r + 1) * HEAD_DIM])

    def softmax_near(bt_ref):
        def softmax_head(r, kvj):
            z = z_scr[r] + bt_ref[r, 0]
            _softmax_step(r, z, jnp.max(z, axis=0, keepdims=True), vt_ref[0, 0, kvj], m_scr, l_scr, acc_scr)
        return softmax_head

    logits_head(0, qi)
    _pipelined_tile(qi, jnp.maximum(qi - 1, 0), logits_head, softmax_near(bt_diag_ref))

    @pl.when(qi >= 1)
    def _():
        _pipelined_tile(qi - 1, None, logits_head, softmax_near(bt_prev_ref))

    _softmax_finish(gate_ref, o_ref, l_scr, acc_scr)


def _attn_scratch(tq):
    return [pltpu.VMEM((HEADS_PER_GROUP, ATTN_TILE, tq), F32), pltpu.VMEM((HEADS_PER_GROUP, 1, tq), F32),
            pltpu.VMEM((HEADS_PER_GROUP, 1, tq), F32), pltpu.VMEM((HEADS_PER_GROUP, HEAD_DIM, tq), F32)]


def _attn_common_specs(t, k_set, branch):
    tq = ATTN_TILE
    q_spec = pl.BlockSpec((tq, GROUP_Q_WIDTH), lambda gi, qi: (qi, gi))
    k_spec = pl.BlockSpec((t, HEAD_DIM), lambda gi, qi: (0, k_set * A_KV_GROUPS + gi))
    vt_spec = pl.BlockSpec((1, 1, t // ATTN_TILE, HEAD_DIM, ATTN_TILE), lambda gi, qi: (k_set, gi, 0, 0, 0))
    out_spec = pl.BlockSpec((tq, GROUP_Q_WIDTH), lambda gi, qi: (qi, gi))
    return q_spec, k_spec, vt_spec, _gate_spec(branch, tq), out_spec


def _bt_spec(kind):
    return pl.BlockSpec((HEADS_PER_GROUP, 1, ATTN_TILE, ATTN_TILE), lambda gi, qi: (gi, kind, 0, 0))


def _attn_block_bytes(t):
    tq = ATTN_TILE
    return [tq * GROUP_Q_WIDTH * 2 * 2, t * HEAD_DIM * 2 * 2, 2 * HEADS_PER_GROUP * tq * tq * 4]


def _sel_attn(q2, k_plain, vt, sel_neg, bt, gates_t, t):
    tq = ATTN_TILE
    q_spec, k_spec, vt_spec, gate_spec, out_spec = _attn_common_specs(t, 0, 1)
    sel_spec = pl.BlockSpec((1, tq, V7X_LANES), lambda gi, qi: (gi, qi, 0))
    e_onehot = (jnp.arange(t, dtype=jnp.int32)[:, None] // L_SEL
                == jnp.arange(V7X_LANES, dtype=jnp.int32)[None, :]).astype(BF16)
    e_spec = pl.BlockSpec((t, V7X_LANES), lambda gi, qi: (0, 0))
    return pl.pallas_call(
        _sel_attn_kernel,
        grid=(A_KV_GROUPS, t // tq),
        in_specs=[q_spec, k_spec, e_spec, vt_spec, sel_spec, _bt_spec(0), _bt_spec(1), gate_spec],
        out_specs=out_spec,
        out_shape=jax.ShapeDtypeStruct((t, A_WIDTH), BF16),
        scratch_shapes=([pltpu.VMEM((HEADS_PER_GROUP, tq, 2 * HEAD_DIM), BF16)] + _attn_scratch(tq)[:1]
                        + [pltpu.VMEM((HEADS_PER_GROUP, 1, tq), F32)] + _attn_scratch(tq)[1:]),
        compiler_params=_params(("parallel", "arbitrary"),
                                _attn_block_bytes(t) + [t * V7X_LANES * 2, tq * V7X_LANES * 2],
                                (8 + HEADS_PER_GROUP) * tq * tq * 4),
        name="sel_attn",
    )(q2, k_plain, e_onehot, vt, sel_neg, bt, bt, gates_t)


def _win_attn(q2, k_plain, vt, bt, gates_t, t):
    tq = ATTN_TILE
    q_spec, k_spec, vt_spec, gate_spec, out_spec = _attn_common_specs(t, 1, 2)
    return pl.pallas_call(
        _win_attn_kernel,
        grid=(A_KV_GROUPS, t // tq),
        in_specs=[q_spec, k_spec, vt_spec, _bt_spec(0), _bt_spec(2), gate_spec],
        out_specs=out_spec,
        out_shape=jax.ShapeDtypeStruct((t, A_WIDTH), BF16),
        scratch_shapes=_attn_scratch(tq),
        compiler_params=_params(("parallel", "arbitrary"), _attn_block_bytes(t),
                                (8 + HEADS_PER_GROUP) * tq * tq * 4),
        name="win_attn",
    )(q2, k_plain, vt, bt, bt, gates_t)


def _gmlp_kernel(u_ref, v_ref, z_ref, ng_ref, w_ref, bt_ref, o_ref):
    v = v_ref[...].astype(F32)
    ms = jnp.mean(v * v, axis=-1, keepdims=True)
    vn = (v * lax.rsqrt(ms + NORM_EPS) * ng_ref[...]).astype(BF16)
    p_idx = lax.broadcasted_iota(jnp.int32, (CHUNK, CHUNK), 0)
    q_idx = lax.broadcasted_iota(jnp.int32, (CHUNK, CHUNK), 1)
    causal = q_idx <= p_idx
    gd = B_WIDTH // B_GROUPS
    for gg in range(B_GROUPS):
        cols = slice(gg * gd, (gg + 1) * gd)
        w = jnp.where(causal, w_ref[gg], jnp.zeros((), w_ref.dtype))
        f = jnp.dot(w, vn[:, cols], preferred_element_type=F32) + bt_ref[:, gg:gg + 1]
        o_ref[:, cols] = (u_ref[:, cols].astype(F32) * f * z_ref[:, cols].astype(F32)).astype(o_ref.dtype)


def _gmlp(proj, norm_g, w_s, b_t, t):
    bw = B_WIDTH

    def col_spec(col):
        return pl.BlockSpec((CHUNK, bw), lambda i: (i, col // bw))

    return pl.pallas_call(
        _gmlp_kernel,
        grid=(t // CHUNK,),
        in_specs=[col_spec(COL_U), col_spec(COL_V), col_spec(COL_ZB),
                  pl.BlockSpec((1, bw), lambda i: (0, 0)),
                  pl.BlockSpec((B_GROUPS, CHUNK, CHUNK), lambda i: (0, 0, 0)),
                  pl.BlockSpec((CHUNK, B_GROUPS), lambda i: (0, 0))],
        out_specs=pl.BlockSpec((CHUNK, bw), lambda i: (i, 0)),
        out_shape=jax.ShapeDtypeStruct((t, bw), BF16),
        compiler_params=_params(("parallel",), [CHUNK * bw * 2] * 4 + [B_GROUPS * CHUNK * CHUNK * 2],
                                4 * CHUNK * bw * 4),
        name="gmlp",
    )(proj, proj, proj, norm_g.reshape(1, bw), w_s, b_t)


def _merge_kernel(oc_ref, os_ref, ow_ref, za_ref, ob_ref, wa_ref, wb_ref, ma_ref, mb_ref, o_ref, lhs_scr):
    @pl.when(pl.program_id(1) == 0)
    def _():
        o_a = oc_ref[...].astype(F32) + os_ref[...].astype(F32) + ow_ref[...].astype(F32)
        lhs_scr[...] = (o_a * za_ref[...].astype(F32)).astype(BF16)

    ya = jnp.dot(lhs_scr[...], wa_ref[0], preferred_element_type=F32)
    yb = jnp.dot(ob_ref[...], wb_ref[0], preferred_element_type=F32)
    o_ref[...] = (ma_ref[...].astype(F32) * ya + mb_ref[...].astype(F32) * yb).astype(o_ref.dtype)


def _merge(o_cmp, o_sel, o_win, proj, o_b, w_a, w_b, layer, t, d):
    tm = min(512, t)
    tn = min(512, d)
    aw = A_WIDTH
    bw = B_WIDTH
    col_mb = COL_MA + d
    row_a = pl.BlockSpec((tm, aw), lambda i, j: (i, 0))
    return pl.pallas_call(
        _merge_kernel,
        grid=(t // tm, d // tn),
        in_specs=[row_a, row_a, row_a,
                  pl.BlockSpec((tm, aw), lambda i, j: (i, COL_ZA // aw)),
                  pl.BlockSpec((tm, bw), lambda i, j: (i, 0)),
                  pl.BlockSpec((1, aw, tn), lambda i, j: (layer, 0, j)),
                  pl.BlockSpec((1, bw, tn), lambda i, j: (layer, 0, j)),
                  pl.BlockSpec((tm, tn), lambda i, j: (i, COL_MA // tn + j)),
                  pl.BlockSpec((tm, tn), lambda i, j: (i, col_mb // tn + j))],
        out_specs=pl.BlockSpec((tm, tn), lambda i, j: (i, j)),
        out_shape=jax.ShapeDtypeStruct((t, d), BF16),
        scratch_shapes=[pltpu.VMEM((tm, aw), BF16)],
        compiler_params=_params(("parallel", "arbitrary"),
                                [tm * aw * 2] * 5 + [aw * tn * 2, bw * tn * 2] + [tm * tn * 2] * 3,
                                tm * aw * 2 + 4 * tm * tn * 4),
        name="merge",
    )(o_cmp, o_sel, o_win, proj, o_b, w_a, w_b, proj, proj)


def _gate_weight_columns():
    src = np.full((V7X_LANES,), -1, np.int64)
    for br in range(3):
        for g in range(A_KV_GROUPS):
            for r in range(HEADS_PER_GROUP):
                src[(br * A_KV_GROUPS + g) * V7X_SUBLANES + r] = (g * HEADS_PER_GROUP + r) * 3 + br
    return src


def kernel(x, rel_bias, pre_norm, w_in, cmp_pe_k, cmp_w1_k, cmp_w2_k, cmp_pe_v, cmp_w1_v, cmp_w2_v,
           w_out_a, sgu_norm, sgu_w, sgu_b, w_out_b, w_out, post_norm):
    batch, t, d = x.shape
    assert batch == 1 and t % MM_TM == 0 and N_SEL <= t // L_SEL <= V7X_LANES
    depth = w_in.shape[0]
    g = A_KV_GROUPS
    xs = x.reshape(t, d)
    n_rest = COL_MA + 2 * d
    col_acts = [(COL_ZA, COL_U, "silu"), (COL_U, COL_ZB, "gelu"), (COL_ZB, COL_MA, "silu"),
                (COL_MA, n_rest, "sigmoid")]

    w_q = _cast_cols(w_in, 0, A_WIDTH, "cast_wq")
    w_cmp = _cast_cols(w_in, SRC_KV, 2 * A_KV_WIDTH, "cast_wcmp")
    w_kv = _cast_cols(w_in, SRC_KV + 2 * A_KV_WIDTH, 4 * A_KV_WIDTH, "cast_wkv")
    w_rest = _cast_cols(w_in, SRC_REST, n_rest, "cast_wrest")
    w_a = _cast_cols(w_out_a, 0, d, "cast_wa")
    w_b = _cast_cols(w_out_b, 0, d, "cast_wb")
    w_o = _cast_cols(w_out, 0, d, "cast_wo")
    gate_src = _gate_weight_columns()
    w_gate = jnp.where(jnp.asarray(gate_src >= 0)[None, None, :],
                       jnp.take(w_in, jnp.asarray(SRC_GATES + np.maximum(gate_src, 0)), axis=2), 0.0).astype(BF16)
    w1 = jnp.stack([cmp_w1_k, cmp_w1_v], axis=1).astype(BF16)
    w2 = jnp.stack([cmp_w2_k, cmp_w2_v], axis=1).astype(BF16)
    slab = STRIDE_CMP * HEAD_DIM
    pe = jnp.stack([cmp_pe_k, cmp_pe_v], axis=1).reshape(depth, 2, 2, slab)
    w_s = sgu_w.astype(BF16)

    bias_tiles = _bias_tiles(rel_bias)
    bias_strip = _bias_strip(rel_bias)

    h = _rmsnorm(xs, pre_norm[0], BF16)
    for layer in range(depth):
        q2 = _mm(h, w_q, layer, BF16, act="log2_scale", name="q_proj")
        x_slabs = _cmp_proj(h, w_cmp, layer)
        k_plain, vt = _kv_proj(h, w_kv, layer, 2)
        gates_t = _gate_proj(h, w_gate, layer)
        proj = _mm(h, w_rest, layer, BF16, col_acts=col_acts, name="in_proj")

        kc, kct = _compress(x_slabs, pe[layer], w1[layer], w2[layer])
        o_cmp, sel_neg = _cmp_attn(q2, kc, kct, bias_strip, gates_t, t)
        o_sel = _sel_attn(q2, k_plain, vt, sel_neg, bias_tiles, gates_t, t)
        o_win = _win_attn(q2, k_plain, vt, bias_tiles, gates_t, t)

        o_b = _gmlp(proj, sgu_norm[layer], w_s[layer], sgu_b[layer].T, t)
        merged = _merge(o_cmp, o_sel, o_win, proj, o_b, w_a, w_b, layer, t, d)
        y = _mm(merged, w_o, layer, F32, name="out_proj")
        g_next = pre_norm[layer + 1] if layer + 1 < depth else pre_norm[layer]
        xs, h = _post_norm_residual(xs, y, post_norm[layer], g_next)
    del g
    return xs.reshape(batch, t, d)
```

```python
import functools
import math

import numpy as np
import jax
import jax.numpy as jnp
from jax import lax
from jax.experimental import pallas as pl
from jax.experimental.pallas import tpu as pltpu

F32 = jnp.float32
BF16 = jnp.bfloat16

A_HEADS = 16
A_KV_GROUPS = 4
HEADS_PER_GROUP = A_HEADS // A_KV_GROUPS
HEAD_DIM = 128
A_WIDTH = A_HEADS * HEAD_DIM
A_KV_WIDTH = A_KV_GROUPS * HEAD_DIM
GROUP_Q_WIDTH = HEADS_PER_GROUP * HEAD_DIM
L_CMP = 32
STRIDE_CMP = 16
L_SEL = 64
N_SEL = 16
WINDOW = 512
B_GROUPS = 16
CHUNK = 128
B_WIDTH = 2048
NUM_BUCKETS = 32
MAX_DISTANCE = 128
NORM_EPS = 1e-6
NEG_INF = -1e30
SCALE = HEAD_DIM ** -0.5
LOG2_E = math.log2(math.e)
SCALE_LOG2 = SCALE * LOG2_E
N_GATE_COLS = 3 * A_HEADS

V7X_LANES = 128
V7X_SUBLANES = 8
V7X_SCOPED_VMEM_CAP_BYTES = 60000 * 1024

ATTN_TILE = 512
CMP_TQ = 256
CMP_ROWS_PER_TQ = CMP_TQ // STRIDE_CMP
MM_TM = 1024
MM_TN = 1024

SRC_KV = A_WIDTH
SRC_GATES = SRC_KV + 6 * A_KV_WIDTH
SRC_REST = SRC_GATES + N_GATE_COLS
COL_ZA = 0
COL_U = COL_ZA + A_WIDTH
COL_V = COL_U + B_WIDTH
COL_ZB = COL_V + B_WIDTH
COL_MA = COL_ZB + B_WIDTH


def _vmem_limit(block_bytes, temp_bytes=0):
    need = 2 * sum(block_bytes) + temp_bytes + (4 << 20)
    return int(min(max(need, 16 << 20), V7X_SCOPED_VMEM_CAP_BYTES))


def _params(sem, block_bytes, temp_bytes=0):
    return pltpu.CompilerParams(dimension_semantics=sem,
                                vmem_limit_bytes=_vmem_limit(block_bytes, temp_bytes))


def _t5_thresholds():
    n = np.arange(0, 4 * MAX_DISTANCE)
    max_exact = NUM_BUCKETS // 2
    nf = np.maximum(n, 1).astype(np.float32)
    large = max_exact + (np.log(nf / np.float32(max_exact)) / np.float32(math.log(MAX_DISTANCE / max_exact))
                         * np.float32(NUM_BUCKETS - max_exact)).astype(np.int32)
    bucket = np.where(n < max_exact, n, np.minimum(large, NUM_BUCKETS - 1))
    assert np.all(np.diff(bucket) >= 0) and bucket[-1] == NUM_BUCKETS - 1
    return [int(np.argmax(bucket >= b)) for b in range(1, NUM_BUCKETS)]


_T5_THR = _t5_thresholds()


def _sigmoid(x):
    return 1.0 / (1.0 + jnp.exp(-x))


def _silu(x):
    return x * _sigmoid(x)


def _gelu(x):
    return jax.nn.gelu(x, approximate=True)


_ACTS = {"none": lambda v: v, "silu": _silu, "gelu": _gelu, "sigmoid": _sigmoid,
         "log2_scale": lambda v: v * SCALE_LOG2}


def _rmsnorm_kernel(x_ref, g_ref, o_ref):
    x = x_ref[...]
    ms = jnp.mean(x * x, axis=-1, keepdims=True)
    o_ref[...] = (x * lax.rsqrt(ms + NORM_EPS) * g_ref[...]).astype(o_ref.dtype)


def _rmsnorm(x, g, out_dtype):
    t, d = x.shape
    tm = min(256, t)
    return pl.pallas_call(
        _rmsnorm_kernel,
        grid=(t // tm,),
        in_specs=[pl.BlockSpec((tm, d), lambda i: (i, 0)), pl.BlockSpec((1, d), lambda i: (0, 0))],
        out_specs=pl.BlockSpec((tm, d), lambda i: (i, 0)),
        out_shape=jax.ShapeDtypeStruct((t, d), out_dtype),
        compiler_params=_params(("parallel",), [tm * d * 4, tm * d * 4]),
        name="rmsnorm",
    )(x, g.reshape(1, d))


def _post_kernel(x_ref, y_ref, g_ref, gn_ref, o_ref, h_ref):
    y = y_ref[...]
    ms = jnp.mean(y * y, axis=-1, keepdims=True)
    x = x_ref[...] + y * lax.rsqrt(ms + NORM_EPS) * g_ref[...]
    o_ref[...] = x
    ms_x = jnp.mean(x * x, axis=-1, keepdims=True)
    h_ref[...] = (x * lax.rsqrt(ms_x + NORM_EPS) * gn_ref[...]).astype(h_ref.dtype)


def _post_norm_residual(x, y, g, g_next):
    t, d = x.shape
    tm = min(256, t)
    row = pl.BlockSpec((tm, d), lambda i: (i, 0))
    vec = pl.BlockSpec((1, d), lambda i: (0, 0))
    return pl.pallas_call(
        _post_kernel,
        grid=(t // tm,),
        in_specs=[row, row, vec, vec],
        out_specs=[row, row],
        out_shape=[jax.ShapeDtypeStruct((t, d), F32), jax.ShapeDtypeStruct((t, d), BF16)],
        compiler_params=_params(("parallel",), [tm * d * 4] * 4),
        name="post_norm_residual",
    )(x, y, g.reshape(1, d), g_next.reshape(1, d))


def _cast_kernel(w_ref, o_ref):
    o_ref[...] = w_ref[...].astype(o_ref.dtype)


def _cast_rows(w, row0, nrows, name):
    n_layers, _, c = w.shape
    assert row0 % (2 * V7X_SUBLANES) == 0
    tr = next(r for r in (512, 256, 128) if nrows % r == 0)
    return pl.pallas_call(
        _cast_kernel,
        grid=(n_layers, nrows // tr),
        in_specs=[pl.BlockSpec((pl.Element(1), pl.Element(tr), pl.Element(c)),
                               lambda l, j: (l, pl.multiple_of(row0 + j * tr, 2 * V7X_SUBLANES), 0))],
        out_specs=pl.BlockSpec((1, tr, c), lambda l, j: (l, j, 0)),
        out_shape=jax.ShapeDtypeStruct((n_layers, nrows, c), BF16),
        compiler_params=_params(("parallel", "parallel"), [tr * c * 4, tr * c * 2], tr * c * 4),
        name=name,
    )(w)


def _nt_dot(a, b):
    return lax.dot_general(a, b, (((1,), (1,)), ((), ())), preferred_element_type=F32)


def _mm_kernel(a_ref, b_ref, o_ref, *, act, nt, n_chunks):
    a = a_ref[...]
    cw = o_ref.shape[1] // n_chunks
    for c in range(n_chunks):
        cols = slice(c * cw, (c + 1) * cw)
        if nt:
            acc = _nt_dot(a, b_ref[0, cols, :])
        else:
            acc = jnp.dot(a, b_ref[0, :, cols], preferred_element_type=F32)
        o_ref[:, cols] = _ACTS[act](acc).astype(o_ref.dtype)


def _mm(a, b, layer, out_dtype, act="none", nt=False, col0=0, n=None, name="mm"):
    m, k = a.shape
    n_all = b.shape[1] if nt else b.shape[2]
    n = n_all - col0 if n is None else n
    tm = min(MM_TM, m)
    tn = next(c for c in (MM_TN, 512, 256, 128) if n % c == 0 and col0 % c == 0)
    j0 = col0 // tn
    n_chunks = 1 if act == "none" else max(1, tn // 256)
    osz = jnp.dtype(out_dtype).itemsize
    if nt:
        b_spec = pl.BlockSpec((1, tn, k), lambda i, j: (layer, j0 + j, 0))
    else:
        b_spec = pl.BlockSpec((1, k, tn), lambda i, j: (layer, 0, j0 + j))
    return pl.pallas_call(
        functools.partial(_mm_kernel, act=act, nt=nt, n_chunks=n_chunks),
        grid=(m // tm, n // tn),
        in_specs=[pl.BlockSpec((tm, k), lambda i, j: (i, 0)), b_spec],
        out_specs=pl.BlockSpec((tm, tn), lambda i, j: (i, j)),
        out_shape=jax.ShapeDtypeStruct((m, n), out_dtype),
        compiler_params=_params(("parallel", "arbitrary"), [tm * k * 2, k * tn * 2, tm * tn * osz],
                                2 * tm * tn * 4),
        name=name,
    )(a, b)


def _gate_proj_kernel(a_ref, b_ref, o_ref):
    o_ref[...] = _sigmoid(_nt_dot(b_ref[0], a_ref[...]))


def _gate_proj(h, w_gate, layer):
    t, k = h.shape
    tm = min(MM_TM, t)
    return pl.pallas_call(
        _gate_proj_kernel,
        grid=(t // tm,),
        in_specs=[pl.BlockSpec((tm, k), lambda i: (i, 0)),
                  pl.BlockSpec((1, V7X_LANES, k), lambda i: (layer, 0, 0))],
        out_specs=pl.BlockSpec((V7X_LANES, tm), lambda i: (0, i)),
        out_shape=jax.ShapeDtypeStruct((V7X_LANES, t), F32),
        compiler_params=_params(("parallel",), [tm * k * 2, k * V7X_LANES * 2, tm * V7X_LANES * 4]),
        name="gate_proj",
    )(h, w_gate)


def _cmp_proj_kernel(a_ref, b_ref, o_ref, scr):
    acc = _nt_dot(a_ref[...], b_ref[0])
    n_sg = scr.shape[0]
    slab_rows = o_ref.shape[1]
    for sg in range(n_sg):
        scr[sg] = acc[:, sg * HEAD_DIM:(sg + 1) * HEAD_DIM]
    for sg in range(n_sg):
        for l in range(STRIDE_CMP):
            o_ref[sg, :, l * HEAD_DIM:(l + 1) * HEAD_DIM] = (
                scr[sg, pl.ds(l, slab_rows, stride=STRIDE_CMP), :].astype(o_ref.dtype))


def _cmp_proj(h, w_cmp, layer):
    t, k = h.shape
    tm = min(MM_TM, t)
    n_sg = 2 * A_KV_GROUPS
    n = n_sg * HEAD_DIM
    slab = STRIDE_CMP * HEAD_DIM
    return pl.pallas_call(
        _cmp_proj_kernel,
        grid=(t // tm,),
        in_specs=[pl.BlockSpec((tm, k), lambda i: (i, 0)), pl.BlockSpec((1, n, k), lambda i: (layer, 0, 0))],
        out_specs=pl.BlockSpec((n_sg, tm // STRIDE_CMP, slab), lambda i: (0, i, 0)),
        out_shape=jax.ShapeDtypeStruct((n_sg, t // STRIDE_CMP, slab), BF16),
        scratch_shapes=[pltpu.VMEM((n_sg, tm, HEAD_DIM), F32)],
        compiler_params=_params(("parallel",), [tm * k * 2, k * n * 2, tm * n * 2], 3 * tm * n * 4),
        name="cmp_proj",
    )(h, w_cmp)


def _kv_proj_kernel(a_ref, b_ref, k_ref, vt_ref):
    a = a_ref[...]
    k_ref[...] = _nt_dot(a, b_ref[0, :A_KV_WIDTH, :]).astype(k_ref.dtype)
    vt = _nt_dot(b_ref[0, A_KV_WIDTH:, :], a).astype(vt_ref.dtype)
    for g in range(A_KV_GROUPS):
        for s in range(vt_ref.shape[2]):
            vt_ref[0, g, s] = vt[g * HEAD_DIM:(g + 1) * HEAD_DIM, s * ATTN_TILE:(s + 1) * ATTN_TILE]


def _kv_proj(h, w_kv, layer, n_sets):
    t, k = h.shape
    tm = min(MM_TM, t)
    tiles = tm // ATTN_TILE
    return pl.pallas_call(
        _kv_proj_kernel,
        grid=(t // tm, n_sets),
        in_specs=[pl.BlockSpec((tm, k), lambda i, j: (i, 0)),
                  pl.BlockSpec((1, 2 * A_KV_WIDTH, k), lambda i, j: (layer, j, 0))],
        out_specs=[pl.BlockSpec((tm, A_KV_WIDTH), lambda i, j: (i, j)),
                   pl.BlockSpec((1, A_KV_GROUPS, tiles, HEAD_DIM, ATTN_TILE), lambda i, j: (j, 0, i, 0, 0))],
        out_shape=[jax.ShapeDtypeStruct((t, n_sets * A_KV_WIDTH), BF16),
                   jax.ShapeDtypeStruct((n_sets, A_KV_GROUPS, t // ATTN_TILE, HEAD_DIM, ATTN_TILE), BF16)],
        compiler_params=_params(("parallel", "arbitrary"),
                                [tm * k * 2, k * 2 * A_KV_WIDTH * 2, tm * 2 * A_KV_WIDTH * 2],
                                3 * tm * 2 * A_KV_WIDTH * 4),
        name="kv_proj",
    )(h, w_kv)


def _t5_bias_rel(dist, rb_ref, h):
    far = rb_ref[NUM_BUCKETS - 1, h]
    val = jnp.zeros(dist.shape, F32)
    for b in range(NUM_BUCKETS - 2, -1, -1):
        val = jnp.where(dist < _T5_THR[b], (rb_ref[b, h] - far) * LOG2_E, val)
    return val


def _bias_tile_kernel(rb_ref, o_ref):
    h = pl.program_id(0)
    kind = pl.program_id(1)
    shape = (ATTN_TILE, ATTN_TILE)
    key = lax.broadcasted_iota(jnp.int32, shape, 0)
    qry = lax.broadcasted_iota(jnp.int32, shape, 1)
    dist = jnp.where(kind == 0, 0, ATTN_TILE) + qry - key
    limit = jnp.where(kind == 2, WINDOW, 4 * ATTN_TILE)
    o_ref[0, 0] = jnp.where((dist >= 0) & (dist < limit), _t5_bias_rel(dist, rb_ref, h), NEG_INF)


def _bias_tiles(rel_bias):
    return pl.pallas_call(
        _bias_tile_kernel,
        grid=(A_HEADS, 3),
        in_specs=[pl.BlockSpec(memory_space=pltpu.SMEM)],
        out_specs=pl.BlockSpec((1, 1, ATTN_TILE, ATTN_TILE), lambda h, k: (h, k, 0, 0)),
        out_shape=jax.ShapeDtypeStruct((A_HEADS, 3, ATTN_TILE, ATTN_TILE), F32),
        compiler_params=_params(("parallel", "parallel"), [ATTN_TILE * ATTN_TILE * 4]),
        name="bias_tiles",
    )(rel_bias)


def _bias_strip_kernel(rb_ref, o_ref):
    h = pl.program_id(0)
    shape = (2 * CMP_ROWS_PER_TQ, CMP_TQ)
    c = lax.broadcasted_iota(jnp.int32, shape, 0)
    a = lax.broadcasted_iota(jnp.int32, shape, 1)
    dist = a - STRIDE_CMP * c + (CMP_TQ - (L_CMP - 1))
    o_ref[0] = jnp.where(dist >= 0, _t5_bias_rel(dist, rb_ref, h), NEG_INF)


def _bias_strip(rel_bias):
    rows = 2 * CMP_ROWS_PER_TQ
    return pl.pallas_call(
        _bias_strip_kernel,
        grid=(A_HEADS,),
        in_specs=[pl.BlockSpec(memory_space=pltpu.SMEM)],
        out_specs=pl.BlockSpec((1, rows, CMP_TQ), lambda h: (h, 0, 0)),
        out_shape=jax.ShapeDtypeStruct((A_HEADS, rows, CMP_TQ), F32),
        compiler_params=_params(("parallel",), [rows * CMP_TQ * 4]),
        name="bias_strip",
    )(rel_bias)


def _compress_kernel(x_ref, pe_ref, w1_ref, w2_ref, o_ref, ot_ref):
    x = x_ref[0].astype(F32)
    half = x.shape[1]
    lo = (x + pe_ref[0, 0:1, :]).astype(BF16)
    hi = (x + pe_ref[0, 1:2, :]).astype(BF16)
    a = jnp.dot(lo, w1_ref[0, :half, :], preferred_element_type=F32)
    b = jnp.dot(hi, w1_ref[0, half:, :], preferred_element_type=F32)
    n_rows = x.shape[0]
    hidden = a + pltpu.roll(b, n_rows - 1, 0)
    out = jnp.dot(_silu(hidden).astype(BF16), w2_ref[0], preferred_element_type=F32)
    o_ref[0] = out.astype(o_ref.dtype)
    ot_ref[0] = out.T.astype(ot_ref.dtype)


def _compress(x_slabs, pe, w1, w2):
    ng, n_rows, half = x_slabs.shape
    g = A_KV_GROUPS
    return pl.pallas_call(
        _compress_kernel,
        grid=(2, g),
        in_specs=[pl.BlockSpec((1, n_rows, half), lambda s, i: (s * g + i, 0, 0)),
                  pl.BlockSpec((1, 2, half), lambda s, i: (s, 0, 0)),
                  pl.BlockSpec((1, 2 * half, HEAD_DIM), lambda s, i: (s, 0, 0)),
                  pl.BlockSpec((1, HEAD_DIM, HEAD_DIM), lambda s, i: (s, 0, 0))],
        out_specs=[pl.BlockSpec((1, n_rows, HEAD_DIM), lambda s, i: (s * g + i, 0, 0)),
                   pl.BlockSpec((1, HEAD_DIM, n_rows), lambda s, i: (s * g + i, 0, 0))],
        out_shape=[jax.ShapeDtypeStruct((ng, n_rows, HEAD_DIM), BF16),
                   jax.ShapeDtypeStruct((ng, HEAD_DIM, n_rows), BF16)],
        compiler_params=_params(("parallel", "parallel"),
                                [n_rows * half * 2, 2 * half * 4, 2 * half * HEAD_DIM * 2],
                                4 * n_rows * half * 4),
        name="compress",
    )(x_slabs, pe, w1, w2)


def _cmp_attn_kernel(q_ref, kc_ref, vct_ref, strip_ref, gate_ref, o_ref, sel_ref, s_scr, pg_scr):
    qi = pl.program_id(1)
    n_rows = kc_ref.shape[1]
    n_sel = n_rows // (L_SEL // STRIDE_CMP)
    tq = CMP_TQ
    rpt = CMP_ROWS_PER_TQ
    t_row = qi * tq + lax.broadcasted_iota(jnp.int32, (1, tq), 1)
    any_valid = jnp.where(t_row >= L_CMP - 1, 1.0, 0.0)
    key_live = lax.broadcasted_iota(jnp.int32, (n_rows, tq), 0) < (qi + 1) * rpt
    strip_rows = pl.ds(pl.multiple_of(qi * rpt, rpt), 2 * rpt)
    for r in range(HEADS_PER_GROUP):
        q = q_ref[:, r * HEAD_DIM:(r + 1) * HEAD_DIM]
        s_scr[r, 0:rpt, :] = jnp.zeros((rpt, tq), F32)
        s_scr[r, rpt:rpt + n_rows, :] = jnp.where(key_live, _nt_dot(kc_ref[0], q), NEG_INF)
        s_scr[r, strip_rows, :] = s_scr[r, strip_rows, :] + strip_ref[r]
    pg = jnp.zeros((n_rows, tq), F32)
    for r in range(HEADS_PER_GROUP):
        s = s_scr[r, rpt:rpt + n_rows, :]
        m = jnp.max(s, axis=0, keepdims=True)
        e = jnp.exp2(s - m)
        p = e * (any_valid / jnp.sum(e, axis=0, keepdims=True))
        pg = pg + p
        ot = jnp.dot(vct_ref[0], p.astype(BF16), preferred_element_type=F32) * gate_ref[r:r + 1, :]
        o_ref[:, r * HEAD_DIM:(r + 1) * HEAD_DIM] = ot.T.astype(o_ref.dtype)

    pad = V7X_SUBLANES
    lane_tiles = tq // V7X_LANES
    for c in range(lane_tiles):
        pg_scr[c, 0:pad, :] = jnp.zeros((pad, V7X_LANES), F32)
        pg_scr[c, pad:pad + n_rows, :] = pg[:, c * V7X_LANES:(c + 1) * V7X_LANES]
    per_sel = L_SEL // STRIDE_CMP

    def rows(off):
        return jnp.concatenate(
            [pg_scr[c, pl.ds(pad + off, n_sel, stride=per_sel), :] for c in range(lane_tiles)], axis=1)

    imp = (rows(-1) + rows(3)) + 2.0 * (rows(0) + rows(1) + rows(2))

    j = lax.broadcasted_iota(jnp.int32, (n_sel, tq), 0)
    jf = j.astype(F32)
    jt = jnp.right_shift(t_row, int(math.log2(L_SEL)))
    forced = (j == 0) | (j == jt) | (j == jt - 1)
    cand = (j >= 1) & (j <= jt - 2)
    score = jnp.where(cand, imp, -1.0)
    sel = jnp.where(forced, 1.0, 0.0)
    for _ in range(N_SEL - 3):
        mx = jnp.max(score, axis=0, keepdims=True)
        first_j = jnp.min(jnp.where(score == mx, jf, float(n_sel)), axis=0, keepdims=True)
        pick = (jf == first_j) & (mx >= 0.0)
        sel = jnp.where(pick, 1.0, sel)
        score = jnp.where(pick, -2.0, score)
    neg = jnp.where(sel > 0.5, 0.0, NEG_INF)
    if n_sel < V7X_LANES:
        neg = jnp.concatenate([neg, jnp.zeros((V7X_LANES - n_sel, tq), F32)], axis=0)
    sel_ref[0] = neg.T.astype(sel_ref.dtype)


def _cmp_attn(q2, kc, vct, strip, gates_t, t):
    n_rows = kc.shape[1]
    g = A_KV_GROUPS
    tq = CMP_TQ
    return pl.pallas_call(
        _cmp_attn_kernel,
        grid=(g, t // tq),
        in_specs=[pl.BlockSpec((tq, GROUP_Q_WIDTH), lambda gi, qi: (qi, gi)),
                  pl.BlockSpec((1, n_rows, HEAD_DIM), lambda gi, qi: (gi, 0, 0)),
                  pl.BlockSpec((1, HEAD_DIM, n_rows), lambda gi, qi: (g + gi, 0, 0)),
                  pl.BlockSpec((HEADS_PER_GROUP, 2 * CMP_ROWS_PER_TQ, tq), lambda gi, qi: (gi, 0, 0)),
                  _gate_spec(0, tq)],
        out_specs=[pl.BlockSpec((tq, GROUP_Q_WIDTH), lambda gi, qi: (qi, gi)),
                   pl.BlockSpec((1, tq, V7X_LANES), lambda gi, qi: (gi, qi, 0))],
        out_shape=[jax.ShapeDtypeStruct((t, A_WIDTH), BF16),
                   jax.ShapeDtypeStruct((g, t, V7X_LANES), BF16)],
        scratch_shapes=[pltpu.VMEM((HEADS_PER_GROUP, CMP_ROWS_PER_TQ + n_rows, tq), F32),
                        pltpu.VMEM((tq // V7X_LANES, n_rows + V7X_SUBLANES, V7X_LANES), F32)],
        compiler_params=_params(("parallel", "arbitrary"),
                                [tq * GROUP_Q_WIDTH * 2 * 2, n_rows * HEAD_DIM * 2 * 2, tq * V7X_LANES * 2],
                                12 * n_rows * tq * 4),
        name="cmp_attn",
    )(q2, kc, vct, strip, gates_t)


def _gate_spec(branch, tq):
    return pl.BlockSpec((V7X_SUBLANES, tq), lambda gi, qi: (branch * A_KV_GROUPS + gi, qi))


def _softmax_step(r, z, z_max, vt, m_scr, l_scr, acc_scr):
    m_prev = m_scr[r]
    m_new = jnp.maximum(m_prev, z_max)
    p = jnp.exp2(z - m_new)
    alpha = jnp.exp2(m_prev - m_new)
    l_scr[r] = alpha * l_scr[r] + jnp.sum(p, axis=0, keepdims=True)
    acc_scr[r] = acc_scr[r] * alpha + jnp.dot(vt, p.astype(BF16), preferred_element_type=F32)
    m_scr[r] = m_new


def _pipelined_tile(kvj, kvj_next, logits_head, softmax_head):
    last = HEADS_PER_GROUP - 1
    for r in range(HEADS_PER_GROUP):
        if r < last:
            logits_head(r + 1, kvj)
        elif kvj_next is not None:
            logits_head(0, kvj_next)
        softmax_head(r, kvj)


def _softmax_init(m_scr, l_scr, acc_scr):
    m_scr[...] = jnp.full(m_scr.shape, NEG_INF, F32)
    l_scr[...] = jnp.zeros(l_scr.shape, F32)
    acc_scr[...] = jnp.zeros(acc_scr.shape, F32)


def _softmax_finish(gate_ref, o_ref, l_scr, acc_scr):
    for r in range(HEADS_PER_GROUP):
        ot = acc_scr[r] * (gate_ref[r:r + 1, :] / l_scr[r])
        o_ref[:, r * HEAD_DIM:(r + 1) * HEAD_DIM] = ot.T.astype(o_ref.dtype)


def _sel_attn_kernel(q_ref, k_ref, e_ref, vt_ref, sel_ref, bt_diag_ref, bt_prev_ref, gate_ref, o_ref,
                     qaug_scr, z_scr, zmax_scr, m_scr, l_scr, acc_scr):
    qi = pl.program_id(1)
    tk = ATTN_TILE
    _softmax_init(m_scr, l_scr, acc_scr)
    for r in range(HEADS_PER_GROUP):
        qaug_scr[r, :, 0:HEAD_DIM] = q_ref[:, r * HEAD_DIM:(r + 1) * HEAD_DIM]
        qaug_scr[r, :, HEAD_DIM:2 * HEAD_DIM] = sel_ref[0]

    def logits_head(r, kvj):
        rows = pl.ds(pl.multiple_of(kvj * tk, tk), tk)
        k_aug = jnp.concatenate([k_ref[rows, :], e_ref[rows, :]], axis=1)
        z = _nt_dot(k_aug, qaug_scr[r])
        z_scr[r] = z
        zmax_scr[r] = jnp.max(z, axis=0, keepdims=True)

    def softmax_near(bt_ref):
        def softmax_head(r, kvj):
            z = z_scr[r] + bt_ref[r, 0]
            _softmax_step(r, z, jnp.max(z, axis=0, keepdims=True), vt_ref[0, 0, kvj], m_scr, l_scr, acc_scr)
        return softmax_head

    def softmax_far(r, kvj):
        _softmax_step(r, z_scr[r], zmax_scr[r], vt_ref[0, 0, kvj], m_scr, l_scr, acc_scr)

    logits_head(0, qi)
    _pipelined_tile(qi, jnp.maximum(qi - 1, 0), logits_head, softmax_near(bt_diag_ref))

    @pl.when(qi >= 1)
    def _():
        _pipelined_tile(qi - 1, 0, logits_head, softmax_near(bt_prev_ref))

    n_far = jnp.maximum(qi - 1, 0)

    def far_tile(kvj, carry):
        _pipelined_tile(kvj, jnp.minimum(kvj + 1, n_far - 1), logits_head, softmax_far)
        return carry

    lax.fori_loop(0, n_far, far_tile, 0)
    _softmax_finish(gate_ref, o_ref, l_scr, acc_scr)


def _win_attn_kernel(q_ref, k_ref, vt_ref, bt_diag_ref, bt_prev_ref, gate_ref, o_ref,
                     z_scr, m_scr, l_scr, acc_scr):
    qi = pl.program_id(1)
    tk = ATTN_TILE
    _softmax_init(m_scr, l_scr, acc_scr)

    def logits_head(r, kvj):
        k = k_ref[pl.ds(pl.multiple_of(kvj * tk, tk), tk), :]
        z_scr[r] = _nt_dot(k, q_ref[:, r * HEAD_DIM:(r + 1) * HEAD_DIM])

    def softmax_near(bt_ref):
        def softmax_head(r, kvj):
            z = z_scr[r] + bt_ref[r, 0]
            _softmax_step(r, z, jnp.max(z, axis=0, keepdims=True), vt_ref[0, 0, kvj], m_scr, l_scr, acc_scr)
        return softmax_head

    logits_head(0, qi)
    _pipelined_tile(qi, jnp.maximum(qi - 1, 0), logits_head, softmax_near(bt_diag_ref))

    @pl.when(qi >= 1)
    def _():
        _pipelined_tile(qi - 1, None, logits_head, softmax_near(bt_prev_ref))

    _softmax_finish(gate_ref, o_ref, l_scr, acc_scr)


def _attn_scratch(tq):
    return [pltpu.VMEM((HEADS_PER_GROUP, ATTN_TILE, tq), F32), pltpu.VMEM((HEADS_PER_GROUP, 1, tq), F32),
            pltpu.VMEM((HEADS_PER_GROUP, 1, tq), F32), pltpu.VMEM((HEADS_PER_GROUP, HEAD_DIM, tq), F32)]


def _attn_common_specs(t, k_set, branch):
    tq = ATTN_TILE
    q_spec = pl.BlockSpec((tq, GROUP_Q_WIDTH), lambda gi, qi: (qi, gi))
    k_spec = pl.BlockSpec((t, HEAD_DIM), lambda gi, qi: (0, k_set * A_KV_GROUPS + gi))
    vt_spec = pl.BlockSpec((1, 1, t // ATTN_TILE, HEAD_DIM, ATTN_TILE), lambda gi, qi: (k_set, gi, 0, 0, 0))
    out_spec = pl.BlockSpec((tq, GROUP_Q_WIDTH), lambda gi, qi: (qi, gi))
    return q_spec, k_spec, vt_spec, _gate_spec(branch, tq), out_spec


def _bt_spec(kind):
    return pl.BlockSpec((HEADS_PER_GROUP, 1, ATTN_TILE, ATTN_TILE), lambda gi, qi: (gi, kind, 0, 0))


def _attn_block_bytes(t):
    tq = ATTN_TILE
    return [tq * GROUP_Q_WIDTH * 2 * 2, t * HEAD_DIM * 2 * 2, 2 * HEADS_PER_GROUP * tq * tq * 4]


def _sel_attn(q2, k_plain, vt, sel_neg, bt, gates_t, t):
    tq = ATTN_TILE
    q_spec, k_spec, vt_spec, gate_spec, out_spec = _attn_common_specs(t, 0, 1)
    sel_spec = pl.BlockSpec((1, tq, V7X_LANES), lambda gi, qi: (gi, qi, 0))
    e_onehot = (jnp.arange(t, dtype=jnp.int32)[:, None] // L_SEL
                == jnp.arange(V7X_LANES, dtype=jnp.int32)[None, :]).astype(BF16)
    e_spec = pl.BlockSpec((t, V7X_LANES), lambda gi, qi: (0, 0))
    return pl.pallas_call(
        _sel_attn_kernel,
        grid=(A_KV_GROUPS, t // tq),
        in_specs=[q_spec, k_spec, e_spec, vt_spec, sel_spec, _bt_spec(0), _bt_spec(1), gate_spec],
        out_specs=out_spec,
        out_shape=jax.ShapeDtypeStruct((t, A_WIDTH), BF16),
        scratch_shapes=([pltpu.VMEM((HEADS_PER_GROUP, tq, 2 * HEAD_DIM), BF16)] + _attn_scratch(tq)[:1]
                        + [pltpu.VMEM((HEADS_PER_GROUP, 1, tq), F32)] + _attn_scratch(tq)[1:]),
        compiler_params=_params(("parallel", "arbitrary"),
                                _attn_block_bytes(t) + [t * V7X_LANES * 2, tq * V7X_LANES * 2],
                                (8 + HEADS_PER_GROUP) * tq * tq * 4),
        name="sel_attn",
    )(q2, k_plain, e_onehot, vt, sel_neg, bt, bt, gates_t)


def _win_attn(q2, k_plain, vt, bt, gates_t, t):
    tq = ATTN_TILE
    q_spec, k_spec, vt_spec, gate_spec, out_spec = _attn_common_specs(t, 1, 2)
    return pl.pallas_call(
        _win_attn_kernel,
        grid=(A_KV_GROUPS, t // tq),
        in_specs=[q_spec, k_spec, vt_spec, _bt_spec(0), _bt_spec(2), gate_spec],
        out_specs=out_spec,
        out_shape=jax.ShapeDtypeStruct((t, A_WIDTH), BF16),
        scratch_shapes=_attn_scratch(tq),
        compiler_params=_params(("parallel", "arbitrary"), _attn_block_bytes(t),
                                (8 + HEADS_PER_GROUP) * tq * tq * 4),
        name="win_attn",
    )(q2, k_plain, vt, bt, bt, gates_t)


def _gmlp_kernel(u_ref, v_ref, z_ref, ng_ref, w_ref, bt_ref, o_ref):
    v = v_ref[...].astype(F32)
    ms = jnp.mean(v * v, axis=-1, keepdims=True)
    vn = (v * lax.rsqrt(ms + NORM_EPS) * ng_ref[...]).astype(BF16)
    p_idx = lax.broadcasted_iota(jnp.int32, (CHUNK, CHUNK), 0)
    q_idx = lax.broadcasted_iota(jnp.int32, (CHUNK, CHUNK), 1)
    causal = q_idx <= p_idx
    gd = B_WIDTH // B_GROUPS
    for gg in range(B_GROUPS):
        cols = slice(gg * gd, (gg + 1) * gd)
        w = jnp.where(causal, w_ref[gg], jnp.zeros((), w_ref.dtype))
        f = jnp.dot(w, vn[:, cols], preferred_element_type=F32) + bt_ref[:, gg:gg + 1]
        o_ref[:, cols] = (u_ref[:, cols].astype(F32) * f * z_ref[:, cols].astype(F32)).astype(o_ref.dtype)


def _gmlp(uv, zb, norm_g, w_s, b_t, t):
    bw = B_WIDTH

    def col_spec(col):
        return pl.BlockSpec((CHUNK, bw), lambda i: (i, col))

    return pl.pallas_call(
        _gmlp_kernel,
        grid=(t // CHUNK,),
        in_specs=[col_spec(0), col_spec(1), col_spec(0),
                  pl.BlockSpec((1, bw), lambda i: (0, 0)),
                  pl.BlockSpec((B_GROUPS, CHUNK, CHUNK), lambda i: (0, 0, 0)),
                  pl.BlockSpec((CHUNK, B_GROUPS), lambda i: (0, 0))],
        out_specs=pl.BlockSpec((CHUNK, bw), lambda i: (i, 0)),
        out_shape=jax.ShapeDtypeStruct((t, bw), BF16),
        compiler_params=_params(("parallel",), [CHUNK * bw * 2] * 4 + [B_GROUPS * CHUNK * CHUNK * 2],
                                4 * CHUNK * bw * 4),
        name="gmlp",
    )(uv, uv, zb, norm_g.reshape(1, bw), w_s, b_t)


def _merge_kernel(oc_ref, os_ref, ow_ref, za_ref, ob_ref, wa_ref, wb_ref, ma_ref, mb_ref, o_ref, lhs_scr):
    @pl.when(pl.program_id(1) == 0)
    def _():
        o_a = oc_ref[...].astype(F32) + os_ref[...].astype(F32) + ow_ref[...].astype(F32)
        lhs_scr[...] = (o_a * za_ref[...].astype(F32)).astype(BF16)

    ya = jnp.dot(lhs_scr[...], wa_ref[0], preferred_element_type=F32)
    yb = jnp.dot(ob_ref[...], wb_ref[0], preferred_element_type=F32)
    o_ref[...] = (ma_ref[...].astype(F32) * ya + mb_ref[...].astype(F32) * yb).astype(o_ref.dtype)


def _merge(o_cmp, o_sel, o_win, za, o_b, m_gates, w_a, w_b, layer, t, d):
    tm = min(512, t)
    tn = min(512, d)
    aw = A_WIDTH
    bw = B_WIDTH
    row_a = pl.BlockSpec((tm, aw), lambda i, j: (i, 0))
    return pl.pallas_call(
        _merge_kernel,
        grid=(t // tm, d // tn),
        in_specs=[row_a, row_a, row_a, row_a,
                  pl.BlockSpec((tm, bw), lambda i, j: (i, 0)),
                  pl.BlockSpec((1, aw, tn), lambda i, j: (layer, 0, j)),
                  pl.BlockSpec((1, bw, tn), lambda i, j: (layer, 0, j)),
                  pl.BlockSpec((tm, tn), lambda i, j: (i, j)),
                  pl.BlockSpec((tm, tn), lambda i, j: (i, d // tn + j))],
        out_specs=pl.BlockSpec((tm, tn), lambda i, j: (i, j)),
        out_shape=jax.ShapeDtypeStruct((t, d), BF16),
        scratch_shapes=[pltpu.VMEM((tm, aw), BF16)],
        compiler_params=_params(("parallel", "arbitrary"),
                                [tm * aw * 2] * 5 + [aw * tn * 2, bw * tn * 2] + [tm * tn * 2] * 3,
                                tm * aw * 2 + 4 * tm * tn * 4),
        name="merge",
    )(o_cmp, o_sel, o_win, za, o_b, w_a, w_b, m_gates, m_gates)


def _gate_weight_columns():
    src = np.full((V7X_LANES,), -1, np.int64)
    for br in range(3):
        for g in range(A_KV_GROUPS):
            for r in range(HEADS_PER_GROUP):
                src[(br * A_KV_GROUPS + g) * V7X_SUBLANES + r] = (g * HEADS_PER_GROUP + r) * 3 + br
    return src


def kernel(x, rel_bias, pre_norm, w_in, cmp_pe_k, cmp_w1_k, cmp_w2_k, cmp_pe_v, cmp_w1_v, cmp_w2_v,
           w_out_a, sgu_norm, sgu_w, sgu_b, w_out_b, w_out, post_norm):
    batch, t, d = x.shape
    assert batch == 1 and t % MM_TM == 0 and N_SEL <= t // L_SEL <= V7X_LANES
    depth = w_in.shape[0]
    g = A_KV_GROUPS
    xs = x.reshape(t, d)
    n_rest = COL_MA + 2 * d

    w_in_t = jnp.swapaxes(w_in, 1, 2)
    w_q = _cast_rows(w_in_t, 0, A_WIDTH, "cast_wq")
    w_cmp = _cast_rows(w_in_t, SRC_KV, 2 * A_KV_WIDTH, "cast_wcmp")
    w_kv = _cast_rows(w_in_t, SRC_KV + 2 * A_KV_WIDTH, 4 * A_KV_WIDTH, "cast_wkv")
    w_rest = _cast_rows(w_in_t, SRC_REST, n_rest, "cast_wrest")
    w_a = _cast_rows(w_out_a, 0, w_out_a.shape[1], "cast_wa")
    w_b = _cast_rows(w_out_b, 0, w_out_b.shape[1], "cast_wb")
    w_o = _cast_rows(w_out, 0, w_out.shape[1], "cast_wo")
    gate_src = _gate_weight_columns()
    w_gate = jnp.take(w_in_t[:, SRC_GATES:SRC_REST, :], jnp.asarray(np.maximum(gate_src, 0)), axis=1)
    w_gate = jnp.where(jnp.asarray(gate_src >= 0)[None, :, None], w_gate, 0.0).astype(BF16)
    w1 = jnp.stack([cmp_w1_k, cmp_w1_v], axis=1).astype(BF16)
    w2 = jnp.stack([cmp_w2_k, cmp_w2_v], axis=1).astype(BF16)
    slab = STRIDE_CMP * HEAD_DIM
    pe = jnp.stack([cmp_pe_k, cmp_pe_v], axis=1).reshape(depth, 2, 2, slab)
    w_s = sgu_w.astype(BF16)

    bias_tiles = _bias_tiles(rel_bias)
    bias_strip = _bias_strip(rel_bias)

    h = _rmsnorm(xs, pre_norm[0], BF16)
    for layer in range(depth):
        q2 = _mm(h, w_q, layer, BF16, act="log2_scale", nt=True, name="q_proj")
        x_slabs = _cmp_proj(h, w_cmp, layer)
        k_plain, vt = _kv_proj(h, w_kv, layer, 2)
        gates_t = _gate_proj(h, w_gate, layer)
        za = _mm(h, w_rest, layer, BF16, act="silu", nt=True, col0=COL_ZA, n=A_WIDTH, name="za_proj")
        uv = _mm(h, w_rest, layer, BF16, act="gelu", nt=True, col0=COL_U, n=2 * B_WIDTH, name="uv_proj")
        zb = _mm(h, w_rest, layer, BF16, act="silu", nt=True, col0=COL_ZB, n=B_WIDTH, name="zb_proj")
        m_gates = _mm(h, w_rest, layer, BF16, act="sigmoid", nt=True, col0=COL_MA, n=2 * d, name="m_proj")

        kc, kct = _compress(x_slabs, pe[layer], w1[layer], w2[layer])
        o_cmp, sel_neg = _cmp_attn(q2, kc, kct, bias_strip, gates_t, t)
        o_sel = _sel_attn(q2, k_plain, vt, sel_neg, bias_tiles, gates_t, t)
        o_win = _win_attn(q2, k_plain, vt, bias_tiles, gates_t, t)

        o_b = _gmlp(uv, zb, sgu_norm[layer], w_s[layer], sgu_b[layer].T, t)
        merged = _merge(o_cmp, o_sel, o_win, za, o_b, m_gates, w_a, w_b, layer, t, d)
        y = _mm(merged, w_o, layer, F32, name="out_proj")
        g_next = pre_norm[layer + 1] if layer + 1 < depth else pre_norm[layer]
        xs, h = _post_norm_residual(xs, y, post_norm[layer], g_next)
    del g
    return xs.reshape(batch, t, d)
```

```python
import functools
import math

import numpy as np
import jax
import jax.numpy as jnp
from jax import lax
from jax.experimental import pallas as pl
from jax.experimental.pallas import tpu as pltpu

F32 = jnp.float32
BF16 = jnp.bfloat16

A_HEADS = 16
A_KV_GROUPS = 4
HEADS_PER_GROUP = A_HEADS // A_KV_GROUPS
HEAD_DIM = 128
A_WIDTH = A_HEADS * HEAD_DIM
A_KV_WIDTH = A_KV_GROUPS * HEAD_DIM
GROUP_Q_WIDTH = HEADS_PER_GROUP * HEAD_DIM
L_CMP = 32
STRIDE_CMP = 16
L_SEL = 64
N_SEL = 16
WINDOW = 512
B_GROUPS = 16
CHUNK = 128
B_WIDTH = 2048
NUM_BUCKETS = 32
MAX_DISTANCE = 128
NORM_EPS = 1e-6
NEG_INF = -1e30
SCALE = HEAD_DIM ** -0.5
LOG2_E = math.log2(math.e)
SCALE_LOG2 = SCALE * LOG2_E
N_GATE_COLS = 3 * A_HEADS

V7X_LANES = 128
V7X_SUBLANES = 8
V7X_SCOPED_VMEM_CAP_BYTES = 60000 * 1024

ATTN_TILE = 512
CMP_TQ = 256
CMP_ROWS_PER_TQ = CMP_TQ // STRIDE_CMP
MM_TM = 1024
MM_TN = 1024

SRC_KV = A_WIDTH
SRC_GATES = SRC_KV + 6 * A_KV_WIDTH
SRC_REST = SRC_GATES + N_GATE_COLS
COL_ZA = 0
COL_U = COL_ZA + A_WIDTH
COL_V = COL_U + B_WIDTH
COL_ZB = COL_V + B_WIDTH
COL_MA = COL_ZB + B_WIDTH


def _vmem_limit(block_bytes, temp_bytes=0):
    need = 2 * sum(block_bytes) + temp_bytes + (4 << 20)
    return int(min(max(need, 16 << 20), V7X_SCOPED_VMEM_CAP_BYTES))


def _params(sem, block_bytes, temp_bytes=0):
    return pltpu.CompilerParams(dimension_semantics=sem,
                                vmem_limit_bytes=_vmem_limit(block_bytes, temp_bytes))


def _t5_thresholds():
    n = np.arange(0, 4 * MAX_DISTANCE)
    max_exact = NUM_BUCKETS // 2
    nf = np.maximum(n, 1).astype(np.float32)
    large = max_exact + (np.log(nf / np.float32(max_exact)) / np.float32(math.log(MAX_DISTANCE / max_exact))
                         * np.float32(NUM_BUCKETS - max_exact)).astype(np.int32)
    bucket = np.where(n < max_exact, n, np.minimum(large, NUM_BUCKETS - 1))
    assert np.all(np.diff(bucket) >= 0) and bucket[-1] == NUM_BUCKETS - 1
    return [int(np.argmax(bucket >= b)) for b in range(1, NUM_BUCKETS)]


_T5_THR = _t5_thresholds()


def _sigmoid(x):
    return 1.0 / (1.0 + jnp.exp(-x))


def _silu(x):
    return x * _sigmoid(x)


def _gelu(x):
    return jax.nn.gelu(x, approximate=True)


_ACTS = {"none": lambda v: v, "silu": _silu, "gelu": _gelu, "sigmoid": _sigmoid}


def _rmsnorm_kernel(x_ref, g_ref, o_ref):
    x = x_ref[...]
    ms = jnp.mean(x * x, axis=-1, keepdims=True)
    o_ref[...] = (x * lax.rsqrt(ms + NORM_EPS) * g_ref[...]).astype(o_ref.dtype)


def _rmsnorm(x, g, out_dtype):
    t, d = x.shape
    tm = min(256, t)
    return pl.pallas_call(
        _rmsnorm_kernel,
        grid=(t // tm,),
        in_specs=[pl.BlockSpec((tm, d), lambda i: (i, 0)), pl.BlockSpec((1, d), lambda i: (0, 0))],
        out_specs=pl.BlockSpec((tm, d), lambda i: (i, 0)),
        out_shape=jax.ShapeDtypeStruct((t, d), out_dtype),
        compiler_params=_params(("parallel",), [tm * d * 4, tm * d * 4]),
        name="rmsnorm",
    )(x, g.reshape(1, d))


def _post_kernel(x_ref, y_ref, g_ref, gn_ref, o_ref, h_ref):
    y = y_ref[...]
    ms = jnp.mean(y * y, axis=-1, keepdims=True)
    x = x_ref[...] + y * lax.rsqrt(ms + NORM_EPS) * g_ref[...]
    o_ref[...] = x
    ms_x = jnp.mean(x * x, axis=-1, keepdims=True)
    h_ref[...] = (x * lax.rsqrt(ms_x + NORM_EPS) * gn_ref[...]).astype(h_ref.dtype)


def _post_norm_residual(x, y, g, g_next):
    t, d = x.shape
    tm = min(256, t)
    row = pl.BlockSpec((tm, d), lambda i: (i, 0))
    vec = pl.BlockSpec((1, d), lambda i: (0, 0))
    return pl.pallas_call(
        _post_kernel,
        grid=(t // tm,),
        in_specs=[row, row, vec, vec],
        out_specs=[row, row],
        out_shape=[jax.ShapeDtypeStruct((t, d), F32), jax.ShapeDtypeStruct((t, d), BF16)],
        compiler_params=_params(("parallel",), [tm * d * 4] * 4),
        name="post_norm_residual",
    )(x, y, g.reshape(1, d), g_next.reshape(1, d))


def _cast_kernel(w_ref, o_ref):
    o_ref[...] = w_ref[...].astype(o_ref.dtype)


def _cast_rows(w, row0, nrows, name):
    n_layers, _, c = w.shape
    assert row0 % (2 * V7X_SUBLANES) == 0
    tr = next(r for r in (512, 256, 128) if nrows % r == 0)
    return pl.pallas_call(
        _cast_kernel,
        grid=(n_layers, nrows // tr),
        in_specs=[pl.BlockSpec((pl.Element(1), pl.Element(tr), pl.Element(c)),
                               lambda l, j: (l, pl.multiple_of(row0 + j * tr, 2 * V7X_SUBLANES), 0))],
        out_specs=pl.BlockSpec((1, tr, c), lambda l, j: (l, j, 0)),
        out_shape=jax.ShapeDtypeStruct((n_layers, nrows, c), BF16),
        compiler_params=_params(("parallel", "parallel"), [tr * c * 4, tr * c * 2], tr * c * 4),
        name=name,
    )(w)


def _nt_dot(a, b):
    return lax.dot_general(a, b, (((1,), (1,)), ((), ())), preferred_element_type=F32)


def _mm_kernel(a_ref, b_ref, o_ref, *, act, nt, n_chunks):
    a = a_ref[...]
    cw = o_ref.shape[1] // n_chunks
    for c in range(n_chunks):
        cols = slice(c * cw, (c + 1) * cw)
        if nt:
            acc = _nt_dot(a, b_ref[0, cols, :])
        else:
            acc = jnp.dot(a, b_ref[0, :, cols], preferred_element_type=F32)
        o_ref[:, cols] = _ACTS[act](acc).astype(o_ref.dtype)


def _mm(a, b, layer, out_dtype, act="none", nt=False, col0=0, n=None, name="mm"):
    m, k = a.shape
    n_all = b.shape[1] if nt else b.shape[2]
    n = n_all - col0 if n is None else n
    tm = min(MM_TM, m)
    tn = next(c for c in (MM_TN, 512, 256, 128) if n % c == 0 and col0 % c == 0)
    j0 = col0 // tn
    n_chunks = 1 if act == "none" else max(1, tn // 256)
    osz = jnp.dtype(out_dtype).itemsize
    if nt:
        b_spec = pl.BlockSpec((1, tn, k), lambda i, j: (layer, j0 + j, 0))
    else:
        b_spec = pl.BlockSpec((1, k, tn), lambda i, j: (layer, 0, j0 + j))
    return pl.pallas_call(
        functools.partial(_mm_kernel, act=act, nt=nt, n_chunks=n_chunks),
        grid=(m // tm, n // tn),
        in_specs=[pl.BlockSpec((tm, k), lambda i, j: (i, 0)), b_spec],
        out_specs=pl.BlockSpec((tm, tn), lambda i, j: (i, j)),
        out_shape=jax.ShapeDtypeStruct((m, n), out_dtype),
        compiler_params=_params(("parallel", "arbitrary"), [tm * k * 2, k * tn * 2, tm * tn * osz],
                                2 * tm * tn * 4),
        name=name,
    )(a, b)


def _q_proj_kernel(a_ref, b_ref, o_ref):
    o_ref[...] = (_nt_dot(b_ref[0], a_ref[...]) * SCALE_LOG2).astype(o_ref.dtype)


def _q_proj(h, w_q, layer):
    t, k = h.shape
    n = w_q.shape[1]
    tm = min(MM_TM, t)
    tn = min(MM_TN, n)
    return pl.pallas_call(
        _q_proj_kernel,
        grid=(t // tm, n // tn),
        in_specs=[pl.BlockSpec((tm, k), lambda i, j: (i, 0)),
                  pl.BlockSpec((1, tn, k), lambda i, j: (layer, j, 0))],
        out_specs=pl.BlockSpec((tn, tm), lambda i, j: (j, i)),
        out_shape=jax.ShapeDtypeStruct((n, t), BF16),
        compiler_params=_params(("parallel", "arbitrary"), [tm * k * 2, k * tn * 2, tm * tn * 2],
                                2 * tm * tn * 4),
        name="q_proj",
    )(h, w_q)


def _gate_proj_kernel(a_ref, b_ref, o_ref):
    o_ref[...] = _sigmoid(_nt_dot(b_ref[0], a_ref[...]))


def _gate_proj(h, w_gate, layer):
    t, k = h.shape
    tm = min(MM_TM, t)
    return pl.pallas_call(
        _gate_proj_kernel,
        grid=(t // tm,),
        in_specs=[pl.BlockSpec((tm, k), lambda i: (i, 0)),
                  pl.BlockSpec((1, V7X_LANES, k), lambda i: (layer, 0, 0))],
        out_specs=pl.BlockSpec((V7X_LANES, tm), lambda i: (0, i)),
        out_shape=jax.ShapeDtypeStruct((V7X_LANES, t), F32),
        compiler_params=_params(("parallel",), [tm * k * 2, k * V7X_LANES * 2, tm * V7X_LANES * 4]),
        name="gate_proj",
    )(h, w_gate)


def _cmp_proj_kernel(a_ref, b_ref, o_ref, scr):
    acc = _nt_dot(a_ref[...], b_ref[0])
    n_sg = scr.shape[0]
    slab_rows = o_ref.shape[1]
    for sg in range(n_sg):
        scr[sg] = acc[:, sg * HEAD_DIM:(sg + 1) * HEAD_DIM]
    for sg in range(n_sg):
        for l in range(STRIDE_CMP):
            o_ref[sg, :, l * HEAD_DIM:(l + 1) * HEAD_DIM] = (
                scr[sg, pl.ds(l, slab_rows, stride=STRIDE_CMP), :].astype(o_ref.dtype))


def _cmp_proj(h, w_cmp, layer):
    t, k = h.shape
    tm = min(MM_TM, t)
    n_sg = 2 * A_KV_GROUPS
    n = n_sg * HEAD_DIM
    slab = STRIDE_CMP * HEAD_DIM
    return pl.pallas_call(
        _cmp_proj_kernel,
        grid=(t // tm,),
        in_specs=[pl.BlockSpec((tm, k), lambda i: (i, 0)), pl.BlockSpec((1, n, k), lambda i: (layer, 0, 0))],
        out_specs=pl.BlockSpec((n_sg, tm // STRIDE_CMP, slab), lambda i: (0, i, 0)),
        out_shape=jax.ShapeDtypeStruct((n_sg, t // STRIDE_CMP, slab), BF16),
        scratch_shapes=[pltpu.VMEM((n_sg, tm, HEAD_DIM), F32)],
        compiler_params=_params(("parallel",), [tm * k * 2, k * n * 2, tm * n * 2], 3 * tm * n * 4),
        name="cmp_proj",
    )(h, w_cmp)


def _kv_proj_kernel(a_ref, b_ref, k_ref, vt_ref):
    a = a_ref[...]
    k_ref[...] = _nt_dot(a, b_ref[0, :A_KV_WIDTH, :]).astype(k_ref.dtype)
    vt = _nt_dot(b_ref[0, A_KV_WIDTH:, :], a).astype(vt_ref.dtype)
    for g in range(A_KV_GROUPS):
        for s in range(vt_ref.shape[2]):
            vt_ref[0, g, s] = vt[g * HEAD_DIM:(g + 1) * HEAD_DIM, s * ATTN_TILE:(s + 1) * ATTN_TILE]


def _kv_proj(h, w_kv, layer, n_sets):
    t, k = h.shape
    tm = min(MM_TM, t)
    tiles = tm // ATTN_TILE
    return pl.pallas_call(
        _kv_proj_kernel,
        grid=(t // tm, n_sets),
        in_specs=[pl.BlockSpec((tm, k), lambda i, j: (i, 0)),
                  pl.BlockSpec((1, 2 * A_KV_WIDTH, k), lambda i, j: (layer, j, 0))],
        out_specs=[pl.BlockSpec((tm, A_KV_WIDTH), lambda i, j: (i, j)),
                   pl.BlockSpec((1, A_KV_GROUPS, tiles, HEAD_DIM, ATTN_TILE), lambda i, j: (j, 0, i, 0, 0))],
        out_shape=[jax.ShapeDtypeStruct((t, n_sets * A_KV_WIDTH), BF16),
                   jax.ShapeDtypeStruct((n_sets, A_KV_GROUPS, t // ATTN_TILE, HEAD_DIM, ATTN_TILE), BF16)],
        compiler_params=_params(("parallel", "arbitrary"),
                                [tm * k * 2, k * 2 * A_KV_WIDTH * 2, tm * 2 * A_KV_WIDTH * 2],
                                3 * tm * 2 * A_KV_WIDTH * 4),
        name="kv_proj",
    )(h, w_kv)


def _t5_bias_rel(dist, rb_ref, h):
    far = rb_ref[NUM_BUCKETS - 1, h]
    val = jnp.zeros(dist.shape, F32)
    for b in range(NUM_BUCKETS - 2, -1, -1):
        val = jnp.where(dist < _T5_THR[b], (rb_ref[b, h] - far) * LOG2_E, val)
    return val


def _bias_tile_kernel(rb_ref, o_ref):
    h = pl.program_id(0)
    sb = V7X_LANES
    key = lax.broadcasted_iota(jnp.int32, (sb, sb), 0)
    qry = lax.broadcasted_iota(jnp.int32, (sb, sb), 1)
    far_dist = _T5_THR[-1]
    for kind, (delta, limit) in enumerate(((0, None), (ATTN_TILE, None), (ATTN_TILE, WINDOW))):
        for bi in range(ATTN_TILE // sb):
            for bj in range(ATTN_TILE // sb):
                base = delta + sb * (bj - bi)
                lo, hi = base - (sb - 1), base + (sb - 1)
                if hi < 0 or (limit is not None and lo >= limit):
                    val = jnp.full((sb, sb), NEG_INF, F32)
                else:
                    dist = base + qry - key
                    val = _t5_bias_rel(dist, rb_ref, h) if lo < far_dist else jnp.zeros((sb, sb), F32)
                    if lo < 0:
                        val = jnp.where(dist >= 0, val, NEG_INF)
                    if limit is not None and hi >= limit:
                        val = jnp.where(dist < limit, val, NEG_INF)
                o_ref[0, kind, bi * sb:(bi + 1) * sb, bj * sb:(bj + 1) * sb] = val


def _bias_tiles(rel_bias):
    return pl.pallas_call(
        _bias_tile_kernel,
        grid=(A_HEADS,),
        in_specs=[pl.BlockSpec(memory_space=pltpu.SMEM)],
        out_specs=pl.BlockSpec((1, 3, ATTN_TILE, ATTN_TILE), lambda h: (h, 0, 0, 0)),
        out_shape=jax.ShapeDtypeStruct((A_HEADS, 3, ATTN_TILE, ATTN_TILE), F32),
        compiler_params=_params(("parallel",), [3 * ATTN_TILE * ATTN_TILE * 4]),
        name="bias_tiles",
    )(rel_bias)


def _bias_strip_kernel(rb_ref, o_ref):
    h = pl.program_id(0)
    shape = (2 * CMP_ROWS_PER_TQ, CMP_TQ)
    c = lax.broadcasted_iota(jnp.int32, shape, 0)
    a = lax.broadcasted_iota(jnp.int32, shape, 1)
    dist = a - STRIDE_CMP * c + (CMP_TQ - (L_CMP - 1))
    o_ref[0] = jnp.where(dist >= 0, _t5_bias_rel(dist, rb_ref, h), NEG_INF)


def _bias_strip(rel_bias):
    rows = 2 * CMP_ROWS_PER_TQ
    return pl.pallas_call(
        _bias_strip_kernel,
        grid=(A_HEADS,),
        in_specs=[pl.BlockSpec(memory_space=pltpu.SMEM)],
        out_specs=pl.BlockSpec((1, rows, CMP_TQ), lambda h: (h, 0, 0)),
        out_shape=jax.ShapeDtypeStruct((A_HEADS, rows, CMP_TQ), F32),
        compiler_params=_params(("parallel",), [rows * CMP_TQ * 4]),
        name="bias_strip",
    )(rel_bias)


def _compress_kernel(x_ref, pe_ref, w1_ref, w2_ref, o_ref, ot_ref):
    x = x_ref[0].astype(F32)
    half = x.shape[1]
    lo = (x + pe_ref[0, 0:1, :]).astype(BF16)
    hi = (x + pe_ref[0, 1:2, :]).astype(BF16)
    a = jnp.dot(lo, w1_ref[0, :half, :], preferred_element_type=F32)
    b = jnp.dot(hi, w1_ref[0, half:, :], preferred_element_type=F32)
    n_rows = x.shape[0]
    hidden = a + pltpu.roll(b, n_rows - 1, 0)
    out = jnp.dot(_silu(hidden).astype(BF16), w2_ref[0], preferred_element_type=F32)
    o_ref[0] = out.astype(o_ref.dtype)
    ot_ref[0] = out.T.astype(ot_ref.dtype)


def _compress(x_slabs, pe, w1, w2):
    ng, n_rows, half = x_slabs.shape
    g = A_KV_GROUPS
    return pl.pallas_call(
        _compress_kernel,
        grid=(2, g),
        in_specs=[pl.BlockSpec((1, n_rows, half), lambda s, i: (s * g + i, 0, 0)),
                  pl.BlockSpec((1, 2, half), lambda s, i: (s, 0, 0)),
                  pl.BlockSpec((1, 2 * half, HEAD_DIM), lambda s, i: (s, 0, 0)),
                  pl.BlockSpec((1, HEAD_DIM, HEAD_DIM), lambda s, i: (s, 0, 0))],
        out_specs=[pl.BlockSpec((1, n_rows, HEAD_DIM), lambda s, i: (s * g + i, 0, 0)),
                   pl.BlockSpec((1, HEAD_DIM, n_rows), lambda s, i: (s * g + i, 0, 0))],
        out_shape=[jax.ShapeDtypeStruct((ng, n_rows, HEAD_DIM), BF16),
                   jax.ShapeDtypeStruct((ng, HEAD_DIM, n_rows), BF16)],
        compiler_params=_params(("parallel", "parallel"),
                                [n_rows * half * 2, 2 * half * 4, 2 * half * HEAD_DIM * 2],
                                4 * n_rows * half * 4),
        name="compress",
    )(x_slabs, pe, w1, w2)


def _cmp_attn_kernel(q_ref, kc_ref, vct_ref, strip_ref, gate_ref, o_ref, sel_ref, s_scr, pg_scr):
    qi = pl.program_id(1)
    n_rows = kc_ref.shape[1]
    n_sel = n_rows // (L_SEL // STRIDE_CMP)
    tq = CMP_TQ
    rpt = CMP_ROWS_PER_TQ
    t_row = qi * tq + lax.broadcasted_iota(jnp.int32, (1, tq), 1)
    any_valid = jnp.where(t_row >= L_CMP - 1, 1.0, 0.0)
    key_live = lax.broadcasted_iota(jnp.int32, (n_rows, tq), 0) < (qi + 1) * rpt
    strip_rows = pl.ds(pl.multiple_of(qi * rpt, rpt), 2 * rpt)
    for r in range(HEADS_PER_GROUP):
        qt = q_ref[r * HEAD_DIM:(r + 1) * HEAD_DIM, :]
        st = jnp.dot(kc_ref[0], qt, preferred_element_type=F32)
        s_scr[r, 0:rpt, :] = jnp.zeros((rpt, tq), F32)
        s_scr[r, rpt:rpt + n_rows, :] = jnp.where(key_live, st, NEG_INF)
        s_scr[r, strip_rows, :] = s_scr[r, strip_rows, :] + strip_ref[r]
    pg = jnp.zeros((n_rows, tq), F32)
    for r in range(HEADS_PER_GROUP):
        s = s_scr[r, rpt:rpt + n_rows, :]
        m = jnp.max(s, axis=0, keepdims=True)
        e = jnp.exp2(s - m)
        p = e * (any_valid / jnp.sum(e, axis=0, keepdims=True))
        pg = pg + p
        ot = jnp.dot(vct_ref[0], p.astype(BF16), preferred_element_type=F32) * gate_ref[r:r + 1, :]
        o_ref[:, r * HEAD_DIM:(r + 1) * HEAD_DIM] = ot.T.astype(o_ref.dtype)

    pad = V7X_SUBLANES
    lane_tiles = tq // V7X_LANES
    for c in range(lane_tiles):
        pg_scr[c, 0:pad, :] = jnp.zeros((pad, V7X_LANES), F32)
        pg_scr[c, pad:pad + n_rows, :] = pg[:, c * V7X_LANES:(c + 1) * V7X_LANES]
    per_sel = L_SEL // STRIDE_CMP

    def rows(off):
        return jnp.concatenate(
            [pg_scr[c, pl.ds(pad + off, n_sel, stride=per_sel), :] for c in range(lane_tiles)], axis=1)

    imp = (rows(-1) + rows(3)) + 2.0 * (rows(0) + rows(1) + rows(2))

    j = lax.broadcasted_iota(jnp.int32, (n_sel, tq), 0)
    jf = j.astype(F32)
    jt = jnp.right_shift(t_row, int(math.log2(L_SEL)))
    forced = (j == 0) | (j == jt) | (j == jt - 1)
    cand = (j >= 1) & (j <= jt - 2)
    score = jnp.where(cand, imp, -1.0)
    sel = jnp.where(forced, 1.0, 0.0)
    for _ in range(N_SEL - 3):
        mx = jnp.max(score, axis=0, keepdims=True)
        first_j = jnp.min(jnp.where(score == mx, jf, float(n_sel)), axis=0, keepdims=True)
        pick = (jf == first_j) & (mx >= 0.0)
        sel = jnp.where(pick, 1.0, sel)
        score = jnp.where(pick, -2.0, score)
    neg = jnp.where(sel > 0.5, 0.0, NEG_INF)
    if n_sel < V7X_LANES:
        neg = jnp.concatenate([neg, jnp.zeros((V7X_LANES - n_sel, tq), F32)], axis=0)
    sel_ref[0] = neg.astype(sel_ref.dtype)


def _cmp_attn(q2, kc, vct, strip, gates_t, t):
    n_rows = kc.shape[1]
    g = A_KV_GROUPS
    tq = CMP_TQ
    return pl.pallas_call(
        _cmp_attn_kernel,
        grid=(g, t // tq),
        in_specs=[pl.BlockSpec((GROUP_Q_WIDTH, tq), lambda gi, qi: (gi, qi)),
                  pl.BlockSpec((1, n_rows, HEAD_DIM), lambda gi, qi: (gi, 0, 0)),
                  pl.BlockSpec((1, HEAD_DIM, n_rows), lambda gi, qi: (g + gi, 0, 0)),
                  pl.BlockSpec((HEADS_PER_GROUP, 2 * CMP_ROWS_PER_TQ, tq), lambda gi, qi: (gi, 0, 0)),
                  _gate_spec(0, tq)],
        out_specs=[pl.BlockSpec((tq, GROUP_Q_WIDTH), lambda gi, qi: (qi, gi)),
                   pl.BlockSpec((1, V7X_LANES, tq), lambda gi, qi: (gi, 0, qi))],
        out_shape=[jax.ShapeDtypeStruct((t, A_WIDTH), BF16),
                   jax.ShapeDtypeStruct((g, V7X_LANES, t), BF16)],
        scratch_shapes=[pltpu.VMEM((HEADS_PER_GROUP, CMP_ROWS_PER_TQ + n_rows, tq), F32),
                        pltpu.VMEM((tq // V7X_LANES, n_rows + V7X_SUBLANES, V7X_LANES), F32)],
        compiler_params=_params(("parallel", "arbitrary"),
                                [tq * GROUP_Q_WIDTH * 2 * 2, n_rows * HEAD_DIM * 2 * 2, tq * V7X_LANES * 2],
                                12 * n_rows * tq * 4),
        name="cmp_attn",
    )(q2, kc, vct, strip, gates_t)


def _gate_spec(branch, tq):
    return pl.BlockSpec((V7X_SUBLANES, tq), lambda gi, qi: (branch * A_KV_GROUPS + gi, qi))


def _softmax_step(r, z, z_max, vt, m_scr, l_scr, acc_scr):
    m_prev = m_scr[r]
    m_new = jnp.maximum(m_prev, z_max)
    p = jnp.exp2(z - m_new)
    alpha = jnp.exp2(m_prev - m_new)
    l_scr[r] = alpha * l_scr[r] + jnp.sum(p, axis=0, keepdims=True)
    acc_scr[r] = acc_scr[r] * alpha + jnp.dot(vt, p.astype(BF16), preferred_element_type=F32)
    m_scr[r] = m_new


def _pipelined_tile(kvj, kvj_next, logits_head, softmax_head):
    last = HEADS_PER_GROUP - 1
    for r in range(HEADS_PER_GROUP):
        if r < last:
            logits_head(r + 1, kvj)
        elif kvj_next is not None:
            logits_head(0, kvj_next)
        softmax_head(r, kvj)


def _softmax_init(m_scr, l_scr, acc_scr):
    m_scr[...] = jnp.full(m_scr.shape, NEG_INF, F32)
    l_scr[...] = jnp.zeros(l_scr.shape, F32)
    acc_scr[...] = jnp.zeros(acc_scr.shape, F32)


def _softmax_finish(gate_ref, o_ref, l_scr, acc_scr):
    for r in range(HEADS_PER_GROUP):
        ot = acc_scr[r] * (gate_ref[r:r + 1, :] / l_scr[r])
        o_ref[:, r * HEAD_DIM:(r + 1) * HEAD_DIM] = ot.T.astype(o_ref.dtype)


def _sel_attn_kernel(q_ref, k_ref, e_ref, vt_ref, sel_ref, bt_diag_ref, bt_prev_ref, gate_ref, o_ref,
                     qaug_scr, z_scr, zmax_scr, m_scr, l_scr, acc_scr):
    qi = pl.program_id(1)
    tk = ATTN_TILE
    _softmax_init(m_scr, l_scr, acc_scr)
    for r in range(HEADS_PER_GROUP):
        qaug_scr[r, 0:HEAD_DIM, :] = q_ref[r * HEAD_DIM:(r + 1) * HEAD_DIM, :]
        qaug_scr[r, HEAD_DIM:2 * HEAD_DIM, :] = sel_ref[0]

    def logits_head(r, kvj):
        rows = pl.ds(pl.multiple_of(kvj * tk, tk), tk)
        k_aug = jnp.concatenate([k_ref[rows, :], e_ref[rows, :]], axis=1)
        z = jnp.dot(k_aug, qaug_scr[r], preferred_element_type=F32)
        z_scr[r] = z
        zmax_scr[r] = jnp.max(z, axis=0, keepdims=True)

    def softmax_near(bt_ref):
        def softmax_head(r, kvj):
            z = z_scr[r] + bt_ref[r, 0]
            _softmax_step(r, z, jnp.max(z, axis=0, keepdims=True), vt_ref[0, 0, kvj], m_scr, l_scr, acc_scr)
        return softmax_head

    def softmax_far(r, kvj):
        _softmax_step(r, z_scr[r], zmax_scr[r], vt_ref[0, 0, kvj], m_scr, l_scr, acc_scr)

    logits_head(0, qi)
    _pipelined_tile(qi, jnp.maximum(qi - 1, 0), logits_head, softmax_near(bt_diag_ref))

    @pl.when(qi >= 1)
    def _():
        _pipelined_tile(qi - 1, 0, logits_head, softmax_near(bt_prev_ref))

    n_far = jnp.maximum(qi - 1, 0)

    def far_tile(kvj, carry):
        _pipelined_tile(kvj, jnp.minimum(kvj + 1, n_far - 1), logits_head, softmax_far)
        return carry

    lax.fori_loop(0, n_far, far_tile, 0)
    _softmax_finish(gate_ref, o_ref, l_scr, acc_scr)


def _win_attn_kernel(q_ref, k_ref, vt_ref, bt_diag_ref, bt_prev_ref, gate_ref, oc_ref, os_ref, za_ref, o_ref,
                     z_scr, m_scr, l_scr, acc_scr):
    qi = pl.program_id(1)
    tk = ATTN_TILE
    _softmax_init(m_scr, l_scr, acc_scr)

    def logits_head(r, kvj):
        k = k_ref[pl.ds(pl.multiple_of(kvj * tk, tk), tk), :]
        z_scr[r] = jnp.dot(k, q_ref[r * HEAD_DIM:(r + 1) * HEAD_DIM, :], preferred_element_type=F32)

    def softmax_near(bt_ref):
        def softmax_head(r, kvj):
            z = z_scr[r] + bt_ref[r, 0]
            _softmax_step(r, z, jnp.max(z, axis=0, keepdims=True), vt_ref[0, 0, kvj], m_scr, l_scr, acc_scr)
        return softmax_head

    logits_head(0, qi)
    _pipelined_tile(qi, jnp.maximum(qi - 1, 0), logits_head, softmax_near(bt_diag_ref))

    @pl.when(qi >= 1)
    def _():
        _pipelined_tile(qi - 1, None, logits_head, softmax_near(bt_prev_ref))

    for r in range(HEADS_PER_GROUP):
        cols = slice(r * HEAD_DIM, (r + 1) * HEAD_DIM)
        o_win = (acc_scr[r] * (gate_ref[r:r + 1, :] / l_scr[r])).T
        o_a = oc_ref[:, cols].astype(F32) + os_ref[:, cols].astype(F32) + o_win
        o_ref[:, cols] = (o_a * za_ref[:, cols].astype(F32)).astype(o_ref.dtype)


def _attn_scratch(tq):
    return [pltpu.VMEM((HEADS_PER_GROUP, ATTN_TILE, tq), F32), pltpu.VMEM((HEADS_PER_GROUP, 1, tq), F32),
            pltpu.VMEM((HEADS_PER_GROUP, 1, tq), F32), pltpu.VMEM((HEADS_PER_GROUP, HEAD_DIM, tq), F32)]


def _attn_common_specs(t, k_set, branch):
    tq = ATTN_TILE
    q_spec = pl.BlockSpec((GROUP_Q_WIDTH, tq), lambda gi, qi: (gi, qi))
    k_spec = pl.BlockSpec((t, HEAD_DIM), lambda gi, qi: (0, k_set * A_KV_GROUPS + gi))
    vt_spec = pl.BlockSpec((1, 1, t // ATTN_TILE, HEAD_DIM, ATTN_TILE), lambda gi, qi: (k_set, gi, 0, 0, 0))
    out_spec = pl.BlockSpec((tq, GROUP_Q_WIDTH), lambda gi, qi: (qi, gi))
    return q_spec, k_spec, vt_spec, _gate_spec(branch, tq), out_spec


def _bt_spec(kind):
    return pl.BlockSpec((HEADS_PER_GROUP, 1, ATTN_TILE, ATTN_TILE), lambda gi, qi: (gi, kind, 0, 0))


def _attn_block_bytes(t):
    tq = ATTN_TILE
    return [tq * GROUP_Q_WIDTH * 2 * 2, t * HEAD_DIM * 2 * 2, 2 * HEADS_PER_GROUP * tq * tq * 4]


def _sel_attn(q2, k_plain, vt, sel_neg, bt, gates_t, t):
    tq = ATTN_TILE
    q_spec, k_spec, vt_spec, gate_spec, out_spec = _attn_common_specs(t, 0, 1)
    sel_spec = pl.BlockSpec((1, V7X_LANES, tq), lambda gi, qi: (gi, 0, qi))
    e_onehot = (jnp.arange(t, dtype=jnp.int32)[:, None] // L_SEL
                == jnp.arange(V7X_LANES, dtype=jnp.int32)[None, :]).astype(BF16)
    e_spec = pl.BlockSpec((t, V7X_LANES), lambda gi, qi: (0, 0))
    return pl.pallas_call(
        _sel_attn_kernel,
        grid=(A_KV_GROUPS, t // tq),
        in_specs=[q_spec, k_spec, e_spec, vt_spec, sel_spec, _bt_spec(0), _bt_spec(1), gate_spec],
        out_specs=out_spec,
        out_shape=jax.ShapeDtypeStruct((t, A_WIDTH), BF16),
        scratch_shapes=([pltpu.VMEM((HEADS_PER_GROUP, 2 * HEAD_DIM, tq), BF16)] + _attn_scratch(tq)[:1]
                        + [pltpu.VMEM((HEADS_PER_GROUP, 1, tq), F32)] + _attn_scratch(tq)[1:]),
        compiler_params=_params(("parallel", "arbitrary"),
                                _attn_block_bytes(t) + [t * V7X_LANES * 2, tq * V7X_LANES * 2],
                                (8 + HEADS_PER_GROUP) * tq * tq * 4),
        name="sel_attn",
    )(q2, k_plain, e_onehot, vt, sel_neg, bt, bt, gates_t)


def _win_attn(q2, k_plain, vt, bt, gates_t, o_cmp, o_sel, za, t):
    tq = ATTN_TILE
    q_spec, k_spec, vt_spec, gate_spec, out_spec = _attn_common_specs(t, 1, 2)
    return pl.pallas_call(
        _win_attn_kernel,
        grid=(A_KV_GROUPS, t // tq),
        in_specs=[q_spec, k_spec, vt_spec, _bt_spec(0), _bt_spec(2), gate_spec, out_spec, out_spec, out_spec],
        out_specs=out_spec,
        out_shape=jax.ShapeDtypeStruct((t, A_WIDTH), BF16),
        scratch_shapes=_attn_scratch(tq),
        compiler_params=_params(("parallel", "arbitrary"),
                                _attn_block_bytes(t) + [3 * tq * GROUP_Q_WIDTH * 2],
                                (8 + HEADS_PER_GROUP) * tq * tq * 4),
        name="win_attn",
    )(q2, k_plain, vt, bt, bt, gates_t, o_cmp, o_sel, za)


def _gmlp_kernel(u_ref, v_ref, z_ref, ng_ref, w_ref, bt_ref, o_ref):
    v = v_ref[...].astype(F32)
    ms = jnp.mean(v * v, axis=-1, keepdims=True)
    vn = (v * lax.rsqrt(ms + NORM_EPS) * ng_ref[...]).astype(BF16)
    p_idx = lax.broadcasted_iota(jnp.int32, (CHUNK, CHUNK), 0)
    q_idx = lax.broadcasted_iota(jnp.int32, (CHUNK, CHUNK), 1)
    causal = q_idx <= p_idx
    gd = B_WIDTH // B_GROUPS
    for gg in range(B_GROUPS):
        cols = slice(gg * gd, (gg + 1) * gd)
        w = jnp.where(causal, w_ref[gg], jnp.zeros((), w_ref.dtype))
        f = jnp.dot(w, vn[:, cols], preferred_element_type=F32) + bt_ref[:, gg:gg + 1]
        o_ref[:, cols] = (u_ref[:, cols].astype(F32) * f * z_ref[:, cols].astype(F32)).astype(o_ref.dtype)


def _gmlp(uv, zb, norm_g, w_s, b_t, t):
    bw = B_WIDTH

    def col_spec(col):
        return pl.BlockSpec((CHUNK, bw), lambda i: (i, col))

    return pl.pallas_call(
        _gmlp_kernel,
        grid=(t // CHUNK,),
        in_specs=[col_spec(0), col_spec(1), col_spec(0),
                  pl.BlockSpec((1, bw), lambda i: (0, 0)),
                  pl.BlockSpec((B_GROUPS, CHUNK, CHUNK), lambda i: (0, 0, 0)),
                  pl.BlockSpec((CHUNK, B_GROUPS), lambda i: (0, 0))],
        out_specs=pl.BlockSpec((CHUNK, bw), lambda i: (i, 0)),
        out_shape=jax.ShapeDtypeStruct((t, bw), BF16),
        compiler_params=_params(("parallel",), [CHUNK * bw * 2] * 4 + [B_GROUPS * CHUNK * CHUNK * 2],
                                4 * CHUNK * bw * 4),
        name="gmlp",
    )(uv, uv, zb, norm_g.reshape(1, bw), w_s, b_t)


def _merge_kernel(oa_ref, ob_ref, wa_ref, wb_ref, ma_ref, mb_ref, o_ref):
    ya = jnp.dot(oa_ref[...], wa_ref[0], preferred_element_type=F32)
    yb = jnp.dot(ob_ref[...], wb_ref[0], preferred_element_type=F32)
    o_ref[...] = (ma_ref[...].astype(F32) * ya + mb_ref[...].astype(F32) * yb).astype(o_ref.dtype)


def _merge(o_a, o_b, m_gates, w_a, w_b, layer, t, d):
    tm = min(MM_TM, t)
    tn = min(512, d)
    aw = A_WIDTH
    bw = B_WIDTH
    return pl.pallas_call(
        _merge_kernel,
        grid=(t // tm, d // tn),
        in_specs=[pl.BlockSpec((tm, aw), lambda i, j: (i, 0)),
                  pl.BlockSpec((tm, bw), lambda i, j: (i, 0)),
                  pl.BlockSpec((1, aw, tn), lambda i, j: (layer, 0, j)),
                  pl.BlockSpec((1, bw, tn), lambda i, j: (layer, 0, j)),
                  pl.BlockSpec((tm, tn), lambda i, j: (i, j)),
                  pl.BlockSpec((tm, tn), lambda i, j: (i, d // tn + j))],
        out_specs=pl.BlockSpec((tm, tn), lambda i, j: (i, j)),
        out_shape=jax.ShapeDtypeStruct((t, d), BF16),
        compiler_params=_params(("parallel", "arbitrary"),
                                [tm * aw * 2] * 2 + [aw * tn * 2, bw * tn * 2] + [tm * tn * 2] * 3,
                                4 * tm * tn * 4),
        name="merge",
    )(o_a, o_b, w_a, w_b, m_gates, m_gates)


def _gate_weight_columns():
    src = np.full((V7X_LANES,), -1, np.int64)
    for br in range(3):
        for g in range(A_KV_GROUPS):
            for r in range(HEADS_PER_GROUP):
                src[(br * A_KV_GROUPS + g) * V7X_SUBLANES + r] = (g * HEADS_PER_GROUP + r) * 3 + br
    return src


def kernel(x, rel_bias, pre_norm, w_in, cmp_pe_k, cmp_w1_k, cmp_w2_k, cmp_pe_v, cmp_w1_v, cmp_w2_v,
           w_out_a, sgu_norm, sgu_w, sgu_b, w_out_b, w_out, post_norm):
    batch, t, d = x.shape
    assert batch == 1 and t % MM_TM == 0 and N_SEL <= t // L_SEL <= V7X_LANES
    depth = w_in.shape[0]
    g = A_KV_GROUPS
    xs = x.reshape(t, d)
    n_rest = COL_MA + 2 * d

    w_in_t = jnp.swapaxes(w_in, 1, 2)
    w_q = _cast_rows(w_in_t, 0, A_WIDTH, "cast_wq")
    w_cmp = _cast_rows(w_in_t, SRC_KV, 2 * A_KV_WIDTH, "cast_wcmp")
    w_kv = _cast_rows(w_in_t, SRC_KV + 2 * A_KV_WIDTH, 4 * A_KV_WIDTH, "cast_wkv")
    w_rest = _cast_rows(w_in_t, SRC_REST, n_rest, "cast_wrest")
    w_a = _cast_rows(w_out_a, 0, w_out_a.shape[1], "cast_wa")
    w_b = _cast_rows(w_out_b, 0, w_out_b.shape[1], "cast_wb")
    w_o = _cast_rows(w_out, 0, w_out.shape[1], "cast_wo")
    gate_src = _gate_weight_columns()
    w_gate = jnp.take(w_in_t[:, SRC_GATES:SRC_REST, :], jnp.asarray(np.maximum(gate_src, 0)), axis=1)
    w_gate = jnp.where(jnp.asarray(gate_src >= 0)[None, :, None], w_gate, 0.0).astype(BF16)
    w1 = jnp.stack([cmp_w1_k, cmp_w1_v], axis=1).astype(BF16)
    w2 = jnp.stack([cmp_w2_k, cmp_w2_v], axis=1).astype(BF16)
    slab = STRIDE_CMP * HEAD_DIM
    pe = jnp.stack([cmp_pe_k, cmp_pe_v], axis=1).reshape(depth, 2, 2, slab)
    w_s = sgu_w.astype(BF16)

    bias_tiles = _bias_tiles(rel_bias)
    bias_strip = _bias_strip(rel_bias)

    h = _rmsnorm(xs, pre_norm[0], BF16)
    for layer in range(depth):
        q2 = _q_proj(h, w_q, layer)
        x_slabs = _cmp_proj(h, w_cmp, layer)
        k_plain, vt = _kv_proj(h, w_kv, layer, 2)
        gates_t = _gate_proj(h, w_gate, layer)
        za = _mm(h, w_rest, layer, BF16, act="silu", nt=True, col0=COL_ZA, n=A_WIDTH, name="za_proj")
        uv = _mm(h, w_rest, layer, BF16, act="gelu", nt=True, col0=COL_U, n=2 * B_WIDTH, name="uv_proj")
        zb = _mm(h, w_rest, layer, BF16, act="silu", nt=True, col0=COL_ZB, n=B_WIDTH, name="zb_proj")
        m_gates = _mm(h, w_rest, layer, BF16, act="sigmoid", nt=True, col0=COL_MA, n=2 * d, name="m_proj")

        kc, kct = _compress(x_slabs, pe[layer], w1[layer], w2[layer])
        o_cmp, sel_neg = _cmp_attn(q2, kc, kct, bias_strip, gates_t, t)
        o_sel = _sel_attn(q2, k_plain, vt, sel_neg, bias_tiles, gates_t, t)
        o_a = _win_attn(q2, k_plain, vt, bias_tiles, gates_t, o_cmp, o_sel, za, t)

        o_b = _gmlp(uv, zb, sgu_norm[layer], w_s[layer], sgu_b[layer].T, t)
        merged = _merge(o_a, o_b, m_gates, w_a, w_b, layer, t, d)
        y = _mm(merged, w_o, layer, F32, name="out_proj")
        g_next = pre_norm[layer + 1] if layer + 1 < depth else pre_norm[layer]
        xs, h = _post_norm_residual(xs, y, post_norm[layer], g_next)
    del g
    return xs.reshape(batch, t, d)
```

```python
import functools
import math

import numpy as np
import jax
import jax.numpy as jnp
from jax import lax
from jax.experimental import pallas as pl
from jax.experimental.pallas import tpu as pltpu

F32 = jnp.float32
BF16 = jnp.bfloat16

A_HEADS = 16
A_KV_GROUPS = 4
HEADS_PER_GROUP = A_HEADS // A_KV_GROUPS
HEAD_DIM = 128
A_WIDTH = A_HEADS * HEAD_DIM
A_KV_WIDTH = A_KV_GROUPS * HEAD_DIM
GROUP_Q_WIDTH = HEADS_PER_GROUP * HEAD_DIM
L_CMP = 32
STRIDE_CMP = 16
L_SEL = 64
N_SEL = 16
WINDOW = 512
B_GROUPS = 16
CHUNK = 128
B_WIDTH = 2048
NUM_BUCKETS = 32
MAX_DISTANCE = 128
NORM_EPS = 1e-6
NEG_INF = -1e30
SCALE = HEAD_DIM ** -0.5
LOG2_E = math.log2(math.e)
SCALE_LOG2 = SCALE * LOG2_E
N_GATE_COLS = 3 * A_HEADS

V7X_LANES = 128
V7X_SUBLANES = 8
V7X_SCOPED_VMEM_CAP_BYTES = 60000 * 1024

ATTN_TILE = 512
CMP_TQ = 256
CMP_ROWS_PER_TQ = CMP_TQ // STRIDE_CMP
CMP_SUB = 2
MM_TM = 1024
MM_TN = 1024
W32_TN = 512

SRC_KV = A_WIDTH
SRC_GATES = SRC_KV + 6 * A_KV_WIDTH
SRC_REST = SRC_GATES + N_GATE_COLS
COL_ZA = 0
COL_U = COL_ZA + A_WIDTH
COL_V = COL_U + B_WIDTH
COL_ZB = COL_V + B_WIDTH
COL_MA = COL_ZB + B_WIDTH


def _vmem_limit(block_bytes, temp_bytes=0):
    need = 2 * sum(block_bytes) + temp_bytes + (4 << 20)
    return int(min(max(need, 16 << 20), V7X_SCOPED_VMEM_CAP_BYTES))


def _params(sem, block_bytes, temp_bytes=0, flags=None):
    return pltpu.CompilerParams(dimension_semantics=sem,
                                vmem_limit_bytes=_vmem_limit(block_bytes, temp_bytes), flags=flags)


def _t5_thresholds():
    n = np.arange(0, 4 * MAX_DISTANCE)
    max_exact = NUM_BUCKETS // 2
    nf = np.maximum(n, 1).astype(np.float32)
    large = max_exact + (np.log(nf / np.float32(max_exact)) / np.float32(math.log(MAX_DISTANCE / max_exact))
                         * np.float32(NUM_BUCKETS - max_exact)).astype(np.int32)
    bucket = np.where(n < max_exact, n, np.minimum(large, NUM_BUCKETS - 1))
    assert np.all(np.diff(bucket) >= 0) and bucket[-1] == NUM_BUCKETS - 1
    return [int(np.argmax(bucket >= b)) for b in range(1, NUM_BUCKETS)]


_T5_THR = _t5_thresholds()


def _sigmoid(x):
    return 1.0 / (1.0 + jnp.exp(-x))


def _silu(x):
    return x * _sigmoid(x)


def _gelu(x):
    return jax.nn.gelu(x, approximate=True)


_ACTS = {"none": lambda v: v, "silu": _silu, "gelu": _gelu, "sigmoid": _sigmoid}


def _rmsnorm_kernel(x_ref, g_ref, o_ref):
    x = x_ref[...]
    ms = jnp.mean(x * x, axis=-1, keepdims=True)
    o_ref[...] = (x * lax.rsqrt(ms + NORM_EPS) * g_ref[...]).astype(o_ref.dtype)


def _rmsnorm(x, g, out_dtype):
    t, d = x.shape
    tm = min(256, t)
    return pl.pallas_call(
        _rmsnorm_kernel,
        grid=(t // tm,),
        in_specs=[pl.BlockSpec((tm, d), lambda i: (i, 0)), pl.BlockSpec((1, d), lambda i: (0, 0))],
        out_specs=pl.BlockSpec((tm, d), lambda i: (i, 0)),
        out_shape=jax.ShapeDtypeStruct((t, d), out_dtype),
        compiler_params=_params(("parallel",), [tm * d * 4, tm * d * 4]),
        name="rmsnorm",
    )(x, g.reshape(1, d))


def _post_kernel(x_ref, y_ref, g_ref, gn_ref, o_ref, h_ref):
    y = y_ref[...]
    ms = jnp.mean(y * y, axis=-1, keepdims=True)
    x = x_ref[...] + y * lax.rsqrt(ms + NORM_EPS) * g_ref[...]
    o_ref[...] = x
    ms_x = jnp.mean(x * x, axis=-1, keepdims=True)
    h_ref[...] = (x * lax.rsqrt(ms_x + NORM_EPS) * gn_ref[...]).astype(h_ref.dtype)


def _post_norm_residual(x, y, g, g_next):
    t, d = x.shape
    tm = min(256, t)
    row = pl.BlockSpec((tm, d), lambda i: (i, 0))
    vec = pl.BlockSpec((1, d), lambda i: (0, 0))
    return pl.pallas_call(
        _post_kernel,
        grid=(t // tm,),
        in_specs=[row, row, vec, vec],
        out_specs=[row, row],
        out_shape=[jax.ShapeDtypeStruct((t, d), F32), jax.ShapeDtypeStruct((t, d), BF16)],
        compiler_params=_params(("parallel",), [tm * d * 4] * 4),
        name="post_norm_residual",
    )(x, y, g.reshape(1, d), g_next.reshape(1, d))


def _nt_dot(a, b):
    return lax.dot_general(a, b, (((1,), (1,)), ((), ())), preferred_element_type=F32)


def _nn_dot(a, b):
    return lax.dot_general(a, b, (((1,), (0,)), ((), ())), preferred_element_type=F32)


def _mm_kernel(a_ref, b_ref, o_ref, *, act, nt, n_chunks):
    a = a_ref[...]
    cw = o_ref.shape[1] // n_chunks
    for c in range(n_chunks):
        cols = slice(c * cw, (c + 1) * cw)
        if nt:
            acc = _nt_dot(a, b_ref[0, cols, :])
        else:
            acc = _nn_dot(a, b_ref[0, :, cols])
        o_ref[:, cols] = _ACTS[act](acc).astype(o_ref.dtype)


def _mm(a, b, layer, out_dtype, name):
    m, k = a.shape
    n = b.shape[2]
    tm = min(MM_TM, m)
    tn = min(W32_TN, n)
    osz = jnp.dtype(out_dtype).itemsize
    return pl.pallas_call(
        functools.partial(_mm_kernel, act="none", nt=False, n_chunks=1),
        grid=(n // tn, m // tm),
        in_specs=[pl.BlockSpec((tm, k), lambda j, i: (i, 0)),
                  pl.BlockSpec((1, k, tn), lambda j, i: (layer, 0, j))],
        out_specs=pl.BlockSpec((tm, tn), lambda j, i: (i, j)),
        out_shape=jax.ShapeDtypeStruct((m, n), out_dtype),
        compiler_params=_params(("arbitrary", "arbitrary"), [tm * k * 2, k * tn * 4, tm * tn * osz],
                                2 * tm * tn * 4),
        name=name,
    )(a, b)


def _w_rows_spec(layer, row0, rows, k):
    assert row0 % V7X_SUBLANES == 0
    return pl.BlockSpec((pl.Element(1), pl.Element(rows), pl.Element(k)),
                        lambda j, i: (layer, pl.multiple_of(row0 + j * rows, V7X_SUBLANES), 0))


def _mm_w32(a, w_t, layer, row0, n, out_dtype, act, name):
    m, k = a.shape
    tm = min(MM_TM, m)
    tn = next(c for c in (W32_TN, 256, 128) if n % c == 0)
    n_chunks = 1 if act == "none" else max(1, tn // 256)
    osz = jnp.dtype(out_dtype).itemsize
    return pl.pallas_call(
        functools.partial(_mm_kernel, act=act, nt=True, n_chunks=n_chunks),
        grid=(n // tn, m // tm),
        in_specs=[pl.BlockSpec((tm, k), lambda j, i: (i, 0)), _w_rows_spec(layer, row0, tn, k)],
        out_specs=pl.BlockSpec((tm, tn), lambda j, i: (i, j)),
        out_shape=jax.ShapeDtypeStruct((m, n), out_dtype),
        compiler_params=_params(("arbitrary", "arbitrary"), [tm * k * 2, k * tn * 4, tm * tn * osz],
                                2 * tm * tn * 4),
        name=name,
    )(a, w_t)


def _q_proj_kernel(a_ref, b_ref, o_ref):
    o_ref[...] = (_nt_dot(b_ref[0], a_ref[...]) * SCALE_LOG2).astype(o_ref.dtype)


def _q_proj(h, w_t, layer):
    t, k = h.shape
    n = A_WIDTH
    tm = min(MM_TM, t)
    tn = W32_TN
    return pl.pallas_call(
        _q_proj_kernel,
        grid=(n // tn, t // tm),
        in_specs=[pl.BlockSpec((tm, k), lambda j, i: (i, 0)), _w_rows_spec(layer, 0, tn, k)],
        out_specs=pl.BlockSpec((tn, tm), lambda j, i: (j, i)),
        out_shape=jax.ShapeDtypeStruct((n, t), BF16),
        compiler_params=_params(("arbitrary", "arbitrary"), [tm * k * 2, k * tn * 4, tm * tn * 2],
                                2 * tm * tn * 4),
        name="q_proj",
    )(h, w_t)


def _gate_proj_kernel(a_ref, b_ref, o_ref):
    o_ref[...] = _sigmoid(_nt_dot(b_ref[0], a_ref[...]))


def _gate_proj(h, w_gate, layer):
    t, k = h.shape
    tm = min(MM_TM, t)
    return pl.pallas_call(
        _gate_proj_kernel,
        grid=(t // tm,),
        in_specs=[pl.BlockSpec((tm, k), lambda i: (i, 0)),
                  pl.BlockSpec((1, V7X_LANES, k), lambda i: (layer, 0, 0))],
        out_specs=pl.BlockSpec((V7X_LANES, tm), lambda i: (0, i)),
        out_shape=jax.ShapeDtypeStruct((V7X_LANES, t), F32),
        compiler_params=_params(("parallel",), [tm * k * 2, k * V7X_LANES * 2, tm * V7X_LANES * 4]),
        name="gate_proj",
    )(h, w_gate)


def _cmp_proj_kernel(a_ref, b_ref, o_ref, scr):
    acc = _nt_dot(a_ref[...], b_ref[0])
    n_sg = scr.shape[0]
    slab_rows = o_ref.shape[1]
    for sg in range(n_sg):
        scr[sg] = acc[:, sg * HEAD_DIM:(sg + 1) * HEAD_DIM]
    for sg in range(n_sg):
        for l in range(STRIDE_CMP):
            o_ref[sg, :, l * HEAD_DIM:(l + 1) * HEAD_DIM] = (
                scr[sg, pl.ds(l, slab_rows, stride=STRIDE_CMP), :].astype(o_ref.dtype))


def _cmp_proj(h, w_t, layer):
    t, k = h.shape
    tm = min(MM_TM, t)
    g = A_KV_GROUPS
    slab = STRIDE_CMP * HEAD_DIM
    return pl.pallas_call(
        _cmp_proj_kernel,
        grid=(2, t // tm),
        in_specs=[pl.BlockSpec((tm, k), lambda j, i: (i, 0)), _w_rows_spec(layer, SRC_KV, A_KV_WIDTH, k)],
        out_specs=pl.BlockSpec((g, tm // STRIDE_CMP, slab), lambda j, i: (j, i, 0)),
        out_shape=jax.ShapeDtypeStruct((2 * g, t // STRIDE_CMP, slab), BF16),
        scratch_shapes=[pltpu.VMEM((g, tm, HEAD_DIM), F32)],
        compiler_params=_params(("arbitrary", "arbitrary"),
                                [tm * k * 2, k * A_KV_WIDTH * 4, tm * A_KV_WIDTH * 2], 3 * tm * A_KV_WIDTH * 4),
        name="cmp_proj",
    )(h, w_t)


def _kv_proj_kernel(a_ref, b_ref, k_ref, vt_ref):
    a = a_ref[...]
    k_ref[...] = _nt_dot(a, b_ref[0, :A_KV_WIDTH, :]).astype(k_ref.dtype)
    vt = _nt_dot(b_ref[0, A_KV_WIDTH:, :], a).astype(vt_ref.dtype)
    for g in range(A_KV_GROUPS):
        for s in range(vt_ref.shape[2]):
            vt_ref[0, g, s] = vt[g * HEAD_DIM:(g + 1) * HEAD_DIM, s * ATTN_TILE:(s + 1) * ATTN_TILE]


def _kv_proj(h, w_t, layer, row0, n_sets):
    t, k = h.shape
    tm = min(MM_TM // 2, t)
    tiles = tm // ATTN_TILE
    return pl.pallas_call(
        _kv_proj_kernel,
        grid=(n_sets, t // tm),
        in_specs=[pl.BlockSpec((tm, k), lambda j, i: (i, 0)), _w_rows_spec(layer, row0, 2 * A_KV_WIDTH, k)],
        out_specs=[pl.BlockSpec((tm, A_KV_WIDTH), lambda j, i: (i, j)),
                   pl.BlockSpec((1, A_KV_GROUPS, tiles, HEAD_DIM, ATTN_TILE), lambda j, i: (j, 0, i, 0, 0))],
        out_shape=[jax.ShapeDtypeStruct((t, n_sets * A_KV_WIDTH), BF16),
                   jax.ShapeDtypeStruct((n_sets, A_KV_GROUPS, t // ATTN_TILE, HEAD_DIM, ATTN_TILE), BF16)],
        compiler_params=_params(("arbitrary", "arbitrary"),
                                [tm * k * 2, k * 2 * A_KV_WIDTH * 4, tm * 2 * A_KV_WIDTH * 2],
                                3 * tm * 2 * A_KV_WIDTH * 4),
        name="kv_proj",
    )(h, w_t)


def _t5_bias_rel(dist, rb_ref, h):
    far = rb_ref[NUM_BUCKETS - 1, h]
    val = jnp.zeros(dist.shape, F32)
    for b in range(NUM_BUCKETS - 2, -1, -1):
        val = jnp.where(dist < _T5_THR[b], (rb_ref[b, h] - far) * LOG2_E, val)
    return val


def _bias_tile_kernel(rb_ref, o_ref):
    h = pl.program_id(0)
    sb = V7X_LANES
    key = lax.broadcasted_iota(jnp.int32, (sb, sb), 0)
    qry = lax.broadcasted_iota(jnp.int32, (sb, sb), 1)
    far_dist = _T5_THR[-1]
    for kind, (delta, limit) in enumerate(((0, None), (ATTN_TILE, None), (ATTN_TILE, WINDOW))):
        for bi in range(ATTN_TILE // sb):
            for bj in range(ATTN_TILE // sb):
                base = delta + sb * (bj - bi)
                lo, hi = base - (sb - 1), base + (sb - 1)
                if hi < 0 or (limit is not None and lo >= limit):
                    val = jnp.full((sb, sb), NEG_INF, F32)
                else:
                    dist = base + qry - key
                    val = _t5_bias_rel(dist, rb_ref, h) if lo < far_dist else jnp.zeros((sb, sb), F32)
                    if lo < 0:
                        val = jnp.where(dist >= 0, val, NEG_INF)
                    if limit is not None and hi >= limit:
                        val = jnp.where(dist < limit, val, NEG_INF)
                o_ref[0, kind, bi * sb:(bi + 1) * sb, bj * sb:(bj + 1) * sb] = val


def _bias_tiles(rel_bias):
    return pl.pallas_call(
        _bias_tile_kernel,
        grid=(A_HEADS,),
        in_specs=[pl.BlockSpec(memory_space=pltpu.SMEM)],
        out_specs=pl.BlockSpec((1, 3, ATTN_TILE, ATTN_TILE), lambda h: (h, 0, 0, 0)),
        out_shape=jax.ShapeDtypeStruct((A_HEADS, 3, ATTN_TILE, ATTN_TILE), F32),
        compiler_params=_params(("parallel",), [3 * ATTN_TILE * ATTN_TILE * 4]),
        name="bias_tiles",
    )(rel_bias)


def _bias_strip_kernel(rb_ref, o_ref):
    h = pl.program_id(0)
    shape = (2 * CMP_ROWS_PER_TQ, CMP_TQ)
    c = lax.broadcasted_iota(jnp.int32, shape, 0)
    a = lax.broadcasted_iota(jnp.int32, shape, 1)
    dist = a - STRIDE_CMP * c + (CMP_TQ - (L_CMP - 1))
    o_ref[0] = jnp.where(dist >= 0, _t5_bias_rel(dist, rb_ref, h), NEG_INF)


def _bias_strip(rel_bias):
    rows = 2 * CMP_ROWS_PER_TQ
    return pl.pallas_call(
        _bias_strip_kernel,
        grid=(A_HEADS,),
        in_specs=[pl.BlockSpec(memory_space=pltpu.SMEM)],
        out_specs=pl.BlockSpec((1, rows, CMP_TQ), lambda h: (h, 0, 0)),
        out_shape=jax.ShapeDtypeStruct((A_HEADS, rows, CMP_TQ), F32),
        compiler_params=_params(("parallel",), [rows * CMP_TQ * 4]),
        name="bias_strip",
    )(rel_bias)


def _compress_kernel(x_ref, pe_ref, w1_ref, w2_ref, o_ref, ot_ref):
    x = x_ref[0].astype(F32)
    half = x.shape[1]
    lo = (x + pe_ref[0, 0:1, :]).astype(BF16)
    hi = (x + pe_ref[0, 1:2, :]).astype(BF16)
    a = jnp.dot(lo, w1_ref[0, :half, :], preferred_element_type=F32)
    b = jnp.dot(hi, w1_ref[0, half:, :], preferred_element_type=F32)
    n_rows = x.shape[0]
    hidden = a + pltpu.roll(b, n_rows - 1, 0)
    out = jnp.dot(_silu(hidden).astype(BF16), w2_ref[0], preferred_element_type=F32)
    o_ref[0] = out.astype(o_ref.dtype)
    ot_ref[0] = out.T.astype(ot_ref.dtype)


def _compress(x_slabs, pe, w1, w2):
    ng, n_rows, half = x_slabs.shape
    g = A_KV_GROUPS
    return pl.pallas_call(
        _compress_kernel,
        grid=(2, g),
        in_specs=[pl.BlockSpec((1, n_rows, half), lambda s, i: (s * g + i, 0, 0)),
                  pl.BlockSpec((1, 2, half), lambda s, i: (s, 0, 0)),
                  pl.BlockSpec((1, 2 * half, HEAD_DIM), lambda s, i: (s, 0, 0)),
                  pl.BlockSpec((1, HEAD_DIM, HEAD_DIM), lambda s, i: (s, 0, 0))],
        out_specs=[pl.BlockSpec((1, n_rows, HEAD_DIM), lambda s, i: (s * g + i, 0, 0)),
                   pl.BlockSpec((1, HEAD_DIM, n_rows), lambda s, i: (s * g + i, 0, 0))],
        out_shape=[jax.ShapeDtypeStruct((ng, n_rows, HEAD_DIM), BF16),
                   jax.ShapeDtypeStruct((ng, HEAD_DIM, n_rows), BF16)],
        compiler_params=_params(("parallel", "parallel"),
                                [n_rows * half * 2, 2 * half * 4, 2 * half * HEAD_DIM * 2],
                                4 * n_rows * half * 4),
        name="compress",
    )(x_slabs, pe, w1, w2)


def _cmp_attn_kernel(q_ref, kc_ref, vct_ref, strip_ref, gate_ref, o_ref, sel_ref, s_scr, pg_scr):
    n_rows = kc_ref.shape[1]
    n_sel = n_rows // (L_SEL // STRIDE_CMP)
    tq = CMP_TQ
    rpt = CMP_ROWS_PER_TQ
    pad = V7X_SUBLANES
    lane_tiles = tq // V7X_LANES
    per_sel = L_SEL // STRIDE_CMP
    sub_qi = [pl.program_id(1) * CMP_SUB + sub for sub in range(CMP_SUB)]
    t_rows = [qi * tq + lax.broadcasted_iota(jnp.int32, (1, tq), 1) for qi in sub_qi]

    for sub, qi in enumerate(sub_qi):
        lanes = slice(sub * tq, (sub + 1) * tq)
        key_live = lax.broadcasted_iota(jnp.int32, (n_rows, tq), 0) < (qi + 1) * rpt
        strip_rows = pl.ds(pl.multiple_of(qi * rpt, rpt), 2 * rpt)
        for r in range(HEADS_PER_GROUP):
            st = jnp.dot(kc_ref[0], q_ref[r * HEAD_DIM:(r + 1) * HEAD_DIM, lanes], preferred_element_type=F32)
            s_scr[sub, r, 0:rpt, :] = jnp.zeros((rpt, tq), F32)
            s_scr[sub, r, rpt:rpt + n_rows, :] = jnp.where(key_live, st, NEG_INF)
            s_scr[sub, r, strip_rows, :] = s_scr[sub, r, strip_rows, :] + strip_ref[r]

    for sub in range(CMP_SUB):
        lanes = slice(sub * tq, (sub + 1) * tq)
        any_valid = jnp.where(t_rows[sub] >= L_CMP - 1, 1.0, 0.0)
        pg = jnp.zeros((n_rows, tq), F32)
        for r in range(HEADS_PER_GROUP):
            s = s_scr[sub, r, rpt:rpt + n_rows, :]
            m = jnp.max(s, axis=0, keepdims=True)
            e = jnp.exp2(s - m)
            p = e * (any_valid / jnp.sum(e, axis=0, keepdims=True))
            pg = pg + p
            ot = jnp.dot(vct_ref[0], p.astype(BF16), preferred_element_type=F32) * gate_ref[r:r + 1, lanes]
            o_ref[lanes, r * HEAD_DIM:(r + 1) * HEAD_DIM] = ot.T.astype(o_ref.dtype)
        for c in range(lane_tiles):
            pg_scr[sub, c, 0:pad, :] = jnp.zeros((pad, V7X_LANES), F32)
            pg_scr[sub, c, pad:pad + n_rows, :] = pg[:, c * V7X_LANES:(c + 1) * V7X_LANES]

    for sub in range(CMP_SUB):
        def rows(off, sub=sub):
            return jnp.concatenate(
                [pg_scr[sub, c, pl.ds(pad + off, n_sel, stride=per_sel), :] for c in range(lane_tiles)], axis=1)

        imp = (rows(-1) + rows(3)) + 2.0 * (rows(0) + rows(1) + rows(2))
        j = lax.broadcasted_iota(jnp.int32, (n_sel, tq), 0)
        jf = j.astype(F32)
        jt = jnp.right_shift(t_rows[sub], int(math.log2(L_SEL)))
        forced = (j == 0) | (j == jt) | (j == jt - 1)
        cand = (j >= 1) & (j <= jt - 2)
        score = jnp.where(cand, imp, -1.0)
        sel = jnp.where(forced, 1.0, 0.0)
        for _ in range(N_SEL - 3):
            mx = jnp.max(score, axis=0, keepdims=True)
            first_j = jnp.min(jnp.where(score == mx, jf, float(n_sel)), axis=0, keepdims=True)
            pick = (jf == first_j) & (mx >= 0.0)
            sel = jnp.where(pick, 1.0, sel)
            score = jnp.where(pick, -2.0, score)
        neg = jnp.where(sel > 0.5, 0.0, NEG_INF)
        if n_sel < V7X_LANES:
            neg = jnp.concatenate([neg, jnp.zeros((V7X_LANES - n_sel, tq), F32)], axis=0)
        sel_ref[0, :, sub * tq:(sub + 1) * tq] = neg.astype(sel_ref.dtype)


def _cmp_attn(q2, kc, vct, strip, gates_t, t):
    n_rows = kc.shape[1]
    g = A_KV_GROUPS
    tq = CMP_TQ
    tqs = CMP_SUB * tq
    return pl.pallas_call(
        _cmp_attn_kernel,
        grid=(g, t // tqs),
        in_specs=[pl.BlockSpec((GROUP_Q_WIDTH, tqs), lambda gi, qi: (gi, qi)),
                  pl.BlockSpec((1, n_rows, HEAD_DIM), lambda gi, qi: (gi, 0, 0)),
                  pl.BlockSpec((1, HEAD_DIM, n_rows), lambda gi, qi: (g + gi, 0, 0)),
                  pl.BlockSpec((HEADS_PER_GROUP, 2 * CMP_ROWS_PER_TQ, tq), lambda gi, qi: (gi, 0, 0)),
                  _gate_spec(0, tqs)],
        out_specs=[pl.BlockSpec((tqs, GROUP_Q_WIDTH), lambda gi, qi: (qi, gi)),
                   pl.BlockSpec((1, V7X_LANES, tqs), lambda gi, qi: (gi, 0, qi))],
        out_shape=[jax.ShapeDtypeStruct((t, A_WIDTH), BF16),
                   jax.ShapeDtypeStruct((g, V7X_LANES, t), BF16)],
        scratch_shapes=[pltpu.VMEM((CMP_SUB, HEADS_PER_GROUP, CMP_ROWS_PER_TQ + n_rows, tq), F32),
                        pltpu.VMEM((CMP_SUB, tq // V7X_LANES, n_rows + V7X_SUBLANES, V7X_LANES), F32)],
        compiler_params=_params(("parallel", "arbitrary"),
                                [tqs * GROUP_Q_WIDTH * 2 * 2, n_rows * HEAD_DIM * 2 * 2, tqs * V7X_LANES * 2],
                                CMP_SUB * 12 * n_rows * tq * 4),
        name="cmp_attn",
    )(q2, kc, vct, strip, gates_t)


def _gate_spec(branch, tq):
    return pl.BlockSpec((V7X_SUBLANES, tq), lambda gi, qi: (branch * A_KV_GROUPS + gi, qi))


def _softmax_step(r, z, z_max, vt, m_scr, l_scr, acc_scr):
    m_prev = m_scr[r]
    m_new = jnp.maximum(m_prev, z_max)
    p = jnp.exp2(z - m_new)
    alpha = jnp.exp2(m_prev - m_new)
    l_scr[r] = alpha * l_scr[r] + jnp.sum(p, axis=0, keepdims=True)
    acc_scr[r] = acc_scr[r] * alpha + jnp.dot(vt, p.astype(BF16), preferred_element_type=F32)
    m_scr[r] = m_new


def _pipelined_tile(kvj, kvj_next, logits_head, softmax_head):
    last = HEADS_PER_GROUP - 1
    for r in range(HEADS_PER_GROUP):
        if r < last:
            logits_head(r + 1, kvj)
        elif kvj_next is not None:
            logits_head(0, kvj_next)
        softmax_head(r, kvj)


def _softmax_init(m_scr, l_scr, acc_scr):
    m_scr[...] = jnp.full(m_scr.shape, NEG_INF, F32)
    l_scr[...] = jnp.zeros(l_scr.shape, F32)
    acc_scr[...] = jnp.zeros(acc_scr.shape, F32)


def _softmax_finish(gate_ref, o_ref, l_scr, acc_scr):
    for r in range(HEADS_PER_GROUP):
        ot = acc_scr[r] * (gate_ref[r:r + 1, :] / l_scr[r])
        o_ref[:, r * HEAD_DIM:(r + 1) * HEAD_DIM] = ot.T.astype(o_ref.dtype)


def _sel_attn_kernel(q_ref, k_ref, e_ref, vt_ref, sel_ref, bt_diag_ref, bt_prev_ref, gate_ref, o_ref,
                     qaug_scr, z_scr, zmax_scr, m_scr, l_scr, acc_scr):
    qi = pl.program_id(1)
    tk = ATTN_TILE
    _softmax_init(m_scr, l_scr, acc_scr)
    for r in range(HEADS_PER_GROUP):
        qaug_scr[r, 0:HEAD_DIM, :] = q_ref[r * HEAD_DIM:(r + 1) * HEAD_DIM, :]
        qaug_scr[r, HEAD_DIM:2 * HEAD_DIM, :] = sel_ref[0]

    def logits_head(r, kvj):
        rows = pl.ds(pl.multiple_of(kvj * tk, tk), tk)
        k_aug = jnp.concatenate([k_ref[rows, :], e_ref[rows, :]], axis=1)
        z = jnp.dot(k_aug, qaug_scr[r], preferred_element_type=F32)
        z_scr[r] = z
        zmax_scr[r] = jnp.max(z, axis=0, keepdims=True)

    def softmax_near(bt_ref):
        def softmax_head(r, kvj):
            z = z_scr[r] + bt_ref[r, 0]
            _softmax_step(r, z, jnp.max(z, axis=0, keepdims=True), vt_ref[0, 0, kvj], m_scr, l_scr, acc_scr)
        return softmax_head

    def softmax_far(r, kvj):
        _softmax_step(r, z_scr[r], zmax_scr[r], vt_ref[0, 0, kvj], m_scr, l_scr, acc_scr)

    logits_head(0, qi)
    _pipelined_tile(qi, jnp.maximum(qi - 1, 0), logits_head, softmax_near(bt_diag_ref))

    @pl.when(qi >= 1)
    def _():
        _pipelined_tile(qi - 1, 0, logits_head, softmax_near(bt_prev_ref))

    n_far = jnp.maximum(qi - 1, 0)

    def far_tile(kvj, carry):
        _pipelined_tile(kvj, jnp.minimum(kvj + 1, n_far - 1), logits_head, softmax_far)
        return carry

    lax.fori_loop(0, n_far, far_tile, 0)
    _softmax_finish(gate_ref, o_ref, l_scr, acc_scr)


def _win_attn_kernel(q_ref, k_ref, vt_ref, bt_diag_ref, bt_prev_ref, gate_ref, oc_ref, os_ref, za_ref, o_ref,
                     z_scr, m_scr, l_scr, acc_scr):
    qi = pl.program_id(1)
    tk = ATTN_TILE
    _softmax_init(m_scr, l_scr, acc_scr)

    def logits_head(r, kvj):
        k = k_ref[pl.ds(pl.multiple_of(kvj * tk, tk), tk), :]
        z_scr[r] = jnp.dot(k, q_ref[r * HEAD_DIM:(r + 1) * HEAD_DIM, :], preferred_element_type=F32)

    def softmax_near(bt_ref):
        def softmax_head(r, kvj):
            z = z_scr[r] + bt_ref[r, 0]
            _softmax_step(r, z, jnp.max(z, axis=0, keepdims=True), vt_ref[0, 0, kvj], m_scr, l_scr, acc_scr)
        return softmax_head

    logits_head(0, qi)
    _pipelined_tile(qi, jnp.maximum(qi - 1, 0), logits_head, softmax_near(bt_diag_ref))

    @pl.when(qi >= 1)
    def _():
        _pipelined_tile(qi - 1, None, logits_head, softmax_near(bt_prev_ref))

    for r in range(HEADS_PER_GROUP):
        cols = slice(r * HEAD_DIM, (r + 1) * HEAD_DIM)
        o_win = (acc_scr[r] * (gate_ref[r:r + 1, :] / l_scr[r])).T
        o_a = oc_ref[:, cols].astype(F32) + os_ref[:, cols].astype(F32) + o_win
        o_ref[:, cols] = (o_a * za_ref[:, cols].astype(F32)).astype(o_ref.dtype)


def _attn_scratch(tq):
    return [pltpu.VMEM((HEADS_PER_GROUP, ATTN_TILE, tq), F32), pltpu.VMEM((HEADS_PER_GROUP, 1, tq), F32),
            pltpu.VMEM((HEADS_PER_GROUP, 1, tq), F32), pltpu.VMEM((HEADS_PER_GROUP, HEAD_DIM, tq), F32)]


def _attn_common_specs(t, k_set, branch):
    tq = ATTN_TILE
    q_spec = pl.BlockSpec((GROUP_Q_WIDTH, tq), lambda gi, qi: (gi, qi))
    k_spec = pl.BlockSpec((t, HEAD_DIM), lambda gi, qi: (0, k_set * A_KV_GROUPS + gi))
    vt_spec = pl.BlockSpec((1, 1, t // ATTN_TILE, HEAD_DIM, ATTN_TILE), lambda gi, qi: (k_set, gi, 0, 0, 0))
    out_spec = pl.BlockSpec((tq, GROUP_Q_WIDTH), lambda gi, qi: (qi, gi))
    return q_spec, k_spec, vt_spec, _gate_spec(branch, tq), out_spec


def _bt_spec(kind):
    return pl.BlockSpec((HEADS_PER_GROUP, 1, ATTN_TILE, ATTN_TILE), lambda gi, qi: (gi, kind, 0, 0))


def _attn_block_bytes(t):
    tq = ATTN_TILE
    return [tq * GROUP_Q_WIDTH * 2 * 2, t * HEAD_DIM * 2 * 2, 2 * HEADS_PER_GROUP * tq * tq * 4]


def _sel_attn(q2, k_plain, vt, sel_neg, bt, gates_t, t):
    tq = ATTN_TILE
    q_spec, k_spec, vt_spec, gate_spec, out_spec = _attn_common_specs(t, 0, 1)
    sel_spec = pl.BlockSpec((1, V7X_LANES, tq), lambda gi, qi: (gi, 0, qi))
    e_onehot = (jnp.arange(t, dtype=jnp.int32)[:, None] // L_SEL
                == jnp.arange(V7X_LANES, dtype=jnp.int32)[None, :]).astype(BF16)
    e_spec = pl.BlockSpec((t, V7X_LANES), lambda gi, qi: (0, 0))
    return pl.pallas_call(
        _sel_attn_kernel,
        grid=(A_KV_GROUPS, t // tq),
        in_specs=[q_spec, k_spec, e_spec, vt_spec, sel_spec, _bt_spec(0), _bt_spec(1), gate_spec],
        out_specs=out_spec,
        out_shape=jax.ShapeDtypeStruct((t, A_WIDTH), BF16),
        scratch_shapes=([pltpu.VMEM((HEADS_PER_GROUP, 2 * HEAD_DIM, tq), BF16)] + _attn_scratch(tq)[:1]
                        + [pltpu.VMEM((HEADS_PER_GROUP, 1, tq), F32)] + _attn_scratch(tq)[1:]),
        compiler_params=_params(("parallel", "arbitrary"),
                                _attn_block_bytes(t) + [t * V7X_LANES * 2, tq * V7X_LANES * 2],
                                (8 + HEADS_PER_GROUP) * tq * tq * 4),
        name="sel_attn",
    )(q2, k_plain, e_onehot, vt, sel_neg, bt, bt, gates_t)


def _win_attn(q2, k_plain, vt, bt, gates_t, o_cmp, o_sel, za, t):
    tq = ATTN_TILE
    q_spec, k_spec, vt_spec, gate_spec, out_spec = _attn_common_specs(t, 1, 2)
    return pl.pallas_call(
        _win_attn_kernel,
        grid=(A_KV_GROUPS, t // tq),
        in_specs=[q_spec, k_spec, vt_spec, _bt_spec(0), _bt_spec(2), gate_spec, out_spec, out_spec, out_spec],
        out_specs=out_spec,
        out_shape=jax.ShapeDtypeStruct((t, A_WIDTH), BF16),
        scratch_shapes=_attn_scratch(tq),
        compiler_params=_params(("parallel", "arbitrary"),
                                _attn_block_bytes(t) + [3 * tq * GROUP_Q_WIDTH * 2],
                                (8 + HEADS_PER_GROUP) * tq * tq * 4),
        name="win_attn",
    )(q2, k_plain, vt, bt, bt, gates_t, o_cmp, o_sel, za)


def _gmlp_kernel(u_ref, v_ref, z_ref, ng_ref, w_ref, bt_ref, o_ref):
    v = v_ref[...].astype(F32)
    ms = jnp.mean(v * v, axis=-1, keepdims=True)
    vn = (v * lax.rsqrt(ms + NORM_EPS) * ng_ref[...]).astype(BF16)
    p_idx = lax.broadcasted_iota(jnp.int32, (CHUNK, CHUNK), 0)
    q_idx = lax.broadcasted_iota(jnp.int32, (CHUNK, CHUNK), 1)
    causal = q_idx <= p_idx
    gd = B_WIDTH // B_GROUPS
    for gg in range(B_GROUPS):
        cols = slice(gg * gd, (gg + 1) * gd)
        w = jnp.where(causal, w_ref[gg], jnp.zeros((), w_ref.dtype))
        f = jnp.dot(w, vn[:, cols], preferred_element_type=F32) + bt_ref[:, gg:gg + 1]
        o_ref[:, cols] = (u_ref[:, cols].astype(F32) * f * z_ref[:, cols].astype(F32)).astype(o_ref.dtype)


def _gmlp(uv, zb, norm_g, w_s, b_t, t):
    bw = B_WIDTH

    def col_spec(col):
        return pl.BlockSpec((CHUNK, bw), lambda i: (i, col))

    return pl.pallas_call(
        _gmlp_kernel,
        grid=(t // CHUNK,),
        in_specs=[col_spec(0), col_spec(1), col_spec(0),
                  pl.BlockSpec((1, bw), lambda i: (0, 0)),
                  pl.BlockSpec((B_GROUPS, CHUNK, CHUNK), lambda i: (0, 0, 0)),
                  pl.BlockSpec((CHUNK, B_GROUPS), lambda i: (0, 0))],
        out_specs=pl.BlockSpec((CHUNK, bw), lambda i: (i, 0)),
        out_shape=jax.ShapeDtypeStruct((t, bw), BF16),
        compiler_params=_params(("parallel",), [CHUNK * bw * 2] * 4 + [B_GROUPS * CHUNK * CHUNK * 2],
                                4 * CHUNK * bw * 4),
        name="gmlp",
    )(uv, uv, zb, norm_g.reshape(1, bw), w_s, b_t)


def _merge_kernel(oa_ref, ob_ref, wa_ref, wb_ref, ma_ref, mb_ref, o_ref):
    ya = _nn_dot(oa_ref[...], wa_ref[0])
    yb = _nn_dot(ob_ref[...], wb_ref[0])
    o_ref[...] = (ma_ref[...].astype(F32) * ya + mb_ref[...].astype(F32) * yb).astype(o_ref.dtype)


def _merge(o_a, o_b, m_gates, w_a, w_b, layer, t, d):
    tm = min(MM_TM, t)
    tn = min(W32_TN, d)
    aw = A_WIDTH
    bw = B_WIDTH
    return pl.pallas_call(
        _merge_kernel,
        grid=(d // tn, t // tm),
        in_specs=[pl.BlockSpec((tm, aw), lambda j, i: (i, 0)),
                  pl.BlockSpec((tm, bw), lambda j, i: (i, 0)),
                  pl.BlockSpec((1, aw, tn), lambda j, i: (layer, 0, j)),
                  pl.BlockSpec((1, bw, tn), lambda j, i: (layer, 0, j)),
                  pl.BlockSpec((tm, tn), lambda j, i: (i, j)),
                  pl.BlockSpec((tm, tn), lambda j, i: (i, d // tn + j))],
        out_specs=pl.BlockSpec((tm, tn), lambda j, i: (i, j)),
        out_shape=jax.ShapeDtypeStruct((t, d), BF16),
        compiler_params=_params(("arbitrary", "arbitrary"),
                                [tm * aw * 2] * 2 + [aw * tn * 4, bw * tn * 4] + [tm * tn * 2] * 3,
                                4 * tm * tn * 4),
        name="merge",
    )(o_a, o_b, w_a, w_b, m_gates, m_gates)


def _gate_weight_columns():
    src = np.full((V7X_LANES,), -1, np.int64)
    for br in range(3):
        for g in range(A_KV_GROUPS):
            for r in range(HEADS_PER_GROUP):
                src[(br * A_KV_GROUPS + g) * V7X_SUBLANES + r] = (g * HEADS_PER_GROUP + r) * 3 + br
    return src


def kernel(x, rel_bias, pre_norm, w_in, cmp_pe_k, cmp_w1_k, cmp_w2_k, cmp_pe_v, cmp_w1_v, cmp_w2_v,
           w_out_a, sgu_norm, sgu_w, sgu_b, w_out_b, w_out, post_norm):
    batch, t, d = x.shape
    assert batch == 1 and t % MM_TM == 0 and N_SEL <= t // L_SEL <= V7X_LANES
    depth = w_in.shape[0]
    g = A_KV_GROUPS
    xs = x.reshape(t, d)

    w_in_t = jnp.swapaxes(w_in, 1, 2)
    gate_src = _gate_weight_columns()
    w_gate = jnp.take(w_in_t[:, SRC_GATES:SRC_REST, :], jnp.asarray(np.maximum(gate_src, 0)), axis=1)
    w_gate = jnp.where(jnp.asarray(gate_src >= 0)[None, :, None], w_gate, 0.0).astype(BF16)
    w1 = jnp.stack([cmp_w1_k, cmp_w1_v], axis=1).astype(BF16)
    w2 = jnp.stack([cmp_w2_k, cmp_w2_v], axis=1).astype(BF16)
    slab = STRIDE_CMP * HEAD_DIM
    pe = jnp.stack([cmp_pe_k, cmp_pe_v], axis=1).reshape(depth, 2, 2, slab)
    w_s = sgu_w.astype(BF16)

    bias_tiles = _bias_tiles(rel_bias)
    bias_strip = _bias_strip(rel_bias)

    h = _rmsnorm(xs, pre_norm[0], BF16)
    for layer in range(depth):
        q2 = _q_proj(h, w_in_t, layer)
        x_slabs = _cmp_proj(h, w_in_t, layer)
        k_plain, vt = _kv_proj(h, w_in_t, layer, SRC_KV + 2 * A_KV_WIDTH, 2)
        gates_t = _gate_proj(h, w_gate, layer)
        za = _mm_w32(h, w_in_t, layer, SRC_REST + COL_ZA, A_WIDTH, BF16, "silu", "za_proj")
        uv = _mm_w32(h, w_in_t, layer, SRC_REST + COL_U, 2 * B_WIDTH, BF16, "gelu", "uv_proj")
        zb = _mm_w32(h, w_in_t, layer, SRC_REST + COL_ZB, B_WIDTH, BF16, "silu", "zb_proj")
        m_gates = _mm_w32(h, w_in_t, layer, SRC_REST + COL_MA, 2 * d, BF16, "sigmoid", "m_proj")

        kc, kct = _compress(x_slabs, pe[layer], w1[layer], w2[layer])
        o_cmp, sel_neg = _cmp_attn(q2, kc, kct, bias_strip, gates_t, t)
        o_sel = _sel_attn(q2, k_plain, vt, sel_neg, bias_tiles, gates_t, t)
        o_a = _win_attn(q2, k_plain, vt, bias_tiles, gates_t, o_cmp, o_sel, za, t)

        o_b = _gmlp(uv, zb, sgu_norm[layer], w_s[layer], sgu_b[layer].T, t)
        merged = _merge(o_a, o_b, m_gates, w_out_a, w_out_b, layer, t, d)
        y = _mm(merged, w_out, layer, F32, name="out_proj")
        g_next = pre_norm[layer + 1] if layer + 1 < depth else pre_norm[layer]
        xs, h = _post_norm_residual(xs, y, post_norm[layer], g_next)
    del g
    return xs.reshape(batch, t, d)
```

```python
import functools
import math

import numpy as np
import jax
import jax.numpy as jnp
from jax import lax
from jax.experimental import pallas as pl
from jax.experimental.pallas import tpu as pltpu

F32 = jnp.float32
BF16 = jnp.bfloat16

A_HEADS = 16
A_KV_GROUPS = 4
HEADS_PER_GROUP = A_HEADS // A_KV_GROUPS
HEAD_DIM = 128
A_WIDTH = A_HEADS * HEAD_DIM
A_KV_WIDTH = A_KV_GROUPS * HEAD_DIM
GROUP_Q_WIDTH = HEADS_PER_GROUP * HEAD_DIM
L_CMP = 32
STRIDE_CMP = 16
L_SEL = 64
N_SEL = 16
WINDOW = 512
B_GROUPS = 16
CHUNK = 128
B_WIDTH = 2048
NUM_BUCKETS = 32
MAX_DISTANCE = 128
NORM_EPS = 1e-6
NEG_INF = -1e30
SCALE = HEAD_DIM ** -0.5
LOG2_E = math.log2(math.e)
SCALE_LOG2 = SCALE * LOG2_E
N_GATE_COLS = 3 * A_HEADS

V7X_LANES = 128
V7X_SUBLANES = 8
V7X_SCOPED_VMEM_CAP_BYTES = 60000 * 1024

ATTN_TILE = 512
CMP_TQ = 256
CMP_ROWS_PER_TQ = CMP_TQ // STRIDE_CMP
CMP_SUB = 2
MM_TM = 1024
MM_TN = 1024
W32_TN = 512

SRC_KV = A_WIDTH
SRC_GATES = SRC_KV + 6 * A_KV_WIDTH
SRC_REST = SRC_GATES + N_GATE_COLS
COL_ZA = 0
COL_U = COL_ZA + A_WIDTH
COL_V = COL_U + B_WIDTH
COL_ZB = COL_V + B_WIDTH
COL_MA = COL_ZB + B_WIDTH


def _vmem_limit(block_bytes, temp_bytes=0):
    need = 2 * sum(block_bytes) + temp_bytes + (4 << 20)
    return int(min(max(need, 16 << 20), V7X_SCOPED_VMEM_CAP_BYTES))


def _params(sem, block_bytes, temp_bytes=0, flags=None):
    return pltpu.CompilerParams(dimension_semantics=sem,
                                vmem_limit_bytes=_vmem_limit(block_bytes, temp_bytes), flags=flags)


def _t5_thresholds():
    n = np.arange(0, 4 * MAX_DISTANCE)
    max_exact = NUM_BUCKETS // 2
    nf = np.maximum(n, 1).astype(np.float32)
    large = max_exact + (np.log(nf / np.float32(max_exact)) / np.float32(math.log(MAX_DISTANCE / max_exact))
                         * np.float32(NUM_BUCKETS - max_exact)).astype(np.int32)
    bucket = np.where(n < max_exact, n, np.minimum(large, NUM_BUCKETS - 1))
    assert np.all(np.diff(bucket) >= 0) and bucket[-1] == NUM_BUCKETS - 1
    return [int(np.argmax(bucket >= b)) for b in range(1, NUM_BUCKETS)]


_T5_THR = _t5_thresholds()


def _sigmoid(x):
    return 1.0 / (1.0 + jnp.exp(-x))


def _silu(x):
    return x * _sigmoid(x)


def _gelu(x):
    return jax.nn.gelu(x, approximate=True)


_ACTS = {"none": lambda v: v, "silu": _silu, "gelu": _gelu, "sigmoid": _sigmoid}


def _rmsnorm_kernel(x_ref, g_ref, o_ref):
    x = x_ref[...]
    ms = jnp.mean(x * x, axis=-1, keepdims=True)
    o_ref[...] = (x * lax.rsqrt(ms + NORM_EPS) * g_ref[...]).astype(o_ref.dtype)


def _rmsnorm(x, g, out_dtype):
    t, d = x.shape
    tm = min(256, t)
    return pl.pallas_call(
        _rmsnorm_kernel,
        grid=(t // tm,),
        in_specs=[pl.BlockSpec((tm, d), lambda i: (i, 0)), pl.BlockSpec((1, d), lambda i: (0, 0))],
        out_specs=pl.BlockSpec((tm, d), lambda i: (i, 0)),
        out_shape=jax.ShapeDtypeStruct((t, d), out_dtype),
        compiler_params=_params(("parallel",), [tm * d * 4, tm * d * 4]),
        name="rmsnorm",
    )(x, g.reshape(1, d))


def _post_kernel(x_ref, y_ref, g_ref, gn_ref, o_ref, h_ref):
    y = y_ref[...].astype(F32)
    ms = jnp.mean(y * y, axis=-1, keepdims=True)
    x = x_ref[...] + y * lax.rsqrt(ms + NORM_EPS) * g_ref[...]
    o_ref[...] = x
    ms_x = jnp.mean(x * x, axis=-1, keepdims=True)
    h_ref[...] = (x * lax.rsqrt(ms_x + NORM_EPS) * gn_ref[...]).astype(h_ref.dtype)


def _post_norm_residual(x, y, g, g_next):
    t, d = x.shape
    tm = min(256, t)
    row = pl.BlockSpec((tm, d), lambda i: (i, 0))
    vec = pl.BlockSpec((1, d), lambda i: (0, 0))
    return pl.pallas_call(
        _post_kernel,
        grid=(t // tm,),
        in_specs=[row, row, vec, vec],
        out_specs=[row, row],
        out_shape=[jax.ShapeDtypeStruct((t, d), F32), jax.ShapeDtypeStruct((t, d), BF16)],
        compiler_params=_params(("parallel",), [tm * d * 4] * 4),
        name="post_norm_residual",
    )(x, y, g.reshape(1, d), g_next.reshape(1, d))


def _nt_dot(a, b):
    return lax.dot_general(a, b, (((1,), (1,)), ((), ())), preferred_element_type=F32)


def _nn_dot(a, b):
    return lax.dot_general(a, b, (((1,), (0,)), ((), ())), preferred_element_type=F32)


def _mm_kernel(a_ref, b_ref, o_ref, *, act, nt, n_chunks):
    a = a_ref[...]
    cw = o_ref.shape[1] // n_chunks
    for c in range(n_chunks):
        cols = slice(c * cw, (c + 1) * cw)
        if nt:
            acc = _nt_dot(a, b_ref[0, cols, :])
        else:
            acc = _nn_dot(a, b_ref[0, :, cols])
        o_ref[:, cols] = _ACTS[act](acc).astype(o_ref.dtype)


def _mm(a, b, layer, out_dtype, name):
    m, k = a.shape
    n = b.shape[2]
    tm = min(MM_TM, m)
    tn = min(W32_TN, n)
    osz = jnp.dtype(out_dtype).itemsize
    return pl.pallas_call(
        functools.partial(_mm_kernel, act="none", nt=False, n_chunks=1),
        grid=(n // tn, m // tm),
        in_specs=[pl.BlockSpec((tm, k), lambda j, i: (i, 0)),
                  pl.BlockSpec((1, k, tn), lambda j, i: (layer, 0, j))],
        out_specs=pl.BlockSpec((tm, tn), lambda j, i: (i, j)),
        out_shape=jax.ShapeDtypeStruct((m, n), out_dtype),
        compiler_params=_params(("arbitrary", "arbitrary"), [tm * k * 2, k * tn * 4, tm * tn * osz],
                                2 * tm * tn * 4),
        name=name,
    )(a, b)


def _w_rows_spec(layer, row0, rows, k):
    assert row0 % V7X_SUBLANES == 0
    return pl.BlockSpec((pl.Element(1), pl.Element(rows), pl.Element(k)),
                        lambda j, i: (layer, pl.multiple_of(row0 + j * rows, V7X_SUBLANES), 0))


def _mm_w32(a, w_t, layer, row0, n, out_dtype, act, name):
    m, k = a.shape
    tm = min(MM_TM, m)
    tn = next(c for c in (W32_TN, 256, 128) if n % c == 0)
    n_chunks = 1 if act == "none" else max(1, tn // 256)
    osz = jnp.dtype(out_dtype).itemsize
    return pl.pallas_call(
        functools.partial(_mm_kernel, act=act, nt=True, n_chunks=n_chunks),
        grid=(n // tn, m // tm),
        in_specs=[pl.BlockSpec((tm, k), lambda j, i: (i, 0)), _w_rows_spec(layer, row0, tn, k)],
        out_specs=pl.BlockSpec((tm, tn), lambda j, i: (i, j)),
        out_shape=jax.ShapeDtypeStruct((m, n), out_dtype),
        compiler_params=_params(("arbitrary", "arbitrary"), [tm * k * 2, k * tn * 4, tm * tn * osz],
                                2 * tm * tn * 4),
        name=name,
    )(a, w_t)


def _q_proj_kernel(a_ref, b_ref, o_ref):
    o_ref[...] = (_nt_dot(b_ref[0], a_ref[...]) * SCALE_LOG2).astype(o_ref.dtype)


def _q_proj(h, w_t, layer):
    t, k = h.shape
    n = A_WIDTH
    tm = min(MM_TM, t)
    tn = W32_TN
    return pl.pallas_call(
        _q_proj_kernel,
        grid=(n // tn, t // tm),
        in_specs=[pl.BlockSpec((tm, k), lambda j, i: (i, 0)), _w_rows_spec(layer, 0, tn, k)],
        out_specs=pl.BlockSpec((tn, tm), lambda j, i: (j, i)),
        out_shape=jax.ShapeDtypeStruct((n, t), BF16),
        compiler_params=_params(("arbitrary", "arbitrary"), [tm * k * 2, k * tn * 4, tm * tn * 2],
                                2 * tm * tn * 4),
        name="q_proj",
    )(h, w_t)


def _gate_proj_kernel(a_ref, b_ref, o_ref):
    o_ref[...] = _sigmoid(_nt_dot(b_ref[0], a_ref[...]))


def _gate_proj(h, w_gate, layer):
    t, k = h.shape
    tm = min(MM_TM, t)
    return pl.pallas_call(
        _gate_proj_kernel,
        grid=(t // tm,),
        in_specs=[pl.BlockSpec((tm, k), lambda i: (i, 0)),
                  pl.BlockSpec((1, V7X_LANES, k), lambda i: (layer, 0, 0))],
        out_specs=pl.BlockSpec((V7X_LANES, tm), lambda i: (0, i)),
        out_shape=jax.ShapeDtypeStruct((V7X_LANES, t), F32),
        compiler_params=_params(("parallel",), [tm * k * 2, k * V7X_LANES * 2, tm * V7X_LANES * 4]),
        name="gate_proj",
    )(h, w_gate)


def _cmp_proj_kernel(a_ref, b_ref, o_ref, scr):
    acc = _nt_dot(a_ref[...], b_ref[0])
    n_sg = scr.shape[0]
    slab_rows = o_ref.shape[1]
    for sg in range(n_sg):
        scr[sg] = acc[:, sg * HEAD_DIM:(sg + 1) * HEAD_DIM]
    for sg in range(n_sg):
        for l in range(STRIDE_CMP):
            o_ref[sg, :, l * HEAD_DIM:(l + 1) * HEAD_DIM] = (
                scr[sg, pl.ds(l, slab_rows, stride=STRIDE_CMP), :].astype(o_ref.dtype))


def _cmp_proj(h, w_t, layer):
    t, k = h.shape
    tm = min(MM_TM, t)
    g = A_KV_GROUPS
    slab = STRIDE_CMP * HEAD_DIM
    return pl.pallas_call(
        _cmp_proj_kernel,
        grid=(2, t // tm),
        in_specs=[pl.BlockSpec((tm, k), lambda j, i: (i, 0)), _w_rows_spec(layer, SRC_KV, A_KV_WIDTH, k)],
        out_specs=pl.BlockSpec((g, tm // STRIDE_CMP, slab), lambda j, i: (j, i, 0)),
        out_shape=jax.ShapeDtypeStruct((2 * g, t // STRIDE_CMP, slab), BF16),
        scratch_shapes=[pltpu.VMEM((g, tm, HEAD_DIM), F32)],
        compiler_params=_params(("arbitrary", "arbitrary"),
                                [tm * k * 2, k * A_KV_WIDTH * 4, tm * A_KV_WIDTH * 2], 3 * tm * A_KV_WIDTH * 4),
        name="cmp_proj",
    )(h, w_t)


def _kv_proj_kernel(a_ref, b_ref, k_ref, vt_ref):
    a = a_ref[...]
    k_ref[...] = _nt_dot(a, b_ref[0, :A_KV_WIDTH, :]).astype(k_ref.dtype)
    vt = _nt_dot(b_ref[0, A_KV_WIDTH:, :], a).astype(vt_ref.dtype)
    for g in range(A_KV_GROUPS):
        for s in range(vt_ref.shape[2]):
            vt_ref[0, g, s] = vt[g * HEAD_DIM:(g + 1) * HEAD_DIM, s * ATTN_TILE:(s + 1) * ATTN_TILE]


def _kv_proj(h, w_t, layer, row0, n_sets):
    t, k = h.shape
    tm = min(MM_TM // 2, t)
    tiles = tm // ATTN_TILE
    return pl.pallas_call(
        _kv_proj_kernel,
        grid=(n_sets, t // tm),
        in_specs=[pl.BlockSpec((tm, k), lambda j, i: (i, 0)), _w_rows_spec(layer, row0, 2 * A_KV_WIDTH, k)],
        out_specs=[pl.BlockSpec((tm, A_KV_WIDTH), lambda j, i: (i, j)),
                   pl.BlockSpec((1, A_KV_GROUPS, tiles, HEAD_DIM, ATTN_TILE), lambda j, i: (j, 0, i, 0, 0))],
        out_shape=[jax.ShapeDtypeStruct((t, n_sets * A_KV_WIDTH), BF16),
                   jax.ShapeDtypeStruct((n_sets, A_KV_GROUPS, t // ATTN_TILE, HEAD_DIM, ATTN_TILE), BF16)],
        compiler_params=_params(("arbitrary", "arbitrary"),
                                [tm * k * 2, k * 2 * A_KV_WIDTH * 4, tm * 2 * A_KV_WIDTH * 2],
                                3 * tm * 2 * A_KV_WIDTH * 4),
        name="kv_proj",
    )(h, w_t)


def _t5_bias_rel(dist, rb_ref, h):
    far = rb_ref[NUM_BUCKETS - 1, h]
    val = jnp.zeros(dist.shape, F32)
    for b in range(NUM_BUCKETS - 2, -1, -1):
        val = jnp.where(dist < _T5_THR[b], (rb_ref[b, h] - far) * LOG2_E, val)
    return val


def _bias_tile_kernel(rb_ref, o_ref):
    h = pl.program_id(0)
    sb = V7X_LANES
    key = lax.broadcasted_iota(jnp.int32, (sb, sb), 0)
    qry = lax.broadcasted_iota(jnp.int32, (sb, sb), 1)
    far_dist = _T5_THR[-1]
    for kind, (delta, limit) in enumerate(((0, None), (ATTN_TILE, None), (ATTN_TILE, WINDOW))):
        for bi in range(ATTN_TILE // sb):
            for bj in range(ATTN_TILE // sb):
                base = delta + sb * (bj - bi)
                lo, hi = base - (sb - 1), base + (sb - 1)
                if hi < 0 or (limit is not None and lo >= limit):
                    val = jnp.full((sb, sb), NEG_INF, F32)
                else:
                    dist = base + qry - key
                    val = _t5_bias_rel(dist, rb_ref, h) if lo < far_dist else jnp.zeros((sb, sb), F32)
                    if lo < 0:
                        val = jnp.where(dist >= 0, val, NEG_INF)
                    if limit is not None and hi >= limit:
                        val = jnp.where(dist < limit, val, NEG_INF)
                o_ref[0, kind, bi * sb:(bi + 1) * sb, bj * sb:(bj + 1) * sb] = val


def _bias_tiles(rel_bias):
    return pl.pallas_call(
        _bias_tile_kernel,
        grid=(A_HEADS,),
        in_specs=[pl.BlockSpec(memory_space=pltpu.SMEM)],
        out_specs=pl.BlockSpec((1, 3, ATTN_TILE, ATTN_TILE), lambda h: (h, 0, 0, 0)),
        out_shape=jax.ShapeDtypeStruct((A_HEADS, 3, ATTN_TILE, ATTN_TILE), F32),
        compiler_params=_params(("parallel",), [3 * ATTN_TILE * ATTN_TILE * 4]),
        name="bias_tiles",
    )(rel_bias)


def _bias_strip_kernel(rb_ref, o_ref):
    h = pl.program_id(0)
    shape = (2 * CMP_ROWS_PER_TQ, CMP_TQ)
    c = lax.broadcasted_iota(jnp.int32, shape, 0)
    a = lax.broadcasted_iota(jnp.int32, shape, 1)
    dist = a - STRIDE_CMP * c + (CMP_TQ - (L_CMP - 1))
    o_ref[0] = jnp.where(dist >= 0, _t5_bias_rel(dist, rb_ref, h), NEG_INF)


def _bias_strip(rel_bias):
    rows = 2 * CMP_ROWS_PER_TQ
    return pl.pallas_call(
        _bias_strip_kernel,
        grid=(A_HEADS,),
        in_specs=[pl.BlockSpec(memory_space=pltpu.SMEM)],
        out_specs=pl.BlockSpec((1, rows, CMP_TQ), lambda h: (h, 0, 0)),
        out_shape=jax.ShapeDtypeStruct((A_HEADS, rows, CMP_TQ), F32),
        compiler_params=_params(("parallel",), [rows * CMP_TQ * 4]),
        name="bias_strip",
    )(rel_bias)


def _compress_kernel(x_ref, pe_ref, w1_ref, w2_ref, o_ref, ot_ref):
    x = x_ref[0].astype(F32)
    half = x.shape[1]
    lo = (x + pe_ref[0, 0:1, :]).astype(BF16)
    hi = (x + pe_ref[0, 1:2, :]).astype(BF16)
    a = jnp.dot(lo, w1_ref[0, :half, :], preferred_element_type=F32)
    b = jnp.dot(hi, w1_ref[0, half:, :], preferred_element_type=F32)
    n_rows = x.shape[0]
    hidden = a + pltpu.roll(b, n_rows - 1, 0)
    out = jnp.dot(_silu(hidden).astype(BF16), w2_ref[0], preferred_element_type=F32)
    o_ref[0] = out.astype(o_ref.dtype)
    ot_ref[0] = out.T.astype(ot_ref.dtype)


def _compress(x_slabs, pe, w1, w2):
    ng, n_rows, half = x_slabs.shape
    g = A_KV_GROUPS
    return pl.pallas_call(
        _compress_kernel,
        grid=(2, g),
        in_specs=[pl.BlockSpec((1, n_rows, half), lambda s, i: (s * g + i, 0, 0)),
                  pl.BlockSpec((1, 2, half), lambda s, i: (s, 0, 0)),
                  pl.BlockSpec((1, 2 * half, HEAD_DIM), lambda s, i: (s, 0, 0)),
                  pl.BlockSpec((1, HEAD_DIM, HEAD_DIM), lambda s, i: (s, 0, 0))],
        out_specs=[pl.BlockSpec((1, n_rows, HEAD_DIM), lambda s, i: (s * g + i, 0, 0)),
                   pl.BlockSpec((1, HEAD_DIM, n_rows), lambda s, i: (s * g + i, 0, 0))],
        out_shape=[jax.ShapeDtypeStruct((ng, n_rows, HEAD_DIM), BF16),
                   jax.ShapeDtypeStruct((ng, HEAD_DIM, n_rows), BF16)],
        compiler_params=_params(("parallel", "parallel"),
                                [n_rows * half * 2, 2 * half * 4, 2 * half * HEAD_DIM * 2],
                                4 * n_rows * half * 4),
        name="compress",
    )(x_slabs, pe, w1, w2)


def _cmp_attn_kernel(q_ref, kc_ref, vct_ref, strip_ref, gate_ref, o_ref, sel_ref, s_scr, pg_scr):
    n_rows = kc_ref.shape[1]
    n_sel = n_rows // (L_SEL // STRIDE_CMP)
    tq = CMP_TQ
    rpt = CMP_ROWS_PER_TQ
    pad = V7X_SUBLANES
    lane_tiles = tq // V7X_LANES
    per_sel = L_SEL // STRIDE_CMP
    sub_qi = [pl.program_id(1) * CMP_SUB + sub for sub in range(CMP_SUB)]
    t_rows = [qi * tq + lax.broadcasted_iota(jnp.int32, (1, tq), 1) for qi in sub_qi]

    for sub, qi in enumerate(sub_qi):
        lanes = slice(sub * tq, (sub + 1) * tq)
        key_live = lax.broadcasted_iota(jnp.int32, (n_rows, tq), 0) < (qi + 1) * rpt
        strip_rows = pl.ds(pl.multiple_of(qi * rpt, rpt), 2 * rpt)
        for r in range(HEADS_PER_GROUP):
            st = jnp.dot(kc_ref[0], q_ref[r * HEAD_DIM:(r + 1) * HEAD_DIM, lanes], preferred_element_type=F32)
            s_scr[sub, r, 0:rpt, :] = jnp.zeros((rpt, tq), F32)
            s_scr[sub, r, rpt:rpt + n_rows, :] = jnp.where(key_live, st, NEG_INF)
            s_scr[sub, r, strip_rows, :] = s_scr[sub, r, strip_rows, :] + strip_ref[r]

    for sub in range(CMP_SUB):
        lanes = slice(sub * tq, (sub + 1) * tq)
        any_valid = jnp.where(t_rows[sub] >= L_CMP - 1, 1.0, 0.0)
        pg = jnp.zeros((n_rows, tq), F32)
        for r in range(HEADS_PER_GROUP):
            s = s_scr[sub, r, rpt:rpt + n_rows, :]
            m = jnp.max(s, axis=0, keepdims=True)
            e = jnp.exp2(s - m)
            p = e * (any_valid / jnp.sum(e, axis=0, keepdims=True))
            pg = pg + p
            ot = jnp.dot(vct_ref[0], p.astype(BF16), preferred_element_type=F32) * gate_ref[r:r + 1, lanes]
            o_ref[lanes, r * HEAD_DIM:(r + 1) * HEAD_DIM] = ot.T.astype(o_ref.dtype)
        for c in range(lane_tiles):
            pg_scr[sub, c, 0:pad, :] = jnp.zeros((pad, V7X_LANES), F32)
            pg_scr[sub, c, pad:pad + n_rows, :] = pg[:, c * V7X_LANES:(c + 1) * V7X_LANES]

    for sub in range(CMP_SUB):
        def rows(off, sub=sub):
            return jnp.concatenate(
                [pg_scr[sub, c, pl.ds(pad + off, n_sel, stride=per_sel), :] for c in range(lane_tiles)], axis=1)

        imp = (rows(-1) + rows(3)) + 2.0 * (rows(0) + rows(1) + rows(2))
        j = lax.broadcasted_iota(jnp.int32, (n_sel, tq), 0)
        jf = j.astype(F32)
        jt = jnp.right_shift(t_rows[sub], int(math.log2(L_SEL)))
        forced = (j == 0) | (j == jt) | (j == jt - 1)
        cand = (j >= 1) & (j <= jt - 2)
        score = jnp.where(cand, imp, -1.0)
        sel = jnp.where(forced, 1.0, 0.0)
        for _ in range(N_SEL - 3):
            mx = jnp.max(score, axis=0, keepdims=True)
            first_j = jnp.min(jnp.where(score == mx, jf, float(n_sel)), axis=0, keepdims=True)
            pick = (jf == first_j) & (mx >= 0.0)
            sel = jnp.where(pick, 1.0, sel)
            score = jnp.where(pick, -2.0, score)
        neg = jnp.where(sel > 0.5, 0.0, NEG_INF)
        if n_sel < V7X_LANES:
            neg = jnp.concatenate([neg, jnp.zeros((V7X_LANES - n_sel, tq), F32)], axis=0)
        sel_ref[0, :, sub * tq:(sub + 1) * tq] = neg.astype(sel_ref.dtype)


def _cmp_attn(q2, kc, vct, strip, gates_t, t):
    n_rows = kc.shape[1]
    g = A_KV_GROUPS
    tq = CMP_TQ
    tqs = CMP_SUB * tq
    return pl.pallas_call(
        _cmp_attn_kernel,
        grid=(g, t // tqs),
        in_specs=[pl.BlockSpec((GROUP_Q_WIDTH, tqs), lambda gi, qi: (gi, qi)),
                  pl.BlockSpec((1, n_rows, HEAD_DIM), lambda gi, qi: (gi, 0, 0)),
                  pl.BlockSpec((1, HEAD_DIM, n_rows), lambda gi, qi: (g + gi, 0, 0)),
                  pl.BlockSpec((HEADS_PER_GROUP, 2 * CMP_ROWS_PER_TQ, tq), lambda gi, qi: (gi, 0, 0)),
                  _gate_spec(0, tqs)],
        out_specs=[pl.BlockSpec((tqs, GROUP_Q_WIDTH), lambda gi, qi: (qi, gi)),
                   pl.BlockSpec((1, V7X_LANES, tqs), lambda gi, qi: (gi, 0, qi))],
        out_shape=[jax.ShapeDtypeStruct((t, A_WIDTH), BF16),
                   jax.ShapeDtypeStruct((g, V7X_LANES, t), BF16)],
        scratch_shapes=[pltpu.VMEM((CMP_SUB, HEADS_PER_GROUP, CMP_ROWS_PER_TQ + n_rows, tq), F32),
                        pltpu.VMEM((CMP_SUB, tq // V7X_LANES, n_rows + V7X_SUBLANES, V7X_LANES), F32)],
        compiler_params=_params(("parallel", "arbitrary"),
                                [tqs * GROUP_Q_WIDTH * 2 * 2, n_rows * HEAD_DIM * 2 * 2, tqs * V7X_LANES * 2],
                                CMP_SUB * 12 * n_rows * tq * 4),
        name="cmp_attn",
    )(q2, kc, vct, strip, gates_t)


def _gate_spec(branch, tq):
    return pl.BlockSpec((V7X_SUBLANES, tq), lambda gi, qi: (branch * A_KV_GROUPS + gi, qi))


def _softmax_step(r, z, z_max, vt, m_scr, l_scr, acc_scr):
    m_prev = m_scr[r]
    m_new = jnp.maximum(m_prev, z_max)
    p = jnp.exp2(z - m_new)
    alpha = jnp.exp2(m_prev - m_new)
    l_scr[r] = alpha * l_scr[r] + jnp.sum(p, axis=0, keepdims=True)
    acc_scr[r] = acc_scr[r] * alpha + jnp.dot(vt, p.astype(BF16), preferred_element_type=F32)
    m_scr[r] = m_new


def _pipelined_tile(kvj, kvj_next, logits_head, softmax_head):
    last = HEADS_PER_GROUP - 1
    for r in range(HEADS_PER_GROUP):
        if r < last:
            logits_head(r + 1, kvj)
        elif kvj_next is not None:
            logits_head(0, kvj_next)
        softmax_head(r, kvj)


def _softmax_init(m_scr, l_scr, acc_scr):
    m_scr[...] = jnp.full(m_scr.shape, NEG_INF, F32)
    l_scr[...] = jnp.zeros(l_scr.shape, F32)
    acc_scr[...] = jnp.zeros(acc_scr.shape, F32)


def _softmax_finish(gate_ref, o_ref, l_scr, acc_scr):
    for r in range(HEADS_PER_GROUP):
        ot = acc_scr[r] * (gate_ref[r:r + 1, :] / l_scr[r])
        o_ref[:, r * HEAD_DIM:(r + 1) * HEAD_DIM] = ot.T.astype(o_ref.dtype)


def _sel_attn_kernel(q_ref, k_ref, e_ref, vt_ref, sel_ref, bt_diag_ref, bt_prev_ref, gate_ref, o_ref,
                     qaug_scr, z_scr, zmax_scr, m_scr, l_scr, acc_scr):
    qi = pl.program_id(1)
    tk = ATTN_TILE
    _softmax_init(m_scr, l_scr, acc_scr)
    for r in range(HEADS_PER_GROUP):
        qaug_scr[r, 0:HEAD_DIM, :] = q_ref[r * HEAD_DIM:(r + 1) * HEAD_DIM, :]
        qaug_scr[r, HEAD_DIM:2 * HEAD_DIM, :] = sel_ref[0]

    def logits_head(r, kvj):
        rows = pl.ds(pl.multiple_of(kvj * tk, tk), tk)
        k_aug = jnp.concatenate([k_ref[rows, :], e_ref[rows, :]], axis=1)
        z = jnp.dot(k_aug, qaug_scr[r], preferred_element_type=F32)
        z_scr[r] = z
        zmax_scr[r] = jnp.max(z, axis=0, keepdims=True)

    def softmax_near(bt_ref):
        def softmax_head(r, kvj):
            z = z_scr[r] + bt_ref[r, 0]
            _softmax_step(r, z, jnp.max(z, axis=0, keepdims=True), vt_ref[0, 0, kvj], m_scr, l_scr, acc_scr)
        return softmax_head

    def softmax_far(r, kvj):
        _softmax_step(r, z_scr[r], zmax_scr[r], vt_ref[0, 0, kvj], m_scr, l_scr, acc_scr)

    logits_head(0, qi)
    _pipelined_tile(qi, jnp.maximum(qi - 1, 0), logits_head, softmax_near(bt_diag_ref))

    @pl.when(qi >= 1)
    def _():
        _pipelined_tile(qi - 1, 0, logits_head, softmax_near(bt_prev_ref))

    n_far = jnp.maximum(qi - 1, 0)

    def far_tile(kvj, carry):
        _pipelined_tile(kvj, jnp.minimum(kvj + 1, n_far - 1), logits_head, softmax_far)
        return carry

    lax.fori_loop(0, n_far, far_tile, 0)
    _softmax_finish(gate_ref, o_ref, l_scr, acc_scr)


def _win_attn_kernel(q_ref, k_ref, vt_ref, bt_diag_ref, bt_prev_ref, gate_ref, oc_ref, os_ref, za_ref, o_ref,
                     z_scr):
    qi = pl.program_id(1)
    tk = ATTN_TILE
    hk = tk // 2
    units = [(r, half) for r in range(HEADS_PER_GROUP) for half in range(2)]

    def run(key_row0, n_keys, bias, values_t):
        def logits(u):
            r, half = units[u]
            row0 = key_row0(half)
            row0 = row0 if isinstance(row0, int) else pl.multiple_of(row0, hk)
            k = k_ref[pl.ds(row0, n_keys(half)), :]
            qt = q_ref[r * HEAD_DIM:(r + 1) * HEAD_DIM, half * hk:(half + 1) * hk]
            z_scr[u, 0:n_keys(half), :] = jnp.dot(k, qt, preferred_element_type=F32)

        def softmax(u):
            r, half = units[u]
            rows = slice(half * hk, (half + 1) * hk)
            cols = slice(r * HEAD_DIM, (r + 1) * HEAD_DIM)
            z = z_scr[u, 0:n_keys(half), :] + bias(r, half)
            p = jnp.exp2(z - jnp.max(z, axis=0, keepdims=True))
            scale = gate_ref[r:r + 1, rows] / jnp.sum(p, axis=0, keepdims=True)
            o_win = (jnp.dot(values_t(half), p.astype(BF16), preferred_element_type=F32) * scale).T
            o_a = oc_ref[rows, cols].astype(F32) + os_ref[rows, cols].astype(F32) + o_win
            o_ref[rows, cols] = (o_a * za_ref[rows, cols].astype(F32)).astype(o_ref.dtype)

        logits(0)
        for u in range(len(units)):
            if u + 1 < len(units):
                logits(u + 1)
            softmax(u)

    @pl.when(qi >= 1)
    def _():
        def bias(r, half):
            lanes = slice(half * hk, (half + 1) * hk)
            if half == 0:
                parts = [bt_prev_ref[r, 0, 0:hk, lanes], bt_prev_ref[r, 0, hk:tk, lanes],
                         bt_diag_ref[r, 0, 0:hk, lanes]]
            else:
                parts = [bt_prev_ref[r, 0, hk:tk, lanes], bt_diag_ref[r, 0, 0:hk, lanes],
                         bt_diag_ref[r, 0, hk:tk, lanes]]
            return jnp.concatenate(parts, axis=0)

        def values_t(half):
            prev, diag = vt_ref[0, 0, qi - 1], vt_ref[0, 0, qi]
            if half == 0:
                return jnp.concatenate([prev, diag[:, 0:hk]], axis=1)
            return jnp.concatenate([prev[:, hk:tk], diag], axis=1)

        run(lambda half: (qi - 1) * tk + half * hk, lambda half: 3 * hk, bias, values_t)

    @pl.when(qi == 0)
    def _():
        def bias(r, half):
            return bt_diag_ref[r, 0, 0:(half + 1) * hk, half * hk:(half + 1) * hk]

        run(lambda half: 0, lambda half: (half + 1) * hk, bias, lambda half: vt_ref[0, 0, 0][:, 0:(half + 1) * hk])


def _attn_scratch(tq):
    return [pltpu.VMEM((HEADS_PER_GROUP, ATTN_TILE, tq), F32), pltpu.VMEM((HEADS_PER_GROUP, 1, tq), F32),
            pltpu.VMEM((HEADS_PER_GROUP, 1, tq), F32), pltpu.VMEM((HEADS_PER_GROUP, HEAD_DIM, tq), F32)]


def _attn_common_specs(t, k_set, branch):
    tq = ATTN_TILE
    q_spec = pl.BlockSpec((GROUP_Q_WIDTH, tq), lambda gi, qi: (gi, qi))
    k_spec = pl.BlockSpec((t, HEAD_DIM), lambda gi, qi: (0, k_set * A_KV_GROUPS + gi))
    vt_spec = pl.BlockSpec((1, 1, t // ATTN_TILE, HEAD_DIM, ATTN_TILE), lambda gi, qi: (k_set, gi, 0, 0, 0))
    out_spec = pl.BlockSpec((tq, GROUP_Q_WIDTH), lambda gi, qi: (qi, gi))
    return q_spec, k_spec, vt_spec, _gate_spec(branch, tq), out_spec


def _bt_spec(kind):
    return pl.BlockSpec((HEADS_PER_GROUP, 1, ATTN_TILE, ATTN_TILE), lambda gi, qi: (gi, kind, 0, 0))


def _attn_block_bytes(t):
    tq = ATTN_TILE
    return [tq * GROUP_Q_WIDTH * 2 * 2, t * HEAD_DIM * 2 * 2, 2 * HEADS_PER_GROUP * tq * tq * 4]


def _sel_attn(q2, k_plain, vt, sel_neg, bt, gates_t, t):
    tq = ATTN_TILE
    q_spec, k_spec, vt_spec, gate_spec, out_spec = _attn_common_specs(t, 0, 1)
    sel_spec = pl.BlockSpec((1, V7X_LANES, tq), lambda gi, qi: (gi, 0, qi))
    e_onehot = (jnp.arange(t, dtype=jnp.int32)[:, None] // L_SEL
                == jnp.arange(V7X_LANES, dtype=jnp.int32)[None, :]).astype(BF16)
    e_spec = pl.BlockSpec((t, V7X_LANES), lambda gi, qi: (0, 0))
    return pl.pallas_call(
        _sel_attn_kernel,
        grid=(A_KV_GROUPS, t // tq),
        in_specs=[q_spec, k_spec, e_spec, vt_spec, sel_spec, _bt_spec(0), _bt_spec(1), gate_spec],
        out_specs=out_spec,
        out_shape=jax.ShapeDtypeStruct((t, A_WIDTH), BF16),
        scratch_shapes=([pltpu.VMEM((HEADS_PER_GROUP, 2 * HEAD_DIM, tq), BF16)] + _attn_scratch(tq)[:1]
                        + [pltpu.VMEM((HEADS_PER_GROUP, 1, tq), F32)] + _attn_scratch(tq)[1:]),
        compiler_params=_params(("parallel", "arbitrary"),
                                _attn_block_bytes(t) + [t * V7X_LANES * 2, tq * V7X_LANES * 2],
                                (8 + HEADS_PER_GROUP) * tq * tq * 4),
        name="sel_attn",
    )(q2, k_plain, e_onehot, vt, sel_neg, bt, bt, gates_t)


def _win_attn(q2, k_plain, vt, bt, gates_t, o_cmp, o_sel, za, t):
    tq = ATTN_TILE
    q_spec, k_spec, vt_spec, gate_spec, out_spec = _attn_common_specs(t, 1, 2)
    return pl.pallas_call(
        _win_attn_kernel,
        grid=(A_KV_GROUPS, t // tq),
        in_specs=[q_spec, k_spec, vt_spec, _bt_spec(0), _bt_spec(2), gate_spec, out_spec, out_spec, out_spec],
        out_specs=out_spec,
        out_shape=jax.ShapeDtypeStruct((t, A_WIDTH), BF16),
        scratch_shapes=[pltpu.VMEM((2 * HEADS_PER_GROUP, 3 * tq // 2, tq // 2), F32)],
        compiler_params=_params(("parallel", "arbitrary"),
                                _attn_block_bytes(t) + [3 * tq * GROUP_Q_WIDTH * 2],
                                (8 + HEADS_PER_GROUP) * tq * tq * 4),
        name="win_attn",
    )(q2, k_plain, vt, bt, bt, gates_t, o_cmp, o_sel, za)


def _gmlp_kernel(u_ref, v_ref, z_ref, ng_ref, w_ref, bt_ref, o_ref):
    v = v_ref[...].astype(F32)
    ms = jnp.mean(v * v, axis=-1, keepdims=True)
    vn = (v * lax.rsqrt(ms + NORM_EPS) * ng_ref[...]).astype(BF16)
    p_idx = lax.broadcasted_iota(jnp.int32, (CHUNK, CHUNK), 0)
    q_idx = lax.broadcasted_iota(jnp.int32, (CHUNK, CHUNK), 1)
    causal = q_idx <= p_idx
    gd = B_WIDTH // B_GROUPS
    for gg in range(B_GROUPS):
        cols = slice(gg * gd, (gg + 1) * gd)
        w = jnp.where(causal, w_ref[gg], jnp.zeros((), w_ref.dtype))
        f = jnp.dot(w, vn[:, cols], preferred_element_type=F32) + bt_ref[:, gg:gg + 1]
        o_ref[:, cols] = (u_ref[:, cols].astype(F32) * f * z_ref[:, cols].astype(F32)).astype(o_ref.dtype)


def _gmlp(uv, zb, norm_g, w_s, b_t, t):
    bw = B_WIDTH

    def col_spec(col):
        return pl.BlockSpec((CHUNK, bw), lambda i: (i, col))

    return pl.pallas_call(
        _gmlp_kernel,
        grid=(t // CHUNK,),
        in_specs=[col_spec(0), col_spec(1), col_spec(0),
                  pl.BlockSpec((1, bw), lambda i: (0, 0)),
                  pl.BlockSpec((B_GROUPS, CHUNK, CHUNK), lambda i: (0, 0, 0)),
                  pl.BlockSpec((CHUNK, B_GROUPS), lambda i: (0, 0))],
        out_specs=pl.BlockSpec((CHUNK, bw), lambda i: (i, 0)),
        out_shape=jax.ShapeDtypeStruct((t, bw), BF16),
        compiler_params=_params(("parallel",), [CHUNK * bw * 2] * 4 + [B_GROUPS * CHUNK * CHUNK * 2],
                                4 * CHUNK * bw * 4),
        name="gmlp",
    )(uv, uv, zb, norm_g.reshape(1, bw), w_s, b_t)


def _merge_kernel(oa_ref, ob_ref, wa_ref, wb_ref, ma_ref, mb_ref, o_ref):
    ya = _nn_dot(oa_ref[...], wa_ref[0])
    yb = _nn_dot(ob_ref[...], wb_ref[0])
    o_ref[...] = (ma_ref[...].astype(F32) * ya + mb_ref[...].astype(F32) * yb).astype(o_ref.dtype)


def _merge(o_a, o_b, m_gates, w_a, w_b, layer, t, d):
    tm = min(MM_TM, t)
    tn = min(W32_TN, d)
    aw = A_WIDTH
    bw = B_WIDTH
    return pl.pallas_call(
        _merge_kernel,
        grid=(d // tn, t // tm),
        in_specs=[pl.BlockSpec((tm, aw), lambda j, i: (i, 0)),
                  pl.BlockSpec((tm, bw), lambda j, i: (i, 0)),
                  pl.BlockSpec((1, aw, tn), lambda j, i: (layer, 0, j)),
                  pl.BlockSpec((1, bw, tn), lambda j, i: (layer, 0, j)),
                  pl.BlockSpec((tm, tn), lambda j, i: (i, j)),
                  pl.BlockSpec((tm, tn), lambda j, i: (i, d // tn + j))],
        out_specs=pl.BlockSpec((tm, tn), lambda j, i: (i, j)),
        out_shape=jax.ShapeDtypeStruct((t, d), BF16),
        compiler_params=_params(("arbitrary", "arbitrary"),
                                [tm * aw * 2] * 2 + [aw * tn * 4, bw * tn * 4] + [tm * tn * 2] * 3,
                                4 * tm * tn * 4),
        name="merge",
    )(o_a, o_b, w_a, w_b, m_gates, m_gates)


def _gate_weight_columns():
    src = np.full((V7X_LANES,), -1, np.int64)
    for br in range(3):
        for g in range(A_KV_GROUPS):
            for r in range(HEADS_PER_GROUP):
                src[(br * A_KV_GROUPS + g) * V7X_SUBLANES + r] = (g * HEADS_PER_GROUP + r) * 3 + br
    return src


def kernel(x, rel_bias, pre_norm, w_in, cmp_pe_k, cmp_w1_k, cmp_w2_k, cmp_pe_v, cmp_w1_v, cmp_w2_v,
           w_out_a, sgu_norm, sgu_w, sgu_b, w_out_b, w_out, post_norm):
    batch, t, d = x.shape
    assert batch == 1 and t % MM_TM == 0 and N_SEL <= t // L_SEL <= V7X_LANES
    depth = w_in.shape[0]
    g = A_KV_GROUPS
    xs = x.reshape(t, d)

    w_in_t = jnp.swapaxes(w_in, 1, 2)
    gate_src = _gate_weight_columns()
    w_gate = jnp.take(w_in_t[:, SRC_GATES:SRC_REST, :], jnp.asarray(np.maximum(gate_src, 0)), axis=1)
    w_gate = jnp.where(jnp.asarray(gate_src >= 0)[None, :, None], w_gate, 0.0).astype(BF16)
    w1 = jnp.stack([cmp_w1_k, cmp_w1_v], axis=1).astype(BF16)
    w2 = jnp.stack([cmp_w2_k, cmp_w2_v], axis=1).astype(BF16)
    slab = STRIDE_CMP * HEAD_DIM
    pe = jnp.stack([cmp_pe_k, cmp_pe_v], axis=1).reshape(depth, 2, 2, slab)
    w_s = sgu_w.astype(BF16)

    bias_tiles = _bias_tiles(rel_bias)
    bias_strip = _bias_strip(rel_bias)

    h = _rmsnorm(xs, pre_norm[0], BF16)
    for layer in range(depth):
        q2 = _q_proj(h, w_in_t, layer)
        x_slabs = _cmp_proj(h, w_in_t, layer)
        k_plain, vt = _kv_proj(h, w_in_t, layer, SRC_KV + 2 * A_KV_WIDTH, 2)
        gates_t = _gate_proj(h, w_gate, layer)
        za = _mm_w32(h, w_in_t, layer, SRC_REST + COL_ZA, A_WIDTH, BF16, "silu", "za_proj")
        uv = _mm_w32(h, w_in_t, layer, SRC_REST + COL_U, 2 * B_WIDTH, BF16, "gelu", "uv_proj")
        zb = _mm_w32(h, w_in_t, layer, SRC_REST + COL_ZB, B_WIDTH, BF16, "silu", "zb_proj")
        m_gates = _mm_w32(h, w_in_t, layer, SRC_REST + COL_MA, 2 * d, BF16, "sigmoid", "m_proj")

        kc, kct = _compress(x_slabs, pe[layer], w1[layer], w2[layer])
        o_cmp, sel_neg = _cmp_attn(q2, kc, kct, bias_strip, gates_t, t)
        o_sel = _sel_attn(q2, k_plain, vt, sel_neg, bias_tiles, gates_t, t)
        o_a = _win_attn(q2, k_plain, vt, bias_tiles, gates_t, o_cmp, o_sel, za, t)

        o_b = _gmlp(uv, zb, sgu_norm[layer], w_s[layer], sgu_b[layer].T, t)
        merged = _merge(o_a, o_b, m_gates, w_out_a, w_out_b, layer, t, d)
        y = _mm(merged, w_out, layer, BF16, name="out_proj")
        g_next = pre_norm[layer + 1] if layer + 1 < depth else pre_norm[layer]
        xs, h = _post_norm_residual(xs, y, post_norm[layer], g_next)
    del g
    return xs.reshape(batch, t, d)
```

```python
import functools
import math

import numpy as np
import jax
import jax.numpy as jnp
from jax import lax
from jax.experimental import pallas as pl
from jax.experimental.pallas import tpu as pltpu

F32 = jnp.float32
BF16 = jnp.bfloat16

A_HEADS = 16
A_KV_GROUPS = 4
HEADS_PER_GROUP = A_HEADS // A_KV_GROUPS
HEAD_DIM = 128
A_WIDTH = A_HEADS * HEAD_DIM
A_KV_WIDTH = A_KV_GROUPS * HEAD_DIM
GROUP_Q_WIDTH = HEADS_PER_GROUP * HEAD_DIM
L_CMP = 32
STRIDE_CMP = 16
L_SEL = 64
N_SEL = 16
WINDOW = 512
B_GROUPS = 16
CHUNK = 128
B_WIDTH = 2048
NUM_BUCKETS = 32
MAX_DISTANCE = 128
NORM_EPS = 1e-6
NEG_INF = -1e30
SCALE = HEAD_DIM ** -0.5
LOG2_E = math.log2(math.e)
SCALE_LOG2 = SCALE * LOG2_E
N_GATE_COLS = 3 * A_HEADS

V7X_LANES = 128
V7X_SUBLANES = 8
V7X_SCOPED_VMEM_CAP_BYTES = 60000 * 1024

ATTN_TILE = 512
CMP_TQ = 256
CMP_ROWS_PER_TQ = CMP_TQ // STRIDE_CMP
CMP_SUB = 2
MM_TM = 1024
MM_TN = 1024
W32_TN = 512
W32_TM = 2048

SRC_KV = A_WIDTH
SRC_GATES = SRC_KV + 6 * A_KV_WIDTH
SRC_REST = SRC_GATES + N_GATE_COLS
COL_ZA = 0
COL_U = COL_ZA + A_WIDTH
COL_V = COL_U + B_WIDTH
COL_ZB = COL_V + B_WIDTH
COL_MA = COL_ZB + B_WIDTH


def _vmem_limit(block_bytes, temp_bytes=0):
    need = 2 * sum(block_bytes) + temp_bytes + (4 << 20)
    return int(min(max(need, 16 << 20), V7X_SCOPED_VMEM_CAP_BYTES))


def _params(sem, block_bytes, temp_bytes=0, flags=None):
    return pltpu.CompilerParams(dimension_semantics=sem,
                                vmem_limit_bytes=_vmem_limit(block_bytes, temp_bytes), flags=flags)


def _t5_thresholds():
    n = np.arange(0, 4 * MAX_DISTANCE)
    max_exact = NUM_BUCKETS // 2
    nf = np.maximum(n, 1).astype(np.float32)
    large = max_exact + (np.log(nf / np.float32(max_exact)) / np.float32(math.log(MAX_DISTANCE / max_exact))
                         * np.float32(NUM_BUCKETS - max_exact)).astype(np.int32)
    bucket = np.where(n < max_exact, n, np.minimum(large, NUM_BUCKETS - 1))
    assert np.all(np.diff(bucket) >= 0) and bucket[-1] == NUM_BUCKETS - 1
    return [int(np.argmax(bucket >= b)) for b in range(1, NUM_BUCKETS)]


_T5_THR = _t5_thresholds()


def _sigmoid(x):
    return 1.0 / (1.0 + jnp.exp(-x))


def _silu(x):
    return x * _sigmoid(x)


def _gelu(x):
    return jax.nn.gelu(x, approximate=True)


_ACTS = {"none": lambda v: v, "silu": _silu, "gelu": _gelu, "sigmoid": _sigmoid}


def _rmsnorm_kernel(x_ref, g_ref, o_ref):
    x = x_ref[...]
    ms = jnp.mean(x * x, axis=-1, keepdims=True)
    o_ref[...] = (x * lax.rsqrt(ms + NORM_EPS) * g_ref[...]).astype(o_ref.dtype)


def _rmsnorm(x, g, out_dtype):
    t, d = x.shape
    tm = min(256, t)
    return pl.pallas_call(
        _rmsnorm_kernel,
        grid=(t // tm,),
        in_specs=[pl.BlockSpec((tm, d), lambda i: (i, 0)), pl.BlockSpec((1, d), lambda i: (0, 0))],
        out_specs=pl.BlockSpec((tm, d), lambda i: (i, 0)),
        out_shape=jax.ShapeDtypeStruct((t, d), out_dtype),
        compiler_params=_params(("parallel",), [tm * d * 4, tm * d * 4]),
        name="rmsnorm",
    )(x, g.reshape(1, d))


def _post_kernel(x_ref, y_ref, g_ref, gn_ref, o_ref, h_ref):
    y = y_ref[...].astype(F32)
    ms = jnp.mean(y * y, axis=-1, keepdims=True)
    x = x_ref[...] + y * lax.rsqrt(ms + NORM_EPS) * g_ref[...]
    o_ref[...] = x
    ms_x = jnp.mean(x * x, axis=-1, keepdims=True)
    h_ref[...] = (x * lax.rsqrt(ms_x + NORM_EPS) * gn_ref[...]).astype(h_ref.dtype)


def _post_norm_residual(x, y, g, g_next):
    t, d = x.shape
    tm = min(256, t)
    row = pl.BlockSpec((tm, d), lambda i: (i, 0))
    vec = pl.BlockSpec((1, d), lambda i: (0, 0))
    return pl.pallas_call(
        _post_kernel,
        grid=(t // tm,),
        in_specs=[row, row, vec, vec],
        out_specs=[row, row],
        out_shape=[jax.ShapeDtypeStruct((t, d), F32), jax.ShapeDtypeStruct((t, d), BF16)],
        compiler_params=_params(("parallel",), [tm * d * 4] * 4),
        name="post_norm_residual",
    )(x, y, g.reshape(1, d), g_next.reshape(1, d))


def _nt_dot(a, b):
    return lax.dot_general(a, b, (((1,), (1,)), ((), ())), preferred_element_type=F32)


def _nn_dot(a, b):
    return lax.dot_general(a, b, (((1,), (0,)), ((), ())), preferred_element_type=F32)


def _mm_kernel(a_ref, b_ref, o_ref, *, act, nt, n_chunks):
    a = a_ref[...]
    cw = o_ref.shape[1] // n_chunks
    for c in range(n_chunks):
        cols = slice(c * cw, (c + 1) * cw)
        if nt:
            acc = _nt_dot(a, b_ref[0, cols, :])
        else:
            acc = _nn_dot(a, b_ref[0, :, cols])
        o_ref[:, cols] = _ACTS[act](acc).astype(o_ref.dtype)


def _mm(a, b, layer, out_dtype, name):
    m, k = a.shape
    n = b.shape[2]
    tm = min(W32_TM, m)
    tn = min(W32_TN, n)
    osz = jnp.dtype(out_dtype).itemsize
    return pl.pallas_call(
        functools.partial(_mm_kernel, act="none", nt=False, n_chunks=1),
        grid=(n // tn, m // tm),
        in_specs=[pl.BlockSpec((tm, k), lambda j, i: (i, 0)),
                  pl.BlockSpec((1, k, tn), lambda j, i: (layer, 0, j))],
        out_specs=pl.BlockSpec((tm, tn), lambda j, i: (i, j)),
        out_shape=jax.ShapeDtypeStruct((m, n), out_dtype),
        compiler_params=_params(("arbitrary", "arbitrary"), [tm * k * 2, k * tn * 4, tm * tn * osz],
                                2 * tm * tn * 4),
        name=name,
    )(a, b)


def _w_rows_spec(layer, row0, rows, k):
    assert row0 % V7X_SUBLANES == 0
    return pl.BlockSpec((pl.Element(1), pl.Element(rows), pl.Element(k)),
                        lambda j, i: (layer, pl.multiple_of(row0 + j * rows, V7X_SUBLANES), 0))


def _mm_w32(a, w_t, layer, row0, n, out_dtype, act, name):
    m, k = a.shape
    tm = min(W32_TM, m)
    tn = next(c for c in (W32_TN, 256, 128) if n % c == 0)
    n_chunks = 1 if act == "none" else max(1, tn // 256)
    osz = jnp.dtype(out_dtype).itemsize
    return pl.pallas_call(
        functools.partial(_mm_kernel, act=act, nt=True, n_chunks=n_chunks),
        grid=(n // tn, m // tm),
        in_specs=[pl.BlockSpec((tm, k), lambda j, i: (i, 0)), _w_rows_spec(layer, row0, tn, k)],
        out_specs=pl.BlockSpec((tm, tn), lambda j, i: (i, j)),
        out_shape=jax.ShapeDtypeStruct((m, n), out_dtype),
        compiler_params=_params(("arbitrary", "arbitrary"), [tm * k * 2, k * tn * 4, tm * tn * osz],
                                2 * tm * tn * 4),
        name=name,
    )(a, w_t)


def _q_proj_kernel(a_ref, b_ref, o_ref):
    o_ref[...] = (_nt_dot(b_ref[0], a_ref[...]) * SCALE_LOG2).astype(o_ref.dtype)


def _q_proj(h, w_t, layer):
    t, k = h.shape
    n = A_WIDTH
    tm = min(W32_TM, t)
    tn = W32_TN
    return pl.pallas_call(
        _q_proj_kernel,
        grid=(n // tn, t // tm),
        in_specs=[pl.BlockSpec((tm, k), lambda j, i: (i, 0)), _w_rows_spec(layer, 0, tn, k)],
        out_specs=pl.BlockSpec((tn, tm), lambda j, i: (j, i)),
        out_shape=jax.ShapeDtypeStruct((n, t), BF16),
        compiler_params=_params(("arbitrary", "arbitrary"), [tm * k * 2, k * tn * 4, tm * tn * 2],
                                2 * tm * tn * 4),
        name="q_proj",
    )(h, w_t)


def _gate_proj_kernel(a_ref, b_ref, o_ref):
    o_ref[...] = _sigmoid(_nt_dot(b_ref[0], a_ref[...]))


def _gate_proj(h, w_gate, layer):
    t, k = h.shape
    tm = min(MM_TM, t)
    return pl.pallas_call(
        _gate_proj_kernel,
        grid=(t // tm,),
        in_specs=[pl.BlockSpec((tm, k), lambda i: (i, 0)),
                  pl.BlockSpec((1, V7X_LANES, k), lambda i: (layer, 0, 0))],
        out_specs=pl.BlockSpec((V7X_LANES, tm), lambda i: (0, i)),
        out_shape=jax.ShapeDtypeStruct((V7X_LANES, t), F32),
        compiler_params=_params(("parallel",), [tm * k * 2, k * V7X_LANES * 2, tm * V7X_LANES * 4]),
        name="gate_proj",
    )(h, w_gate)


def _cmp_proj_kernel(a_ref, b_ref, o_ref, scr):
    acc = _nt_dot(a_ref[...], b_ref[0])
    n_sg = scr.shape[0]
    slab_rows = o_ref.shape[1]
    for sg in range(n_sg):
        scr[sg] = acc[:, sg * HEAD_DIM:(sg + 1) * HEAD_DIM]
    for sg in range(n_sg):
        for l in range(STRIDE_CMP):
            o_ref[sg, :, l * HEAD_DIM:(l + 1) * HEAD_DIM] = (
                scr[sg, pl.ds(l, slab_rows, stride=STRIDE_CMP), :].astype(o_ref.dtype))


def _cmp_proj(h, w_t, layer):
    t, k = h.shape
    tm = min(MM_TM, t)
    g = A_KV_GROUPS
    slab = STRIDE_CMP * HEAD_DIM
    return pl.pallas_call(
        _cmp_proj_kernel,
        grid=(2, t // tm),
        in_specs=[pl.BlockSpec((tm, k), lambda j, i: (i, 0)), _w_rows_spec(layer, SRC_KV, A_KV_WIDTH, k)],
        out_specs=pl.BlockSpec((g, tm // STRIDE_CMP, slab), lambda j, i: (j, i, 0)),
        out_shape=jax.ShapeDtypeStruct((2 * g, t // STRIDE_CMP, slab), BF16),
        scratch_shapes=[pltpu.VMEM((g, tm, HEAD_DIM), F32)],
        compiler_params=_params(("arbitrary", "arbitrary"),
                                [tm * k * 2, k * A_KV_WIDTH * 4, tm * A_KV_WIDTH * 2], 3 * tm * A_KV_WIDTH * 4),
        name="cmp_proj",
    )(h, w_t)


def _kv_proj_kernel(a_ref, b_ref, k_ref, vt_ref):
    a = a_ref[...]
    k_ref[...] = _nt_dot(a, b_ref[0, :A_KV_WIDTH, :]).astype(k_ref.dtype)
    vt = _nt_dot(b_ref[0, A_KV_WIDTH:, :], a).astype(vt_ref.dtype)
    for g in range(A_KV_GROUPS):
        for s in range(vt_ref.shape[2]):
            vt_ref[0, g, s] = vt[g * HEAD_DIM:(g + 1) * HEAD_DIM, s * ATTN_TILE:(s + 1) * ATTN_TILE]


def _kv_proj(h, w_t, layer, row0, n_sets):
    t, k = h.shape
    tm = min(MM_TM // 2, t)
    tiles = tm // ATTN_TILE
    return pl.pallas_call(
        _kv_proj_kernel,
        grid=(n_sets, t // tm),
        in_specs=[pl.BlockSpec((tm, k), lambda j, i: (i, 0)), _w_rows_spec(layer, row0, 2 * A_KV_WIDTH, k)],
        out_specs=[pl.BlockSpec((tm, A_KV_WIDTH), lambda j, i: (i, j)),
                   pl.BlockSpec((1, A_KV_GROUPS, tiles, HEAD_DIM, ATTN_TILE), lambda j, i: (j, 0, i, 0, 0))],
        out_shape=[jax.ShapeDtypeStruct((t, n_sets * A_KV_WIDTH), BF16),
                   jax.ShapeDtypeStruct((n_sets, A_KV_GROUPS, t // ATTN_TILE, HEAD_DIM, ATTN_TILE), BF16)],
        compiler_params=_params(("arbitrary", "arbitrary"),
                                [tm * k * 2, k * 2 * A_KV_WIDTH * 4, tm * 2 * A_KV_WIDTH * 2],
                                3 * tm * 2 * A_KV_WIDTH * 4),
        name="kv_proj",
    )(h, w_t)


def _t5_bias_rel(dist, rb_ref, h):
    far = rb_ref[NUM_BUCKETS - 1, h]
    val = jnp.zeros(dist.shape, F32)
    for b in range(NUM_BUCKETS - 2, -1, -1):
        val = jnp.where(dist < _T5_THR[b], (rb_ref[b, h] - far) * LOG2_E, val)
    return val


def _bias_tile_kernel(rb_ref, o_ref):
    h = pl.program_id(0)
    sb = V7X_LANES
    key = lax.broadcasted_iota(jnp.int32, (sb, sb), 0)
    qry = lax.broadcasted_iota(jnp.int32, (sb, sb), 1)
    far_dist = _T5_THR[-1]
    for kind, (delta, limit) in enumerate(((0, None), (ATTN_TILE, None), (ATTN_TILE, WINDOW))):
        for bi in range(ATTN_TILE // sb):
            for bj in range(ATTN_TILE // sb):
                base = delta + sb * (bj - bi)
                lo, hi = base - (sb - 1), base + (sb - 1)
                if hi < 0 or (limit is not None and lo >= limit):
                    val = jnp.full((sb, sb), NEG_INF, F32)
                else:
                    dist = base + qry - key
                    val = _t5_bias_rel(dist, rb_ref, h) if lo < far_dist else jnp.zeros((sb, sb), F32)
                    if lo < 0:
                        val = jnp.where(dist >= 0, val, NEG_INF)
                    if limit is not None and hi >= limit:
                        val = jnp.where(dist < limit, val, NEG_INF)
                o_ref[0, kind, bi * sb:(bi + 1) * sb, bj * sb:(bj + 1) * sb] = val


def _bias_tiles(rel_bias):
    return pl.pallas_call(
        _bias_tile_kernel,
        grid=(A_HEADS,),
        in_specs=[pl.BlockSpec(memory_space=pltpu.SMEM)],
        out_specs=pl.BlockSpec((1, 3, ATTN_TILE, ATTN_TILE), lambda h: (h, 0, 0, 0)),
        out_shape=jax.ShapeDtypeStruct((A_HEADS, 3, ATTN_TILE, ATTN_TILE), F32),
        compiler_params=_params(("parallel",), [3 * ATTN_TILE * ATTN_TILE * 4]),
        name="bias_tiles",
    )(rel_bias)


def _bias_strip_kernel(rb_ref, o_ref):
    h = pl.program_id(0)
    shape = (2 * CMP_ROWS_PER_TQ, CMP_TQ)
    c = lax.broadcasted_iota(jnp.int32, shape, 0)
    a = lax.broadcasted_iota(jnp.int32, shape, 1)
    dist = a - STRIDE_CMP * c + (CMP_TQ - (L_CMP - 1))
    o_ref[0] = jnp.where(dist >= 0, _t5_bias_rel(dist, rb_ref, h), NEG_INF)


def _bias_strip(rel_bias):
    rows = 2 * CMP_ROWS_PER_TQ
    return pl.pallas_call(
        _bias_strip_kernel,
        grid=(A_HEADS,),
        in_specs=[pl.BlockSpec(memory_space=pltpu.SMEM)],
        out_specs=pl.BlockSpec((1, rows, CMP_TQ), lambda h: (h, 0, 0)),
        out_shape=jax.ShapeDtypeStruct((A_HEADS, rows, CMP_TQ), F32),
        compiler_params=_params(("parallel",), [rows * CMP_TQ * 4]),
        name="bias_strip",
    )(rel_bias)


def _compress_kernel(x_ref, pe_ref, w1_ref, w2_ref, o_ref, ot_ref):
    x = x_ref[0].astype(F32)
    half = x.shape[1]
    lo = (x + pe_ref[0, 0:1, :]).astype(BF16)
    hi = (x + pe_ref[0, 1:2, :]).astype(BF16)
    a = jnp.dot(lo, w1_ref[0, :half, :], preferred_element_type=F32)
    b = jnp.dot(hi, w1_ref[0, half:, :], preferred_element_type=F32)
    n_rows = x.shape[0]
    hidden = a + pltpu.roll(b, n_rows - 1, 0)
    out = jnp.dot(_silu(hidden).astype(BF16), w2_ref[0], preferred_element_type=F32)
    o_ref[0] = out.astype(o_ref.dtype)
    ot_ref[0] = out.T.astype(ot_ref.dtype)


def _compress(x_slabs, pe, w1, w2):
    ng, n_rows, half = x_slabs.shape
    g = A_KV_GROUPS
    return pl.pallas_call(
        _compress_kernel,
        grid=(2, g),
        in_specs=[pl.BlockSpec((1, n_rows, half), lambda s, i: (s * g + i, 0, 0)),
                  pl.BlockSpec((1, 2, half), lambda s, i: (s, 0, 0)),
                  pl.BlockSpec((1, 2 * half, HEAD_DIM), lambda s, i: (s, 0, 0)),
                  pl.BlockSpec((1, HEAD_DIM, HEAD_DIM), lambda s, i: (s, 0, 0))],
        out_specs=[pl.BlockSpec((1, n_rows, HEAD_DIM), lambda s, i: (s * g + i, 0, 0)),
                   pl.BlockSpec((1, HEAD_DIM, n_rows), lambda s, i: (s * g + i, 0, 0))],
        out_shape=[jax.ShapeDtypeStruct((ng, n_rows, HEAD_DIM), BF16),
                   jax.ShapeDtypeStruct((ng, HEAD_DIM, n_rows), BF16)],
        compiler_params=_params(("parallel", "parallel"),
                                [n_rows * half * 2, 2 * half * 4, 2 * half * HEAD_DIM * 2],
                                4 * n_rows * half * 4),
        name="compress",
    )(x_slabs, pe, w1, w2)


def _cmp_attn_kernel(q_ref, kc_ref, vct_ref, strip_ref, gate_ref, o_ref, sel_ref, s_scr, pg_scr):
    n_rows = kc_ref.shape[1]
    n_sel = n_rows // (L_SEL // STRIDE_CMP)
    tq = CMP_TQ
    rpt = CMP_ROWS_PER_TQ
    pad = V7X_SUBLANES
    lane_tiles = tq // V7X_LANES
    per_sel = L_SEL // STRIDE_CMP
    sub_qi = [pl.program_id(1) * CMP_SUB + sub for sub in range(CMP_SUB)]
    t_rows = [qi * tq + lax.broadcasted_iota(jnp.int32, (1, tq), 1) for qi in sub_qi]

    for sub, qi in enumerate(sub_qi):
        lanes = slice(sub * tq, (sub + 1) * tq)
        key_live = lax.broadcasted_iota(jnp.int32, (n_rows, tq), 0) < (qi + 1) * rpt
        strip_rows = pl.ds(pl.multiple_of(qi * rpt, rpt), 2 * rpt)
        for r in range(HEADS_PER_GROUP):
            st = jnp.dot(kc_ref[0], q_ref[r * HEAD_DIM:(r + 1) * HEAD_DIM, lanes], preferred_element_type=F32)
            s_scr[sub, r, 0:rpt, :] = jnp.zeros((rpt, tq), F32)
            s_scr[sub, r, rpt:rpt + n_rows, :] = jnp.where(key_live, st, NEG_INF)
            s_scr[sub, r, strip_rows, :] = s_scr[sub, r, strip_rows, :] + strip_ref[r]

    for sub in range(CMP_SUB):
        lanes = slice(sub * tq, (sub + 1) * tq)
        any_valid = jnp.where(t_rows[sub] >= L_CMP - 1, 1.0, 0.0)
        pg = jnp.zeros((n_rows, tq), F32)
        for r in range(HEADS_PER_GROUP):
            s = s_scr[sub, r, rpt:rpt + n_rows, :]
            m = jnp.max(s, axis=0, keepdims=True)
            e = jnp.exp2(s - m)
            p = e * (any_valid / jnp.sum(e, axis=0, keepdims=True))
            pg = pg + p
            ot = jnp.dot(vct_ref[0], p.astype(BF16), preferred_element_type=F32) * gate_ref[r:r + 1, lanes]
            o_ref[lanes, r * HEAD_DIM:(r + 1) * HEAD_DIM] = ot.T.astype(o_ref.dtype)
        for c in range(lane_tiles):
            pg_scr[sub, c, 0:pad, :] = jnp.zeros((pad, V7X_LANES), F32)
            pg_scr[sub, c, pad:pad + n_rows, :] = pg[:, c * V7X_LANES:(c + 1) * V7X_LANES]

    for sub in range(CMP_SUB):
        def rows(off, sub=sub):
            return jnp.concatenate(
                [pg_scr[sub, c, pl.ds(pad + off, n_sel, stride=per_sel), :] for c in range(lane_tiles)], axis=1)

        imp = (rows(-1) + rows(3)) + 2.0 * (rows(0) + rows(1) + rows(2))
        j = lax.broadcasted_iota(jnp.int32, (n_sel, tq), 0)
        jf = j.astype(F32)
        jt = jnp.right_shift(t_rows[sub], int(math.log2(L_SEL)))
        forced = (j == 0) | (j == jt) | (j == jt - 1)
        cand = (j >= 1) & (j <= jt - 2)
        score = jnp.where(cand, imp, -1.0)
        sel = jnp.where(forced, 1.0, 0.0)
        for _ in range(N_SEL - 3):
            mx = jnp.max(score, axis=0, keepdims=True)
            first_j = jnp.min(jnp.where(score == mx, jf, float(n_sel)), axis=0, keepdims=True)
            pick = (jf == first_j) & (mx >= 0.0)
            sel = jnp.where(pick, 1.0, sel)
            score = jnp.where(pick, -2.0, score)
        neg = jnp.where(sel > 0.5, 0.0, NEG_INF)
        if n_sel < V7X_LANES:
            neg = jnp.concatenate([neg, jnp.zeros((V7X_LANES - n_sel, tq), F32)], axis=0)
        sel_ref[0, :, sub * tq:(sub + 1) * tq] = neg.astype(sel_ref.dtype)


def _cmp_attn(q2, kc, vct, strip, gates_t, t):
    n_rows = kc.shape[1]
    g = A_KV_GROUPS
    tq = CMP_TQ
    tqs = CMP_SUB * tq
    return pl.pallas_call(
        _cmp_attn_kernel,
        grid=(g, t // tqs),
        in_specs=[pl.BlockSpec((GROUP_Q_WIDTH, tqs), lambda gi, qi: (gi, qi)),
                  pl.BlockSpec((1, n_rows, HEAD_DIM), lambda gi, qi: (gi, 0, 0)),
                  pl.BlockSpec((1, HEAD_DIM, n_rows), lambda gi, qi: (g + gi, 0, 0)),
                  pl.BlockSpec((HEADS_PER_GROUP, 2 * CMP_ROWS_PER_TQ, tq), lambda gi, qi: (gi, 0, 0)),
                  _gate_spec(0, tqs)],
        out_specs=[pl.BlockSpec((tqs, GROUP_Q_WIDTH), lambda gi, qi: (qi, gi)),
                   pl.BlockSpec((1, V7X_LANES, tqs), lambda gi, qi: (gi, 0, qi))],
        out_shape=[jax.ShapeDtypeStruct((t, A_WIDTH), BF16),
                   jax.ShapeDtypeStruct((g, V7X_LANES, t), BF16)],
        scratch_shapes=[pltpu.VMEM((CMP_SUB, HEADS_PER_GROUP, CMP_ROWS_PER_TQ + n_rows, tq), F32),
                        pltpu.VMEM((CMP_SUB, tq // V7X_LANES, n_rows + V7X_SUBLANES, V7X_LANES), F32)],
        compiler_params=_params(("parallel", "arbitrary"),
                                [tqs * GROUP_Q_WIDTH * 2 * 2, n_rows * HEAD_DIM * 2 * 2, tqs * V7X_LANES * 2],
                                CMP_SUB * 12 * n_rows * tq * 4),
        name="cmp_attn",
    )(q2, kc, vct, strip, gates_t)


def _gate_spec(branch, tq):
    return pl.BlockSpec((V7X_SUBLANES, tq), lambda gi, qi: (branch * A_KV_GROUPS + gi, qi))


def _softmax_step(r, z, z_max, vt, m_scr, l_scr, acc_scr):
    m_prev = m_scr[r]
    m_new = jnp.maximum(m_prev, z_max)
    p = jnp.exp2(z - m_new)
    alpha = jnp.exp2(m_prev - m_new)
    l_scr[r] = alpha * l_scr[r] + jnp.sum(p, axis=0, keepdims=True)
    acc_scr[r] = acc_scr[r] * alpha + jnp.dot(vt, p.astype(BF16), preferred_element_type=F32)
    m_scr[r] = m_new


def _pipelined_tile(kvj, kvj_next, logits_head, logits_next, softmax_head):
    last = HEADS_PER_GROUP - 1
    for r in range(HEADS_PER_GROUP):
        if r < last:
            logits_head(r + 1, kvj)
        else:
            logits_next(0, kvj_next)
        softmax_head(r, kvj)


def _softmax_init(m_scr, l_scr, acc_scr):
    m_scr[...] = jnp.full(m_scr.shape, NEG_INF, F32)
    l_scr[...] = jnp.zeros(l_scr.shape, F32)
    acc_scr[...] = jnp.zeros(acc_scr.shape, F32)


def _softmax_finish(gate_ref, o_ref, l_scr, acc_scr):
    for r in range(HEADS_PER_GROUP):
        ot = acc_scr[r] * (gate_ref[r:r + 1, :] / l_scr[r])
        o_ref[:, r * HEAD_DIM:(r + 1) * HEAD_DIM] = ot.T.astype(o_ref.dtype)


def _sel_attn_kernel(q_ref, k_ref, e_ref, vt_ref, sel_ref, bt_diag_ref, bt_prev_ref, gate_ref, o_ref,
                     qaug_scr, z_scr, zmax_scr, m_scr, l_scr, acc_scr):
    qi = pl.program_id(1)
    tk = ATTN_TILE
    _softmax_init(m_scr, l_scr, acc_scr)
    for r in range(HEADS_PER_GROUP):
        qaug_scr[r, 0:HEAD_DIM, :] = q_ref[r * HEAD_DIM:(r + 1) * HEAD_DIM, :]
        qaug_scr[r, HEAD_DIM:2 * HEAD_DIM, :] = sel_ref[0]

    def logits_with(bt_ref):
        def logits_head(r, kvj):
            rows = pl.ds(pl.multiple_of(kvj * tk, tk), tk)
            k_aug = jnp.concatenate([k_ref[rows, :], e_ref[rows, :]], axis=1)
            z = jnp.dot(k_aug, qaug_scr[r], preferred_element_type=F32)
            if bt_ref is not None:
                z = z + bt_ref[r, 0]
            z_scr[r] = z
            zmax_scr[r] = jnp.max(z, axis=0, keepdims=True)
        return logits_head

    logits_diag, logits_prev, logits_far = logits_with(bt_diag_ref), logits_with(bt_prev_ref), logits_with(None)

    def softmax_head(r, kvj):
        _softmax_step(r, z_scr[r], zmax_scr[r], vt_ref[0, 0, kvj], m_scr, l_scr, acc_scr)

    logits_diag(0, qi)
    _pipelined_tile(qi, jnp.maximum(qi - 1, 0), logits_diag, logits_prev, softmax_head)

    @pl.when(qi >= 1)
    def _():
        _pipelined_tile(qi - 1, 0, logits_prev, logits_far, softmax_head)

    n_far = jnp.maximum(qi - 1, 0)

    def far_tile(kvj, carry):
        _pipelined_tile(kvj, jnp.minimum(kvj + 1, n_far - 1), logits_far, logits_far, softmax_head)
        return carry

    lax.fori_loop(0, n_far, far_tile, 0)
    _softmax_finish(gate_ref, o_ref, l_scr, acc_scr)


def _win_attn_kernel(q_ref, k_ref, vt_ref, bt_diag_ref, bt_prev_ref, gate_ref, oc_ref, os_ref, za_ref, o_ref,
                     z_scr):
    qi = pl.program_id(1)
    tk = ATTN_TILE
    hk = tk // 2
    units = [(r, half) for r in range(HEADS_PER_GROUP) for half in range(2)]

    def run(key_row0, n_keys, bias, values_t):
        def logits(u):
            r, half = units[u]
            row0 = key_row0(half)
            row0 = row0 if isinstance(row0, int) else pl.multiple_of(row0, hk)
            k = k_ref[pl.ds(row0, n_keys(half)), :]
            qt = q_ref[r * HEAD_DIM:(r + 1) * HEAD_DIM, half * hk:(half + 1) * hk]
            z_scr[u, 0:n_keys(half), :] = jnp.dot(k, qt, preferred_element_type=F32)

        def softmax(u):
            r, half = units[u]
            rows = slice(half * hk, (half + 1) * hk)
            cols = slice(r * HEAD_DIM, (r + 1) * HEAD_DIM)
            z = z_scr[u, 0:n_keys(half), :] + bias(r, half)
            p = jnp.exp2(z - jnp.max(z, axis=0, keepdims=True))
            scale = gate_ref[r:r + 1, rows] / jnp.sum(p, axis=0, keepdims=True)
            o_win = (jnp.dot(values_t(half), p.astype(BF16), preferred_element_type=F32) * scale).T
            o_a = oc_ref[rows, cols].astype(F32) + os_ref[rows, cols].astype(F32) + o_win
            o_ref[rows, cols] = (o_a * za_ref[rows, cols].astype(F32)).astype(o_ref.dtype)

        logits(0)
        for u in range(len(units)):
            if u + 1 < len(units):
                logits(u + 1)
            softmax(u)

    @pl.when(qi >= 1)
    def _():
        def bias(r, half):
            lanes = slice(half * hk, (half + 1) * hk)
            if half == 0:
                parts = [bt_prev_ref[r, 0, 0:hk, lanes], bt_prev_ref[r, 0, hk:tk, lanes],
                         bt_diag_ref[r, 0, 0:hk, lanes]]
            else:
                parts = [bt_prev_ref[r, 0, hk:tk, lanes], bt_diag_ref[r, 0, 0:hk, lanes],
                         bt_diag_ref[r, 0, hk:tk, lanes]]
            return jnp.concatenate(parts, axis=0)

        def values_t(half):
            prev, diag = vt_ref[0, 0, qi - 1], vt_ref[0, 0, qi]
            if half == 0:
                return jnp.concatenate([prev, diag[:, 0:hk]], axis=1)
            return jnp.concatenate([prev[:, hk:tk], diag], axis=1)

        run(lambda half: (qi - 1) * tk + half * hk, lambda half: 3 * hk, bias, values_t)

    @pl.when(qi == 0)
    def _():
        def bias(r, half):
            return bt_diag_ref[r, 0, 0:(half + 1) * hk, half * hk:(half + 1) * hk]

        run(lambda half: 0, lambda half: (half + 1) * hk, bias, lambda half: vt_ref[0, 0, 0][:, 0:(half + 1) * hk])


def _attn_scratch(tq):
    return [pltpu.VMEM((HEADS_PER_GROUP, ATTN_TILE, tq), F32), pltpu.VMEM((HEADS_PER_GROUP, 1, tq), F32),
            pltpu.VMEM((HEADS_PER_GROUP, 1, tq), F32), pltpu.VMEM((HEADS_PER_GROUP, HEAD_DIM, tq), F32)]


def _attn_common_specs(t, k_set, branch):
    tq = ATTN_TILE
    q_spec = pl.BlockSpec((GROUP_Q_WIDTH, tq), lambda gi, qi: (gi, qi))
    k_spec = pl.BlockSpec((t, HEAD_DIM), lambda gi, qi: (0, k_set * A_KV_GROUPS + gi))
    vt_spec = pl.BlockSpec((1, 1, t // ATTN_TILE, HEAD_DIM, ATTN_TILE), lambda gi, qi: (k_set, gi, 0, 0, 0))
    out_spec = pl.BlockSpec((tq, GROUP_Q_WIDTH), lambda gi, qi: (qi, gi))
    return q_spec, k_spec, vt_spec, _gate_spec(branch, tq), out_spec


def _bt_spec(kind):
    return pl.BlockSpec((HEADS_PER_GROUP, 1, ATTN_TILE, ATTN_TILE), lambda gi, qi: (gi, kind, 0, 0))


def _attn_block_bytes(t):
    tq = ATTN_TILE
    return [tq * GROUP_Q_WIDTH * 2 * 2, t * HEAD_DIM * 2 * 2, 2 * HEADS_PER_GROUP * tq * tq * 4]


def _sel_attn(q2, k_plain, vt, sel_neg, bt, gates_t, t):
    tq = ATTN_TILE
    q_spec, k_spec, vt_spec, gate_spec, out_spec = _attn_common_specs(t, 0, 1)
    sel_spec = pl.BlockSpec((1, V7X_LANES, tq), lambda gi, qi: (gi, 0, qi))
    e_onehot = (jnp.arange(t, dtype=jnp.int32)[:, None] // L_SEL
                == jnp.arange(V7X_LANES, dtype=jnp.int32)[None, :]).astype(BF16)
    e_spec = pl.BlockSpec((t, V7X_LANES), lambda gi, qi: (0, 0))
    return pl.pallas_call(
        _sel_attn_kernel,
        grid=(A_KV_GROUPS, t // tq),
        in_specs=[q_spec, k_spec, e_spec, vt_spec, sel_spec, _bt_spec(0), _bt_spec(1), gate_spec],
        out_specs=out_spec,
        out_shape=jax.ShapeDtypeStruct((t, A_WIDTH), BF16),
        scratch_shapes=([pltpu.VMEM((HEADS_PER_GROUP, 2 * HEAD_DIM, tq), BF16)] + _attn_scratch(tq)[:1]
                        + [pltpu.VMEM((HEADS_PER_GROUP, 1, tq), F32)] + _attn_scratch(tq)[1:]),
        compiler_params=_params(("parallel", "arbitrary"),
                                _attn_block_bytes(t) + [t * V7X_LANES * 2, tq * V7X_LANES * 2],
                                (8 + HEADS_PER_GROUP) * tq * tq * 4),
        name="sel_attn",
    )(q2, k_plain, e_onehot, vt, sel_neg, bt, bt, gates_t)


def _win_attn(q2, k_plain, vt, bt, gates_t, o_cmp, o_sel, za, t):
    tq = ATTN_TILE
    q_spec, k_spec, vt_spec, gate_spec, out_spec = _attn_common_specs(t, 1, 2)
    return pl.pallas_call(
        _win_attn_kernel,
        grid=(A_KV_GROUPS, t // tq),
        in_specs=[q_spec, k_spec, vt_spec, _bt_spec(0), _bt_spec(2), gate_spec, out_spec, out_spec, out_spec],
        out_specs=out_spec,
        out_shape=jax.ShapeDtypeStruct((t, A_WIDTH), BF16),
        scratch_shapes=[pltpu.VMEM((2 * HEADS_PER_GROUP, 3 * tq // 2, tq // 2), F32)],
        compiler_params=_params(("parallel", "arbitrary"),
                                _attn_block_bytes(t) + [3 * tq * GROUP_Q_WIDTH * 2],
                                (8 + HEADS_PER_GROUP) * tq * tq * 4),
        name="win_attn",
    )(q2, k_plain, vt, bt, bt, gates_t, o_cmp, o_sel, za)


def _gmlp_kernel(u_ref, v_ref, z_ref, ng_ref, w_ref, bt_ref, o_ref):
    v = v_ref[...].astype(F32)
    ms = jnp.mean(v * v, axis=-1, keepdims=True)
    vn = (v * lax.rsqrt(ms + NORM_EPS) * ng_ref[...]).astype(BF16)
    p_idx = lax.broadcasted_iota(jnp.int32, (CHUNK, CHUNK), 0)
    q_idx = lax.broadcasted_iota(jnp.int32, (CHUNK, CHUNK), 1)
    causal = q_idx <= p_idx
    gd = B_WIDTH // B_GROUPS
    for gg in range(B_GROUPS):
        cols = slice(gg * gd, (gg + 1) * gd)
        w = jnp.where(causal, w_ref[gg], jnp.zeros((), w_ref.dtype))
        f = jnp.dot(w, vn[:, cols], preferred_element_type=F32) + bt_ref[:, gg:gg + 1]
        o_ref[:, cols] = (u_ref[:, cols].astype(F32) * f * z_ref[:, cols].astype(F32)).astype(o_ref.dtype)


def _gmlp(uv, zb, norm_g, w_s, b_t, t):
    bw = B_WIDTH

    def col_spec(col):
        return pl.BlockSpec((CHUNK, bw), lambda i: (i, col))

    return pl.pallas_call(
        _gmlp_kernel,
        grid=(t // CHUNK,),
        in_specs=[col_spec(0), col_spec(1), col_spec(0),
                  pl.BlockSpec((1, bw), lambda i: (0, 0)),
                  pl.BlockSpec((B_GROUPS, CHUNK, CHUNK), lambda i: (0, 0, 0)),
                  pl.BlockSpec((CHUNK, B_GROUPS), lambda i: (0, 0))],
        out_specs=pl.BlockSpec((CHUNK, bw), lambda i: (i, 0)),
        out_shape=jax.ShapeDtypeStruct((t, bw), BF16),
        compiler_params=_params(("parallel",), [CHUNK * bw * 2] * 4 + [B_GROUPS * CHUNK * CHUNK * 2],
                                4 * CHUNK * bw * 4),
        name="gmlp",
    )(uv, uv, zb, norm_g.reshape(1, bw), w_s, b_t)


def _merge_kernel(oa_ref, ob_ref, wa_ref, wb_ref, ma_ref, mb_ref, o_ref):
    ya = _nn_dot(oa_ref[...], wa_ref[0])
    yb = _nn_dot(ob_ref[...], wb_ref[0])
    o_ref[...] = (ma_ref[...].astype(F32) * ya + mb_ref[...].astype(F32) * yb).astype(o_ref.dtype)


def _merge(o_a, o_b, m_gates, w_a, w_b, layer, t, d):
    tm = min(MM_TM, t)
    tn = min(W32_TN, d)
    aw = A_WIDTH
    bw = B_WIDTH
    return pl.pallas_call(
        _merge_kernel,
        grid=(d // tn, t // tm),
        in_specs=[pl.BlockSpec((tm, aw), lambda j, i: (i, 0)),
                  pl.BlockSpec((tm, bw), lambda j, i: (i, 0)),
                  pl.BlockSpec((1, aw, tn), lambda j, i: (layer, 0, j)),
                  pl.BlockSpec((1, bw, tn), lambda j, i: (layer, 0, j)),
                  pl.BlockSpec((tm, tn), lambda j, i: (i, j)),
                  pl.BlockSpec((tm, tn), lambda j, i: (i, d // tn + j))],
        out_specs=pl.BlockSpec((tm, tn), lambda j, i: (i, j)),
        out_shape=jax.ShapeDtypeStruct((t, d), BF16),
        compiler_params=_params(("arbitrary", "arbitrary"),
                                [tm * aw * 2] * 2 + [aw * tn * 4, bw * tn * 4] + [tm * tn * 2] * 3,
                                4 * tm * tn * 4),
        name="merge",
    )(o_a, o_b, w_a, w_b, m_gates, m_gates)


def _gate_weight_columns():
    src = np.full((V7X_LANES,), -1, np.int64)
    for br in range(3):
        for g in range(A_KV_GROUPS):
            for r in range(HEADS_PER_GROUP):
                src[(br * A_KV_GROUPS + g) * V7X_SUBLANES + r] = (g * HEADS_PER_GROUP + r) * 3 + br
    return src


def kernel(x, rel_bias, pre_norm, w_in, cmp_pe_k, cmp_w1_k, cmp_w2_k, cmp_pe_v, cmp_w1_v, cmp_w2_v,
           w_out_a, sgu_norm, sgu_w, sgu_b, w_out_b, w_out, post_norm):
    batch, t, d = x.shape
    assert batch == 1 and t % MM_TM == 0 and N_SEL <= t // L_SEL <= V7X_LANES
    depth = w_in.shape[0]
    g = A_KV_GROUPS
    xs = x.reshape(t, d)

    w_in_t = jnp.swapaxes(w_in, 1, 2)
    gate_src = _gate_weight_columns()
    w_gate = jnp.take(w_in_t[:, SRC_GATES:SRC_REST, :], jnp.asarray(np.maximum(gate_src, 0)), axis=1)
    w_gate = jnp.where(jnp.asarray(gate_src >= 0)[None, :, None], w_gate, 0.0).astype(BF16)
    w1 = jnp.stack([cmp_w1_k, cmp_w1_v], axis=1).astype(BF16)
    w2 = jnp.stack([cmp_w2_k, cmp_w2_v], axis=1).astype(BF16)
    slab = STRIDE_CMP * HEAD_DIM
    pe = jnp.stack([cmp_pe_k, cmp_pe_v], axis=1).reshape(depth, 2, 2, slab)
    w_s = sgu_w.astype(BF16)

    bias_tiles = _bias_tiles(rel_bias)
    bias_strip = _bias_strip(rel_bias)

    h = _rmsnorm(xs, pre_norm[0], BF16)
    for layer in range(depth):
        q2 = _q_proj(h, w_in_t, layer)
        x_slabs = _cmp_proj(h, w_in_t, layer)
        k_plain, vt = _kv_proj(h, w_in_t, layer, SRC_KV + 2 * A_KV_WIDTH, 2)
        gates_t = _gate_proj(h, w_gate, layer)
        za = _mm_w32(h, w_in_t, layer, SRC_REST + COL_ZA, A_WIDTH, BF16, "silu", "za_proj")
        uv = _mm_w32(h, w_in_t, layer, SRC_REST + COL_U, 2 * B_WIDTH, BF16, "gelu", "uv_proj")
        zb = _mm_w32(h, w_in_t, layer, SRC_REST + COL_ZB, B_WIDTH, BF16, "silu", "zb_proj")
        m_gates = _mm_w32(h, w_in_t, layer, SRC_REST + COL_MA, 2 * d, BF16, "sigmoid", "m_proj")

        kc, kct = _compress(x_slabs, pe[layer], w1[layer], w2[layer])
        o_cmp, sel_neg = _cmp_attn(q2, kc, kct, bias_strip, gates_t, t)
        o_sel = _sel_attn(q2, k_plain, vt, sel_neg, bias_tiles, gates_t, t)
        o_a = _win_attn(q2, k_plain, vt, bias_tiles, gates_t, o_cmp, o_sel, za, t)

        o_b = _gmlp(uv, zb, sgu_norm[layer], w_s[layer], sgu_b[layer].T, t)
        merged = _merge(o_a, o_b, m_gates, w_out_a, w_out_b, layer, t, d)
        y = _mm(merged, w_out, layer, BF16, name="out_proj")
        g_next = pre_norm[layer + 1] if layer + 1 < depth else pre_norm[layer]
        xs, h = _post_norm_residual(xs, y, post_norm[layer], g_next)
    del g
    return xs.reshape(batch, t, d)
```

```python
import functools
import math

import numpy as np
import jax
import jax.numpy as jnp
from jax import lax
from jax.experimental import pallas as pl
from jax.experimental.pallas import tpu as pltpu

F32 = jnp.float32
BF16 = jnp.bfloat16

A_HEADS = 16
A_KV_GROUPS = 4
HEADS_PER_GROUP = A_HEADS // A_KV_GROUPS
HEAD_DIM = 128
A_WIDTH = A_HEADS * HEAD_DIM
A_KV_WIDTH = A_KV_GROUPS * HEAD_DIM
GROUP_Q_WIDTH = HEADS_PER_GROUP * HEAD_DIM
L_CMP = 32
STRIDE_CMP = 16
L_SEL = 64
N_SEL = 16
WINDOW = 512
B_GROUPS = 16
CHUNK = 128
B_WIDTH = 2048
NUM_BUCKETS = 32
MAX_DISTANCE = 128
NORM_EPS = 1e-6
NEG_INF = -1e30
SCALE = HEAD_DIM ** -0.5
LOG2_E = math.log2(math.e)
SCALE_LOG2 = SCALE * LOG2_E
N_GATE_COLS = 3 * A_HEADS

V7X_LANES = 128
V7X_SUBLANES = 8
V7X_SCOPED_VMEM_CAP_BYTES = 60000 * 1024

ATTN_TILE = 512
CMP_TQ = 256
CMP_ROWS_PER_TQ = CMP_TQ // STRIDE_CMP
CMP_SUB = 2
MM_TM = 1024
MM_TN = 1024
W32_TN = 512
W32_TM = 2048
MM_ROW_CHUNK = 512
GMLP_CHUNKS = 4

SRC_KV = A_WIDTH
SRC_GATES = SRC_KV + 6 * A_KV_WIDTH
SRC_REST = SRC_GATES + N_GATE_COLS
COL_ZA = 0
COL_U = COL_ZA + A_WIDTH
COL_V = COL_U + B_WIDTH
COL_ZB = COL_V + B_WIDTH
COL_MA = COL_ZB + B_WIDTH


def _vmem_limit(block_bytes, temp_bytes=0):
    need = 2 * sum(block_bytes) + temp_bytes + (4 << 20)
    return int(min(max(need, 16 << 20), V7X_SCOPED_VMEM_CAP_BYTES))


def _params(sem, block_bytes, temp_bytes=0, flags=None):
    return pltpu.CompilerParams(dimension_semantics=sem,
                                vmem_limit_bytes=_vmem_limit(block_bytes, temp_bytes), flags=flags)


def _t5_thresholds():
    n = np.arange(0, 4 * MAX_DISTANCE)
    max_exact = NUM_BUCKETS // 2
    nf = np.maximum(n, 1).astype(np.float32)
    large = max_exact + (np.log(nf / np.float32(max_exact)) / np.float32(math.log(MAX_DISTANCE / max_exact))
                         * np.float32(NUM_BUCKETS - max_exact)).astype(np.int32)
    bucket = np.where(n < max_exact, n, np.minimum(large, NUM_BUCKETS - 1))
    assert np.all(np.diff(bucket) >= 0) and bucket[-1] == NUM_BUCKETS - 1
    return [int(np.argmax(bucket >= b)) for b in range(1, NUM_BUCKETS)]


_T5_THR = _t5_thresholds()


def _sigmoid(x):
    return 1.0 / (1.0 + jnp.exp(-x))


def _silu(x):
    return x * _sigmoid(x)


def _gelu(x):
    return jax.nn.gelu(x, approximate=True)


_ACTS = {"none": lambda v: v, "silu": _silu, "gelu": _gelu, "sigmoid": _sigmoid}


def _rmsnorm_kernel(x_ref, g_ref, o_ref):
    x = x_ref[...]
    ms = jnp.mean(x * x, axis=-1, keepdims=True)
    o_ref[...] = (x * lax.rsqrt(ms + NORM_EPS) * g_ref[...]).astype(o_ref.dtype)


def _rmsnorm(x, g, out_dtype):
    t, d = x.shape
    tm = min(256, t)
    return pl.pallas_call(
        _rmsnorm_kernel,
        grid=(t // tm,),
        in_specs=[pl.BlockSpec((tm, d), lambda i: (i, 0)), pl.BlockSpec((1, d), lambda i: (0, 0))],
        out_specs=pl.BlockSpec((tm, d), lambda i: (i, 0)),
        out_shape=jax.ShapeDtypeStruct((t, d), out_dtype),
        compiler_params=_params(("parallel",), [tm * d * 4, tm * d * 4]),
        name="rmsnorm",
    )(x, g.reshape(1, d))


def _post_kernel(x_ref, y_ref, g_ref, gn_ref, o_ref, h_ref):
    y = y_ref[...].astype(F32)
    ms = jnp.mean(y * y, axis=-1, keepdims=True)
    x = x_ref[...] + y * lax.rsqrt(ms + NORM_EPS) * g_ref[...]
    o_ref[...] = x
    ms_x = jnp.mean(x * x, axis=-1, keepdims=True)
    h_ref[...] = (x * lax.rsqrt(ms_x + NORM_EPS) * gn_ref[...]).astype(h_ref.dtype)


def _post_norm_residual(x, y, g, g_next):
    t, d = x.shape
    tm = min(256, t)
    row = pl.BlockSpec((tm, d), lambda i: (i, 0))
    vec = pl.BlockSpec((1, d), lambda i: (0, 0))
    return pl.pallas_call(
        _post_kernel,
        grid=(t // tm,),
        in_specs=[row, row, vec, vec],
        out_specs=[row, row],
        out_shape=[jax.ShapeDtypeStruct((t, d), F32), jax.ShapeDtypeStruct((t, d), BF16)],
        compiler_params=_params(("parallel",), [tm * d * 4] * 4),
        name="post_norm_residual",
    )(x, y, g.reshape(1, d), g_next.reshape(1, d))


def _nt_dot(a, b):
    return lax.dot_general(a, b, (((1,), (1,)), ((), ())), preferred_element_type=F32)


def _nn_dot(a, b):
    return lax.dot_general(a, b, (((1,), (0,)), ((), ())), preferred_element_type=F32)


def _mm_kernel(a_ref, b_ref, o_ref, *, act, nt, n_chunks):
    cw = o_ref.shape[1] // n_chunks
    tm = o_ref.shape[0]
    rh = MM_ROW_CHUNK if n_chunks > 1 and tm % MM_ROW_CHUNK == 0 else tm
    for rc in range(o_ref.shape[0] // rh):
        rows = slice(rc * rh, (rc + 1) * rh)
        a = a_ref[rows, :]
        for c in range(n_chunks):
            cols = slice(c * cw, (c + 1) * cw)
            if nt:
                acc = _nt_dot(a, b_ref[0, cols, :])
            else:
                acc = _nn_dot(a, b_ref[0, :, cols])
            o_ref[rows, cols] = _ACTS[act](acc).astype(o_ref.dtype)


def _mm(a, b, layer, out_dtype, name):
    m, k = a.shape
    n = b.shape[2]
    tm = min(W32_TM, m)
    tn = min(W32_TN, n)
    osz = jnp.dtype(out_dtype).itemsize
    return pl.pallas_call(
        functools.partial(_mm_kernel, act="none", nt=False, n_chunks=1),
        grid=(n // tn, m // tm),
        in_specs=[pl.BlockSpec((tm, k), lambda j, i: (i, 0)),
                  pl.BlockSpec((1, k, tn), lambda j, i: (layer, 0, j))],
        out_specs=pl.BlockSpec((tm, tn), lambda j, i: (i, j)),
        out_shape=jax.ShapeDtypeStruct((m, n), out_dtype),
        compiler_params=_params(("arbitrary", "arbitrary"), [tm * k * 2, k * tn * 4, tm * tn * osz],
                                2 * tm * tn * 4),
        name=name,
    )(a, b)


def _w_rows_spec(layer, row0, rows, k):
    assert row0 % V7X_SUBLANES == 0
    return pl.BlockSpec((pl.Element(1), pl.Element(rows), pl.Element(k)),
                        lambda j, i: (layer, pl.multiple_of(row0 + j * rows, V7X_SUBLANES), 0))


def _mm_w32(a, w_t, layer, row0, n, out_dtype, act, name):
    m, k = a.shape
    tm = min(W32_TM, m)
    tn = next(c for c in (W32_TN, 256, 128) if n % c == 0)
    n_chunks = 1 if act == "none" else max(1, tn // 256)
    osz = jnp.dtype(out_dtype).itemsize
    return pl.pallas_call(
        functools.partial(_mm_kernel, act=act, nt=True, n_chunks=n_chunks),
        grid=(n // tn, m // tm),
        in_specs=[pl.BlockSpec((tm, k), lambda j, i: (i, 0)), _w_rows_spec(layer, row0, tn, k)],
        out_specs=pl.BlockSpec((tm, tn), lambda j, i: (i, j)),
        out_shape=jax.ShapeDtypeStruct((m, n), out_dtype),
        compiler_params=_params(("arbitrary", "arbitrary"), [tm * k * 2, k * tn * 4, tm * tn * osz],
                                2 * tm * tn * 4),
        name=name,
    )(a, w_t)


def _q_proj_kernel(a_ref, b_ref, o_ref):
    o_ref[...] = (_nt_dot(b_ref[0], a_ref[...]) * SCALE_LOG2).astype(o_ref.dtype)


def _q_proj(h, w_t, layer):
    t, k = h.shape
    n = A_WIDTH
    tm = min(W32_TM, t)
    tn = W32_TN
    return pl.pallas_call(
        _q_proj_kernel,
        grid=(n // tn, t // tm),
        in_specs=[pl.BlockSpec((tm, k), lambda j, i: (i, 0)), _w_rows_spec(layer, 0, tn, k)],
        out_specs=pl.BlockSpec((tn, tm), lambda j, i: (j, i)),
        out_shape=jax.ShapeDtypeStruct((n, t), BF16),
        compiler_params=_params(("arbitrary", "arbitrary"), [tm * k * 2, k * tn * 4, tm * tn * 2],
                                2 * tm * tn * 4),
        name="q_proj",
    )(h, w_t)


def _gate_proj_kernel(a_ref, b_ref, o_ref):
    o_ref[...] = _sigmoid(_nt_dot(b_ref[0], a_ref[...]))


def _gate_proj(h, w_gate, layer):
    t, k = h.shape
    tm = min(MM_TM, t)
    return pl.pallas_call(
        _gate_proj_kernel,
        grid=(t // tm,),
        in_specs=[pl.BlockSpec((tm, k), lambda i: (i, 0)),
                  pl.BlockSpec((1, V7X_LANES, k), lambda i: (layer, 0, 0))],
        out_specs=pl.BlockSpec((V7X_LANES, tm), lambda i: (0, i)),
        out_shape=jax.ShapeDtypeStruct((V7X_LANES, t), F32),
        compiler_params=_params(("parallel",), [tm * k * 2, k * V7X_LANES * 2, tm * V7X_LANES * 4]),
        name="gate_proj",
    )(h, w_gate)


def _cmp_proj_kernel(a_ref, b_ref, o_ref, scr):
    acc = _nt_dot(a_ref[...], b_ref[0])
    n_sg = scr.shape[0]
    slab_rows = o_ref.shape[1]
    for sg in range(n_sg):
        scr[sg] = acc[:, sg * HEAD_DIM:(sg + 1) * HEAD_DIM]
    for sg in range(n_sg):
        for l in range(STRIDE_CMP):
            o_ref[sg, :, l * HEAD_DIM:(l + 1) * HEAD_DIM] = (
                scr[sg, pl.ds(l, slab_rows, stride=STRIDE_CMP), :].astype(o_ref.dtype))


def _cmp_proj(h, w_t, layer):
    t, k = h.shape
    tm = min(MM_TM, t)
    g = A_KV_GROUPS
    slab = STRIDE_CMP * HEAD_DIM
    return pl.pallas_call(
        _cmp_proj_kernel,
        grid=(2, t // tm),
        in_specs=[pl.BlockSpec((tm, k), lambda j, i: (i, 0)), _w_rows_spec(layer, SRC_KV, A_KV_WIDTH, k)],
        out_specs=pl.BlockSpec((g, tm // STRIDE_CMP, slab), lambda j, i: (j, i, 0)),
        out_shape=jax.ShapeDtypeStruct((2 * g, t // STRIDE_CMP, slab), BF16),
        scratch_shapes=[pltpu.VMEM((g, tm, HEAD_DIM), F32)],
        compiler_params=_params(("arbitrary", "arbitrary"),
                                [tm * k * 2, k * A_KV_WIDTH * 4, tm * A_KV_WIDTH * 2], 3 * tm * A_KV_WIDTH * 4),
        name="cmp_proj",
    )(h, w_t)


def _kv_proj_kernel(a_ref, b_ref, k_ref, vt_ref):
    a = a_ref[...]
    k_ref[...] = _nt_dot(a, b_ref[0, :A_KV_WIDTH, :]).astype(k_ref.dtype)
    vt = _nt_dot(b_ref[0, A_KV_WIDTH:, :], a).astype(vt_ref.dtype)
    for g in range(A_KV_GROUPS):
        for s in range(vt_ref.shape[2]):
            vt_ref[0, g, s] = vt[g * HEAD_DIM:(g + 1) * HEAD_DIM, s * ATTN_TILE:(s + 1) * ATTN_TILE]


def _kv_proj(h, w_t, layer, row0, n_sets):
    t, k = h.shape
    tm = min(MM_TM // 2, t)
    tiles = tm // ATTN_TILE
    return pl.pallas_call(
        _kv_proj_kernel,
        grid=(n_sets, t // tm),
        in_specs=[pl.BlockSpec((tm, k), lambda j, i: (i, 0)), _w_rows_spec(layer, row0, 2 * A_KV_WIDTH, k)],
        out_specs=[pl.BlockSpec((tm, A_KV_WIDTH), lambda j, i: (i, j)),
                   pl.BlockSpec((1, A_KV_GROUPS, tiles, HEAD_DIM, ATTN_TILE), lambda j, i: (j, 0, i, 0, 0))],
        out_shape=[jax.ShapeDtypeStruct((t, n_sets * A_KV_WIDTH), BF16),
                   jax.ShapeDtypeStruct((n_sets, A_KV_GROUPS, t // ATTN_TILE, HEAD_DIM, ATTN_TILE), BF16)],
        compiler_params=_params(("arbitrary", "arbitrary"),
                                [tm * k * 2, k * 2 * A_KV_WIDTH * 4, tm * 2 * A_KV_WIDTH * 2],
                                3 * tm * 2 * A_KV_WIDTH * 4),
        name="kv_proj",
    )(h, w_t)


def _t5_bias_rel(dist, rb_ref, h):
    far = rb_ref[NUM_BUCKETS - 1, h]
    val = jnp.zeros(dist.shape, F32)
    for b in range(NUM_BUCKETS - 2, -1, -1):
        val = jnp.where(dist < _T5_THR[b], (rb_ref[b, h] - far) * LOG2_E, val)
    return val


def _bias_tile_kernel(rb_ref, o_ref):
    h = pl.program_id(0)
    sb = V7X_LANES
    key = lax.broadcasted_iota(jnp.int32, (sb, sb), 0)
    qry = lax.broadcasted_iota(jnp.int32, (sb, sb), 1)
    far_dist = _T5_THR[-1]
    for kind, (delta, limit) in enumerate(((0, None), (ATTN_TILE, None), (ATTN_TILE, WINDOW))):
        for bi in range(ATTN_TILE // sb):
            for bj in range(ATTN_TILE // sb):
                base = delta + sb * (bj - bi)
                lo, hi = base - (sb - 1), base + (sb - 1)
                if hi < 0 or (limit is not None and lo >= limit):
                    val = jnp.full((sb, sb), NEG_INF, F32)
                else:
                    dist = base + qry - key
                    val = _t5_bias_rel(dist, rb_ref, h) if lo < far_dist else jnp.zeros((sb, sb), F32)
                    if lo < 0:
                        val = jnp.where(dist >= 0, val, NEG_INF)
                    if limit is not None and hi >= limit:
                        val = jnp.where(dist < limit, val, NEG_INF)
                o_ref[0, kind, bi * sb:(bi + 1) * sb, bj * sb:(bj + 1) * sb] = val


def _bias_tiles(rel_bias):
    return pl.pallas_call(
        _bias_tile_kernel,
        grid=(A_HEADS,),
        in_specs=[pl.BlockSpec(memory_space=pltpu.SMEM)],
        out_specs=pl.BlockSpec((1, 3, ATTN_TILE, ATTN_TILE), lambda h: (h, 0, 0, 0)),
        out_shape=jax.ShapeDtypeStruct((A_HEADS, 3, ATTN_TILE, ATTN_TILE), F32),
        compiler_params=_params(("parallel",), [3 * ATTN_TILE * ATTN_TILE * 4]),
        name="bias_tiles",
    )(rel_bias)


def _bias_strip_kernel(rb_ref, o_ref):
    h = pl.program_id(0)
    shape = (2 * CMP_ROWS_PER_TQ, CMP_TQ)
    c = lax.broadcasted_iota(jnp.int32, shape, 0)
    a = lax.broadcasted_iota(jnp.int32, shape, 1)
    dist = a - STRIDE_CMP * c + (CMP_TQ - (L_CMP - 1))
    o_ref[0] = jnp.where(dist >= 0, _t5_bias_rel(dist, rb_ref, h), NEG_INF)


def _bias_strip(rel_bias):
    rows = 2 * CMP_ROWS_PER_TQ
    return pl.pallas_call(
        _bias_strip_kernel,
        grid=(A_HEADS,),
        in_specs=[pl.BlockSpec(memory_space=pltpu.SMEM)],
        out_specs=pl.BlockSpec((1, rows, CMP_TQ), lambda h: (h, 0, 0)),
        out_shape=jax.ShapeDtypeStruct((A_HEADS, rows, CMP_TQ), F32),
        compiler_params=_params(("parallel",), [rows * CMP_TQ * 4]),
        name="bias_strip",
    )(rel_bias)


def _compress_kernel(x_ref, pe_ref, w1_ref, w2_ref, o_ref, ot_ref):
    x = x_ref[0].astype(F32)
    half = x.shape[1]
    lo = (x + pe_ref[0, 0:1, :]).astype(BF16)
    hi = (x + pe_ref[0, 1:2, :]).astype(BF16)
    a = jnp.dot(lo, w1_ref[0, :half, :], preferred_element_type=F32)
    b = jnp.dot(hi, w1_ref[0, half:, :], preferred_element_type=F32)
    n_rows = x.shape[0]
    hidden = a + pltpu.roll(b, n_rows - 1, 0)
    out = jnp.dot(_silu(hidden).astype(BF16), w2_ref[0], preferred_element_type=F32)
    o_ref[0] = out.astype(o_ref.dtype)
    ot_ref[0] = out.T.astype(ot_ref.dtype)


def _compress(x_slabs, pe, w1, w2):
    ng, n_rows, half = x_slabs.shape
    g = A_KV_GROUPS
    return pl.pallas_call(
        _compress_kernel,
        grid=(2, g),
        in_specs=[pl.BlockSpec((1, n_rows, half), lambda s, i: (s * g + i, 0, 0)),
                  pl.BlockSpec((1, 2, half), lambda s, i: (s, 0, 0)),
                  pl.BlockSpec((1, 2 * half, HEAD_DIM), lambda s, i: (s, 0, 0)),
                  pl.BlockSpec((1, HEAD_DIM, HEAD_DIM), lambda s, i: (s, 0, 0))],
        out_specs=[pl.BlockSpec((1, n_rows, HEAD_DIM), lambda s, i: (s * g + i, 0, 0)),
                   pl.BlockSpec((1, HEAD_DIM, n_rows), lambda s, i: (s * g + i, 0, 0))],
        out_shape=[jax.ShapeDtypeStruct((ng, n_rows, HEAD_DIM), BF16),
                   jax.ShapeDtypeStruct((ng, HEAD_DIM, n_rows), BF16)],
        compiler_params=_params(("parallel", "parallel"),
                                [n_rows * half * 2, 2 * half * 4, 2 * half * HEAD_DIM * 2],
                                4 * n_rows * half * 4),
        name="compress",
    )(x_slabs, pe, w1, w2)


def _cmp_attn_kernel(q_ref, kc_ref, vct_ref, strip_ref, gate_ref, o_ref, sel_ref, s_scr, pg_scr):
    n_rows = kc_ref.shape[1]
    n_sel = n_rows // (L_SEL // STRIDE_CMP)
    tq = CMP_TQ
    rpt = CMP_ROWS_PER_TQ
    pad = V7X_SUBLANES
    lane_tiles = tq // V7X_LANES
    per_sel = L_SEL // STRIDE_CMP
    sub_qi = [pl.program_id(1) * CMP_SUB + sub for sub in range(CMP_SUB)]
    t_rows = [qi * tq + lax.broadcasted_iota(jnp.int32, (1, tq), 1) for qi in sub_qi]

    for sub, qi in enumerate(sub_qi):
        lanes = slice(sub * tq, (sub + 1) * tq)
        key_live = lax.broadcasted_iota(jnp.int32, (n_rows, tq), 0) < (qi + 1) * rpt
        strip_rows = pl.ds(pl.multiple_of(qi * rpt, rpt), 2 * rpt)
        for r in range(HEADS_PER_GROUP):
            st = jnp.dot(kc_ref[0], q_ref[r * HEAD_DIM:(r + 1) * HEAD_DIM, lanes], preferred_element_type=F32)
            s_scr[sub, r, 0:rpt, :] = jnp.zeros((rpt, tq), F32)
            s_scr[sub, r, rpt:rpt + n_rows, :] = jnp.where(key_live, st, NEG_INF)
            s_scr[sub, r, strip_rows, :] = s_scr[sub, r, strip_rows, :] + strip_ref[r]

    for sub in range(CMP_SUB):
        lanes = slice(sub * tq, (sub + 1) * tq)
        any_valid = jnp.where(t_rows[sub] >= L_CMP - 1, 1.0, 0.0)
        pg = jnp.zeros((n_rows, tq), F32)
        for r in range(HEADS_PER_GROUP):
            s = s_scr[sub, r, rpt:rpt + n_rows, :]
            m = jnp.max(s, axis=0, keepdims=True)
            e = jnp.exp2(s - m)
            p = e * (any_valid / jnp.sum(e, axis=0, keepdims=True))
            pg = pg + p
            ot = jnp.dot(vct_ref[0], p.astype(BF16), preferred_element_type=F32) * gate_ref[r:r + 1, lanes]
            o_ref[lanes, r * HEAD_DIM:(r + 1) * HEAD_DIM] = ot.T.astype(o_ref.dtype)
        for c in range(lane_tiles):
            pg_scr[sub, c, 0:pad, :] = jnp.zeros((pad, V7X_LANES), F32)
            pg_scr[sub, c, pad:pad + n_rows, :] = pg[:, c * V7X_LANES:(c + 1) * V7X_LANES]

    for sub in range(CMP_SUB):
        def rows(off, sub=sub):
            return jnp.concatenate(
                [pg_scr[sub, c, pl.ds(pad + off, n_sel, stride=per_sel), :] for c in range(lane_tiles)], axis=1)

        imp = (rows(-1) + rows(3)) + 2.0 * (rows(0) + rows(1) + rows(2))
        j = lax.broadcasted_iota(jnp.int32, (n_sel, tq), 0)
        jf = j.astype(F32)
        jt = jnp.right_shift(t_rows[sub], int(math.log2(L_SEL)))
        forced = (j == 0) | (j == jt) | (j == jt - 1)
        cand = (j >= 1) & (j <= jt - 2)
        score = jnp.where(cand, imp, -1.0)
        sel = jnp.where(forced, 1.0, 0.0)
        for _ in range(N_SEL - 3):
            mx = jnp.max(score, axis=0, keepdims=True)
            first_j = jnp.min(jnp.where(score == mx, jf, float(n_sel)), axis=0, keepdims=True)
            pick = (jf == first_j) & (mx >= 0.0)
            sel = jnp.where(pick, 1.0, sel)
            score = jnp.where(pick, -2.0, score)
        neg = jnp.where(sel > 0.5, 0.0, NEG_INF)
        if n_sel < V7X_LANES:
            neg = jnp.concatenate([neg, jnp.zeros((V7X_LANES - n_sel, tq), F32)], axis=0)
        sel_ref[0, :, sub * tq:(sub + 1) * tq] = neg.astype(sel_ref.dtype)


def _cmp_attn(q2, kc, vct, strip, gates_t, t):
    n_rows = kc.shape[1]
    g = A_KV_GROUPS
    tq = CMP_TQ
    tqs = CMP_SUB * tq
    return pl.pallas_call(
        _cmp_attn_kernel,
        grid=(g, t // tqs),
        in_specs=[pl.BlockSpec((GROUP_Q_WIDTH, tqs), lambda gi, qi: (gi, qi)),
                  pl.BlockSpec((1, n_rows, HEAD_DIM), lambda gi, qi: (gi, 0, 0)),
                  pl.BlockSpec((1, HEAD_DIM, n_rows), lambda gi, qi: (g + gi, 0, 0)),
                  pl.BlockSpec((HEADS_PER_GROUP, 2 * CMP_ROWS_PER_TQ, tq), lambda gi, qi: (gi, 0, 0)),
                  _gate_spec(0, tqs)],
        out_specs=[pl.BlockSpec((tqs, GROUP_Q_WIDTH), lambda gi, qi: (qi, gi)),
                   pl.BlockSpec((1, V7X_LANES, tqs), lambda gi, qi: (gi, 0, qi))],
        out_shape=[jax.ShapeDtypeStruct((t, A_WIDTH), BF16),
                   jax.ShapeDtypeStruct((g, V7X_LANES, t), BF16)],
        scratch_shapes=[pltpu.VMEM((CMP_SUB, HEADS_PER_GROUP, CMP_ROWS_PER_TQ + n_rows, tq), F32),
                        pltpu.VMEM((CMP_SUB, tq // V7X_LANES, n_rows + V7X_SUBLANES, V7X_LANES), F32)],
        compiler_params=_params(("parallel", "arbitrary"),
                                [tqs * GROUP_Q_WIDTH * 2 * 2, n_rows * HEAD_DIM * 2 * 2, tqs * V7X_LANES * 2],
                                CMP_SUB * 12 * n_rows * tq * 4),
        name="cmp_attn",
    )(q2, kc, vct, strip, gates_t)


def _gate_spec(branch, tq):
    return pl.BlockSpec((V7X_SUBLANES, tq), lambda gi, qi: (branch * A_KV_GROUPS + gi, qi))


def _softmax_step(r, z, z_max, vt, m_scr, l_scr, acc_scr):
    m_prev = m_scr[r]
    m_new = jnp.maximum(m_prev, z_max)
    p = jnp.exp2(z - m_new)
    alpha = jnp.exp2(m_prev - m_new)
    l_scr[r] = alpha * l_scr[r] + jnp.sum(p, axis=0, keepdims=True)
    acc_scr[r] = acc_scr[r] * alpha + jnp.dot(vt, p.astype(BF16), preferred_element_type=F32)
    m_scr[r] = m_new


def _pipelined_tile(kvj, kvj_next, logits_head, logits_next, softmax_head):
    last = HEADS_PER_GROUP - 1
    for r in range(HEADS_PER_GROUP):
        if r < last:
            logits_head(r + 1, kvj)
        else:
            logits_next(0, kvj_next)
        softmax_head(r, kvj)


def _softmax_init(m_scr, l_scr, acc_scr):
    m_scr[...] = jnp.full(m_scr.shape, NEG_INF, F32)
    l_scr[...] = jnp.zeros(l_scr.shape, F32)
    acc_scr[...] = jnp.zeros(acc_scr.shape, F32)


def _softmax_finish(gate_ref, o_ref, l_scr, acc_scr):
    for r in range(HEADS_PER_GROUP):
        ot = acc_scr[r] * (gate_ref[r:r + 1, :] / l_scr[r])
        o_ref[:, r * HEAD_DIM:(r + 1) * HEAD_DIM] = ot.T.astype(o_ref.dtype)


def _sel_attn_kernel(q_ref, k_ref, e_ref, vt_ref, sel_ref, bt_diag_ref, bt_prev_ref, gate_ref, o_ref,
                     qaug_scr, z_scr, zmax_scr, m_scr, l_scr, acc_scr):
    qi = pl.program_id(1)
    tk = ATTN_TILE
    _softmax_init(m_scr, l_scr, acc_scr)
    for r in range(HEADS_PER_GROUP):
        qaug_scr[r, 0:HEAD_DIM, :] = q_ref[r * HEAD_DIM:(r + 1) * HEAD_DIM, :]
        qaug_scr[r, HEAD_DIM:2 * HEAD_DIM, :] = sel_ref[0]

    def logits_with(bt_ref):
        def logits_head(r, kvj):
            rows = pl.ds(pl.multiple_of(kvj * tk, tk), tk)
            k_aug = jnp.concatenate([k_ref[rows, :], e_ref[rows, :]], axis=1)
            z = jnp.dot(k_aug, qaug_scr[r], preferred_element_type=F32)
            if bt_ref is not None:
                z = z + bt_ref[r, 0]
            z_scr[r] = z
            zmax_scr[r] = jnp.max(z, axis=0, keepdims=True)
        return logits_head

    logits_diag, logits_prev, logits_far = logits_with(bt_diag_ref), logits_with(bt_prev_ref), logits_with(None)

    def softmax_head(r, kvj):
        _softmax_step(r, z_scr[r], zmax_scr[r], vt_ref[0, 0, kvj], m_scr, l_scr, acc_scr)

    logits_diag(0, qi)
    _pipelined_tile(qi, jnp.maximum(qi - 1, 0), logits_diag, logits_prev, softmax_head)

    @pl.when(qi >= 1)
    def _():
        _pipelined_tile(qi - 1, 0, logits_prev, logits_far, softmax_head)

    n_far = jnp.maximum(qi - 1, 0)

    def far_tile(kvj, carry):
        _pipelined_tile(kvj, jnp.minimum(kvj + 1, n_far - 1), logits_far, logits_far, softmax_head)
        return carry

    lax.fori_loop(0, n_far, far_tile, 0)
    _softmax_finish(gate_ref, o_ref, l_scr, acc_scr)


def _win_attn_kernel(q_ref, k_ref, vt_ref, bt_diag_ref, bt_prev_ref, gate_ref, oc_ref, os_ref, za_ref, o_ref,
                     z_scr):
    qi = pl.program_id(1)
    tk = ATTN_TILE
    hk = tk // 2
    units = [(r, half) for r in range(HEADS_PER_GROUP) for half in range(2)]

    def run(key_row0, n_keys, bias, values_t):
        def logits(u):
            r, half = units[u]
            row0 = key_row0(half)
            row0 = row0 if isinstance(row0, int) else pl.multiple_of(row0, hk)
            k = k_ref[pl.ds(row0, n_keys(half)), :]
            qt = q_ref[r * HEAD_DIM:(r + 1) * HEAD_DIM, half * hk:(half + 1) * hk]
            z_scr[u, 0:n_keys(half), :] = jnp.dot(k, qt, preferred_element_type=F32)

        def softmax(u):
            r, half = units[u]
            rows = slice(half * hk, (half + 1) * hk)
            cols = slice(r * HEAD_DIM, (r + 1) * HEAD_DIM)
            z = z_scr[u, 0:n_keys(half), :] + bias(r, half)
            p = jnp.exp2(z - jnp.max(z, axis=0, keepdims=True))
            scale = gate_ref[r:r + 1, rows] / jnp.sum(p, axis=0, keepdims=True)
            o_win = (jnp.dot(values_t(half), p.astype(BF16), preferred_element_type=F32) * scale).T
            o_a = oc_ref[rows, cols].astype(F32) + os_ref[rows, cols].astype(F32) + o_win
            o_ref[rows, cols] = (o_a * za_ref[rows, cols].astype(F32)).astype(o_ref.dtype)

        logits(0)
        for u in range(len(units)):
            if u + 1 < len(units):
                logits(u + 1)
            softmax(u)

    @pl.when(qi >= 1)
    def _():
        def bias(r, half):
            lanes = slice(half * hk, (half + 1) * hk)
            if half == 0:
                parts = [bt_prev_ref[r, 0, 0:hk, lanes], bt_prev_ref[r, 0, hk:tk, lanes],
                         bt_diag_ref[r, 0, 0:hk, lanes]]
            else:
                parts = [bt_prev_ref[r, 0, hk:tk, lanes], bt_diag_ref[r, 0, 0:hk, lanes],
                         bt_diag_ref[r, 0, hk:tk, lanes]]
            return jnp.concatenate(parts, axis=0)

        def values_t(half):
            prev, diag = vt_ref[0, 0, qi - 1], vt_ref[0, 0, qi]
            if half == 0:
                return jnp.concatenate([prev, diag[:, 0:hk]], axis=1)
            return jnp.concatenate([prev[:, hk:tk], diag], axis=1)

        run(lambda half: (qi - 1) * tk + half * hk, lambda half: 3 * hk, bias, values_t)

    @pl.when(qi == 0)
    def _():
        def bias(r, half):
            return bt_diag_ref[r, 0, 0:(half + 1) * hk, half * hk:(half + 1) * hk]

        run(lambda half: 0, lambda half: (half + 1) * hk, bias, lambda half: vt_ref[0, 0, 0][:, 0:(half + 1) * hk])


def _attn_scratch(tq):
    return [pltpu.VMEM((HEADS_PER_GROUP, ATTN_TILE, tq), F32), pltpu.VMEM((HEADS_PER_GROUP, 1, tq), F32),
            pltpu.VMEM((HEADS_PER_GROUP, 1, tq), F32), pltpu.VMEM((HEADS_PER_GROUP, HEAD_DIM, tq), F32)]


def _attn_common_specs(t, k_set, branch):
    tq = ATTN_TILE
    q_spec = pl.BlockSpec((GROUP_Q_WIDTH, tq), lambda gi, qi: (gi, qi))
    k_spec = pl.BlockSpec((t, HEAD_DIM), lambda gi, qi: (0, k_set * A_KV_GROUPS + gi))
    vt_spec = pl.BlockSpec((1, 1, t // ATTN_TILE, HEAD_DIM, ATTN_TILE), lambda gi, qi: (k_set, gi, 0, 0, 0))
    out_spec = pl.BlockSpec((tq, GROUP_Q_WIDTH), lambda gi, qi: (qi, gi))
    return q_spec, k_spec, vt_spec, _gate_spec(branch, tq), out_spec


def _bt_spec(kind):
    return pl.BlockSpec((HEADS_PER_GROUP, 1, ATTN_TILE, ATTN_TILE), lambda gi, qi: (gi, kind, 0, 0))


def _attn_block_bytes(t):
    tq = ATTN_TILE
    return [tq * GROUP_Q_WIDTH * 2 * 2, t * HEAD_DIM * 2 * 2, 2 * HEADS_PER_GROUP * tq * tq * 4]


def _sel_attn(q2, k_plain, vt, sel_neg, bt, gates_t, t):
    tq = ATTN_TILE
    q_spec, k_spec, vt_spec, gate_spec, out_spec = _attn_common_specs(t, 0, 1)
    sel_spec = pl.BlockSpec((1, V7X_LANES, tq), lambda gi, qi: (gi, 0, qi))
    e_onehot = (jnp.arange(t, dtype=jnp.int32)[:, None] // L_SEL
                == jnp.arange(V7X_LANES, dtype=jnp.int32)[None, :]).astype(BF16)
    e_spec = pl.BlockSpec((t, V7X_LANES), lambda gi, qi: (0, 0))
    return pl.pallas_call(
        _sel_attn_kernel,
        grid=(A_KV_GROUPS, t // tq),
        in_specs=[q_spec, k_spec, e_spec, vt_spec, sel_spec, _bt_spec(0), _bt_spec(1), gate_spec],
        out_specs=out_spec,
        out_shape=jax.ShapeDtypeStruct((t, A_WIDTH), BF16),
        scratch_shapes=([pltpu.VMEM((HEADS_PER_GROUP, 2 * HEAD_DIM, tq), BF16)] + _attn_scratch(tq)[:1]
                        + [pltpu.VMEM((HEADS_PER_GROUP, 1, tq), F32)] + _attn_scratch(tq)[1:]),
        compiler_params=_params(("parallel", "arbitrary"),
                                _attn_block_bytes(t) + [t * V7X_LANES * 2, tq * V7X_LANES * 2],
                                (8 + HEADS_PER_GROUP) * tq * tq * 4),
        name="sel_attn",
    )(q2, k_plain, e_onehot, vt, sel_neg, bt, bt, gates_t)


def _win_attn(q2, k_plain, vt, bt, gates_t, o_cmp, o_sel, za, t):
    tq = ATTN_TILE
    q_spec, k_spec, vt_spec, gate_spec, out_spec = _attn_common_specs(t, 1, 2)
    return pl.pallas_call(
        _win_attn_kernel,
        grid=(A_KV_GROUPS, t // tq),
        in_specs=[q_spec, k_spec, vt_spec, _bt_spec(0), _bt_spec(2), gate_spec, out_spec, out_spec, out_spec],
        out_specs=out_spec,
        out_shape=jax.ShapeDtypeStruct((t, A_WIDTH), BF16),
        scratch_shapes=[pltpu.VMEM((2 * HEADS_PER_GROUP, 3 * tq // 2, tq // 2), F32)],
        compiler_params=_params(("parallel", "arbitrary"),
                                _attn_block_bytes(t) + [3 * tq * GROUP_Q_WIDTH * 2],
                                (8 + HEADS_PER_GROUP) * tq * tq * 4),
        name="win_attn",
    )(q2, k_plain, vt, bt, bt, gates_t, o_cmp, o_sel, za)


def _gmlp_kernel(u_ref, v_ref, z_ref, ng_ref, w_ref, bt_ref, o_ref):
    v = v_ref[...].astype(F32)
    ms = jnp.mean(v * v, axis=-1, keepdims=True)
    vn = (v * lax.rsqrt(ms + NORM_EPS) * ng_ref[...]).astype(BF16)
    p_idx = lax.broadcasted_iota(jnp.int32, (CHUNK, CHUNK), 0)
    q_idx = lax.broadcasted_iota(jnp.int32, (CHUNK, CHUNK), 1)
    causal = q_idx <= p_idx
    gd = B_WIDTH // B_GROUPS
    for gg in range(B_GROUPS):
        cols = slice(gg * gd, (gg + 1) * gd)
        w = jnp.where(causal, w_ref[gg], jnp.zeros((), w_ref.dtype))
        for c in range(GMLP_CHUNKS):
            rows = slice(c * CHUNK, (c + 1) * CHUNK)
            f = jnp.dot(w, vn[rows, cols], preferred_element_type=F32) + bt_ref[:, gg:gg + 1]
            o_ref[rows, cols] = (u_ref[rows, cols].astype(F32) * f
                                 * z_ref[rows, cols].astype(F32)).astype(o_ref.dtype)


def _gmlp(uv, zb, norm_g, w_s, b_t, t):
    bw = B_WIDTH
    rows = GMLP_CHUNKS * CHUNK

    def col_spec(col):
        return pl.BlockSpec((rows, bw), lambda i: (i, col))

    return pl.pallas_call(
        _gmlp_kernel,
        grid=(t // rows,),
        in_specs=[col_spec(0), col_spec(1), col_spec(0),
                  pl.BlockSpec((1, bw), lambda i: (0, 0)),
                  pl.BlockSpec((B_GROUPS, CHUNK, CHUNK), lambda i: (0, 0, 0)),
                  pl.BlockSpec((CHUNK, B_GROUPS), lambda i: (0, 0))],
        out_specs=pl.BlockSpec((rows, bw), lambda i: (i, 0)),
        out_shape=jax.ShapeDtypeStruct((t, bw), BF16),
        compiler_params=_params(("parallel",), [rows * bw * 2] * 4 + [B_GROUPS * CHUNK * CHUNK * 2],
                                4 * rows * bw * 4),
        name="gmlp",
    )(uv, uv, zb, norm_g.reshape(1, bw), w_s, b_t)


def _merge_kernel(oa_ref, ob_ref, wa_ref, wb_ref, ma_ref, mb_ref, o_ref):
    ya = _nn_dot(oa_ref[...], wa_ref[0])
    yb = _nn_dot(ob_ref[...], wb_ref[0])
    o_ref[...] = (ma_ref[...].astype(F32) * ya + mb_ref[...].astype(F32) * yb).astype(o_ref.dtype)


def _merge(o_a, o_b, m_gates, w_a, w_b, layer, t, d):
    tm = min(MM_TM, t)
    tn = min(W32_TN, d)
    aw = A_WIDTH
    bw = B_WIDTH
    return pl.pallas_call(
        _merge_kernel,
        grid=(d // tn, t // tm),
        in_specs=[pl.BlockSpec((tm, aw), lambda j, i: (i, 0)),
                  pl.BlockSpec((tm, bw), lambda j, i: (i, 0)),
                  pl.BlockSpec((1, aw, tn), lambda j, i: (layer, 0, j)),
                  pl.BlockSpec((1, bw, tn), lambda j, i: (layer, 0, j)),
                  pl.BlockSpec((tm, tn), lambda j, i: (i, j)),
                  pl.BlockSpec((tm, tn), lambda j, i: (i, d // tn + j))],
        out_specs=pl.BlockSpec((tm, tn), lambda j, i: (i, j)),
        out_shape=jax.ShapeDtypeStruct((t, d), BF16),
        compiler_params=_params(("arbitrary", "arbitrary"),
                                [tm * aw * 2] * 2 + [aw * tn * 4, bw * tn * 4] + [tm * tn * 2] * 3,
                                4 * tm * tn * 4),
        name="merge",
    )(o_a, o_b, w_a, w_b, m_gates, m_gates)


def _gate_weight_columns():
    src = np.full((V7X_LANES,), -1, np.int64)
    for br in range(3):
        for g in range(A_KV_GROUPS):
            for r in range(HEADS_PER_GROUP):
                src[(br * A_KV_GROUPS + g) * V7X_SUBLANES + r] = (g * HEADS_PER_GROUP + r) * 3 + br
    return src


def kernel(x, rel_bias, pre_norm, w_in, cmp_pe_k, cmp_w1_k, cmp_w2_k, cmp_pe_v, cmp_w1_v, cmp_w2_v,
           w_out_a, sgu_norm, sgu_w, sgu_b, w_out_b, w_out, post_norm):
    batch, t, d = x.shape
    assert batch == 1 and t % MM_TM == 0 and N_SEL <= t // L_SEL <= V7X_LANES
    depth = w_in.shape[0]
    g = A_KV_GROUPS
    xs = x.reshape(t, d)

    w_in_t = jnp.swapaxes(w_in, 1, 2)
    gate_src = _gate_weight_columns()
    w_gate = jnp.take(w_in_t[:, SRC_GATES:SRC_REST, :], jnp.asarray(np.maximum(gate_src, 0)), axis=1)
    w_gate = jnp.where(jnp.asarray(gate_src >= 0)[None, :, None], w_gate, 0.0).astype(BF16)
    w1 = jnp.stack([cmp_w1_k, cmp_w1_v], axis=1).astype(BF16)
    w2 = jnp.stack([cmp_w2_k, cmp_w2_v], axis=1).astype(BF16)
    slab = STRIDE_CMP * HEAD_DIM
    pe = jnp.stack([cmp_pe_k, cmp_pe_v], axis=1).reshape(depth, 2, 2, slab)
    w_s = sgu_w.astype(BF16)

    bias_tiles = _bias_tiles(rel_bias)
    bias_strip = _bias_strip(rel_bias)

    h = _rmsnorm(xs, pre_norm[0], BF16)
    for layer in range(depth):
        q2 = _q_proj(h, w_in_t, layer)
        x_slabs = _cmp_proj(h, w_in_t, layer)
        k_plain, vt = _kv_proj(h, w_in_t, layer, SRC_KV + 2 * A_KV_WIDTH, 2)
        gates_t = _gate_proj(h, w_gate, layer)
        za = _mm_w32(h, w_in_t, layer, SRC_REST + COL_ZA, A_WIDTH, BF16, "silu", "za_proj")
        uv = _mm_w32(h, w_in_t, layer, SRC_REST + COL_U, 2 * B_WIDTH, BF16, "gelu", "uv_proj")
        zb = _mm_w32(h, w_in_t, layer, SRC_REST + COL_ZB, B_WIDTH, BF16, "silu", "zb_proj")
        m_gates = _mm_w32(h, w_in_t, layer, SRC_REST + COL_MA, 2 * d, BF16, "sigmoid", "m_proj")

        kc, kct = _compress(x_slabs, pe[layer], w1[layer], w2[layer])
        o_cmp, sel_neg = _cmp_attn(q2, kc, kct, bias_strip, gates_t, t)
        o_sel = _sel_attn(q2, k_plain, vt, sel_neg, bias_tiles, gates_t, t)
        o_a = _win_attn(q2, k_plain, vt, bias_tiles, gates_t, o_cmp, o_sel, za, t)

        o_b = _gmlp(uv, zb, sgu_norm[layer], w_s[layer], sgu_b[layer].T, t)
        merged = _merge(o_a, o_b, m_gates, w_out_a, w_out_b, layer, t, d)
        y = _mm(merged, w_out, layer, BF16, name="out_proj")
        g_next = pre_norm[layer + 1] if layer + 1 < depth else pre_norm[layer]
        xs, h = _post_norm_residual(xs, y, post_norm[layer], g_next)
    del g
    return xs.reshape(batch, t, d)
```

```python
import functools
import math

import numpy as np
import jax
import jax.numpy as jnp
from jax import lax
from jax.experimental import pallas as pl
from jax.experimental.pallas import tpu as pltpu

F32 = jnp.float32
BF16 = jnp.bfloat16

A_HEADS = 16
A_KV_GROUPS = 4
HEADS_PER_GROUP = A_HEADS // A_KV_GROUPS
HEAD_DIM = 128
A_WIDTH = A_HEADS * HEAD_DIM
A_KV_WIDTH = A_KV_GROUPS * HEAD_DIM
GROUP_Q_WIDTH = HEADS_PER_GROUP * HEAD_DIM
L_CMP = 32
STRIDE_CMP = 16
L_SEL = 64
N_SEL = 16
WINDOW = 512
B_GROUPS = 16
CHUNK = 128
B_WIDTH = 2048
NUM_BUCKETS = 32
MAX_DISTANCE = 128
NORM_EPS = 1e-6
NEG_INF = -1e30
SCALE = HEAD_DIM ** -0.5
LOG2_E = math.log2(math.e)
SCALE_LOG2 = SCALE * LOG2_E
N_GATE_COLS = 3 * A_HEADS

V7X_LANES = 128
V7X_SUBLANES = 8
V7X_SCOPED_VMEM_CAP_BYTES = 60000 * 1024

ATTN_TILE = 512
CMP_TQ = 256
CMP_ROWS_PER_TQ = CMP_TQ // STRIDE_CMP
CMP_SUB = 2
MM_TM = 1024
MM_TN = 1024
W32_TN = 512
W32_TM = 2048
MM_ROW_CHUNK = 512
GMLP_CHUNKS = 4
NORM_ROWS = 512

SRC_KV = A_WIDTH
SRC_GATES = SRC_KV + 6 * A_KV_WIDTH
SRC_REST = SRC_GATES + N_GATE_COLS
COL_ZA = 0
COL_U = COL_ZA + A_WIDTH
COL_V = COL_U + B_WIDTH
COL_ZB = COL_V + B_WIDTH
COL_MA = COL_ZB + B_WIDTH


def _vmem_limit(block_bytes, temp_bytes=0):
    need = 2 * sum(block_bytes) + temp_bytes + (4 << 20)
    return int(min(max(need, 16 << 20), V7X_SCOPED_VMEM_CAP_BYTES))


def _params(sem, block_bytes, temp_bytes=0, flags=None):
    return pltpu.CompilerParams(dimension_semantics=sem,
                                vmem_limit_bytes=_vmem_limit(block_bytes, temp_bytes), flags=flags)


def _t5_thresholds():
    n = np.arange(0, 4 * MAX_DISTANCE)
    max_exact = NUM_BUCKETS // 2
    nf = np.maximum(n, 1).astype(np.float32)
    large = max_exact + (np.log(nf / np.float32(max_exact)) / np.float32(math.log(MAX_DISTANCE / max_exact))
                         * np.float32(NUM_BUCKETS - max_exact)).astype(np.int32)
    bucket = np.where(n < max_exact, n, np.minimum(large, NUM_BUCKETS - 1))
    assert np.all(np.diff(bucket) >= 0) and bucket[-1] == NUM_BUCKETS - 1
    return [int(np.argmax(bucket >= b)) for b in range(1, NUM_BUCKETS)]


_T5_THR = _t5_thresholds()


def _sigmoid(x):
    return 1.0 / (1.0 + jnp.exp(-x))


def _silu(x):
    return x * _sigmoid(x)


def _gelu(x):
    return jax.nn.gelu(x, approximate=True)


_ACTS = {"none": lambda v: v, "silu": _silu, "gelu": _gelu, "sigmoid": _sigmoid}


def _rmsnorm_kernel(x_ref, g_ref, o_ref):
    x = x_ref[...]
    ms = jnp.mean(x * x, axis=-1, keepdims=True)
    o_ref[...] = (x * lax.rsqrt(ms + NORM_EPS) * g_ref[...]).astype(o_ref.dtype)


def _rmsnorm(x, g, out_dtype):
    t, d = x.shape
    tm = min(NORM_ROWS, t)
    return pl.pallas_call(
        _rmsnorm_kernel,
        grid=(t // tm,),
        in_specs=[pl.BlockSpec((tm, d), lambda i: (i, 0)), pl.BlockSpec((1, d), lambda i: (0, 0))],
        out_specs=pl.BlockSpec((tm, d), lambda i: (i, 0)),
        out_shape=jax.ShapeDtypeStruct((t, d), out_dtype),
        compiler_params=_params(("parallel",), [tm * d * 4, tm * d * 4]),
        name="rmsnorm",
    )(x, g.reshape(1, d))


def _post_kernel(x_ref, y_ref, g_ref, gn_ref, o_ref, h_ref):
    y = y_ref[...].astype(F32)
    ms = jnp.mean(y * y, axis=-1, keepdims=True)
    x = x_ref[...] + y * lax.rsqrt(ms + NORM_EPS) * g_ref[...]
    o_ref[...] = x
    ms_x = jnp.mean(x * x, axis=-1, keepdims=True)
    h_ref[...] = (x * lax.rsqrt(ms_x + NORM_EPS) * gn_ref[...]).astype(h_ref.dtype)


def _post_norm_residual(x, y, g, g_next):
    t, d = x.shape
    tm = min(NORM_ROWS, t)
    row = pl.BlockSpec((tm, d), lambda i: (i, 0))
    vec = pl.BlockSpec((1, d), lambda i: (0, 0))
    return pl.pallas_call(
        _post_kernel,
        grid=(t // tm,),
        in_specs=[row, row, vec, vec],
        out_specs=[row, row],
        out_shape=[jax.ShapeDtypeStruct((t, d), F32), jax.ShapeDtypeStruct((t, d), BF16)],
        compiler_params=_params(("parallel",), [tm * d * 4] * 4),
        name="post_norm_residual",
    )(x, y, g.reshape(1, d), g_next.reshape(1, d))


def _nt_dot(a, b):
    return lax.dot_general(a, b, (((1,), (1,)), ((), ())), preferred_element_type=F32)


def _nn_dot(a, b):
    return lax.dot_general(a, b, (((1,), (0,)), ((), ())), preferred_element_type=F32)


def _mm_kernel(a_ref, b_ref, o_ref, *, act, nt, n_chunks):
    cw = o_ref.shape[1] // n_chunks
    tm = o_ref.shape[0]
    rh = MM_ROW_CHUNK if n_chunks > 1 and tm % MM_ROW_CHUNK == 0 else tm
    for rc in range(o_ref.shape[0] // rh):
        rows = slice(rc * rh, (rc + 1) * rh)
        a = a_ref[rows, :]
        for c in range(n_chunks):
            cols = slice(c * cw, (c + 1) * cw)
            if nt:
                acc = _nt_dot(a, b_ref[0, cols, :])
            else:
                acc = _nn_dot(a, b_ref[0, :, cols])
            o_ref[rows, cols] = _ACTS[act](acc).astype(o_ref.dtype)


def _mm(a, b, layer, out_dtype, name):
    m, k = a.shape
    n = b.shape[2]
    tm = min(W32_TM, m)
    tn = min(W32_TN, n)
    osz = jnp.dtype(out_dtype).itemsize
    return pl.pallas_call(
        functools.partial(_mm_kernel, act="none", nt=False, n_chunks=1),
        grid=(n // tn, m // tm),
        in_specs=[pl.BlockSpec((tm, k), lambda j, i: (i, 0)),
                  pl.BlockSpec((1, k, tn), lambda j, i: (layer, 0, j))],
        out_specs=pl.BlockSpec((tm, tn), lambda j, i: (i, j)),
        out_shape=jax.ShapeDtypeStruct((m, n), out_dtype),
        compiler_params=_params(("arbitrary", "arbitrary"), [tm * k * 2, k * tn * 4, tm * tn * osz],
                                2 * tm * tn * 4),
        name=name,
    )(a, b)


def _w_rows_spec(layer, row0, rows, k, split=None):
    assert row0 % V7X_SUBLANES == 0 and (split is None or split[1] % V7X_SUBLANES == 0)

    def index(j, i):
        row = row0 + j * rows
        if split is not None:
            row = jnp.where(j < split[0], row, split[1] + (j - split[0]) * rows)
        return layer, pl.multiple_of(row, V7X_SUBLANES), 0

    return pl.BlockSpec((pl.Element(1), pl.Element(rows), pl.Element(k)), index)


def _mm_w32(a, w_t, layer, row0, n, out_dtype, act, name, row1=None):
    m, k = a.shape
    tm = min(W32_TM, m)
    tn = next(c for c in (W32_TN, 256, 128) if n % c == 0)
    n_chunks = 1 if act == "none" else max(1, tn // 256)
    osz = jnp.dtype(out_dtype).itemsize
    split = None if row1 is None else (n // tn, row1)
    n = n if row1 is None else 2 * n
    return pl.pallas_call(
        functools.partial(_mm_kernel, act=act, nt=True, n_chunks=n_chunks),
        grid=(n // tn, m // tm),
        in_specs=[pl.BlockSpec((tm, k), lambda j, i: (i, 0)), _w_rows_spec(layer, row0, tn, k, split)],
        out_specs=pl.BlockSpec((tm, tn), lambda j, i: (i, j)),
        out_shape=jax.ShapeDtypeStruct((m, n), out_dtype),
        compiler_params=_params(("arbitrary", "arbitrary"), [tm * k * 2, k * tn * 4, tm * tn * osz],
                                2 * tm * tn * 4),
        name=name,
    )(a, w_t)


def _q_proj_kernel(a_ref, b_ref, o_ref):
    o_ref[...] = (_nt_dot(b_ref[0], a_ref[...]) * SCALE_LOG2).astype(o_ref.dtype)


def _q_proj(h, w_t, layer):
    t, k = h.shape
    n = A_WIDTH
    tm = min(W32_TM, t)
    tn = W32_TN
    return pl.pallas_call(
        _q_proj_kernel,
        grid=(n // tn, t // tm),
        in_specs=[pl.BlockSpec((tm, k), lambda j, i: (i, 0)), _w_rows_spec(layer, 0, tn, k)],
        out_specs=pl.BlockSpec((tn, tm), lambda j, i: (j, i)),
        out_shape=jax.ShapeDtypeStruct((n, t), BF16),
        compiler_params=_params(("arbitrary", "arbitrary"), [tm * k * 2, k * tn * 4, tm * tn * 2],
                                2 * tm * tn * 4),
        name="q_proj",
    )(h, w_t)


def _gate_proj_kernel(a_ref, b_ref, o_ref):
    o_ref[...] = _sigmoid(_nt_dot(b_ref[0], a_ref[...]))


def _gate_proj(h, w_gate, layer):
    t, k = h.shape
    tm = min(MM_TM, t)
    return pl.pallas_call(
        _gate_proj_kernel,
        grid=(t // tm,),
        in_specs=[pl.BlockSpec((tm, k), lambda i: (i, 0)),
                  pl.BlockSpec((1, V7X_LANES, k), lambda i: (layer, 0, 0))],
        out_specs=pl.BlockSpec((V7X_LANES, tm), lambda i: (0, i)),
        out_shape=jax.ShapeDtypeStruct((V7X_LANES, t), F32),
        compiler_params=_params(("parallel",), [tm * k * 2, k * V7X_LANES * 2, tm * V7X_LANES * 4]),
        name="gate_proj",
    )(h, w_gate)


def _cmp_proj_kernel(a_ref, b_ref, o_ref, scr):
    acc = _nt_dot(a_ref[...], b_ref[0])
    n_sg = scr.shape[0]
    slab_rows = o_ref.shape[1]
    for sg in range(n_sg):
        scr[sg] = acc[:, sg * HEAD_DIM:(sg + 1) * HEAD_DIM]
    for sg in range(n_sg):
        for l in range(STRIDE_CMP):
            o_ref[sg, :, l * HEAD_DIM:(l + 1) * HEAD_DIM] = (
                scr[sg, pl.ds(l, slab_rows, stride=STRIDE_CMP), :].astype(o_ref.dtype))


def _cmp_proj(h, w_t, layer):
    t, k = h.shape
    tm = min(MM_TM, t)
    g = A_KV_GROUPS
    slab = STRIDE_CMP * HEAD_DIM
    return pl.pallas_call(
        _cmp_proj_kernel,
        grid=(2, t // tm),
        in_specs=[pl.BlockSpec((tm, k), lambda j, i: (i, 0)), _w_rows_spec(layer, SRC_KV, A_KV_WIDTH, k)],
        out_specs=pl.BlockSpec((g, tm // STRIDE_CMP, slab), lambda j, i: (j, i, 0)),
        out_shape=jax.ShapeDtypeStruct((2 * g, t // STRIDE_CMP, slab), BF16),
        scratch_shapes=[pltpu.VMEM((g, tm, HEAD_DIM), F32)],
        compiler_params=_params(("arbitrary", "arbitrary"),
                                [tm * k * 2, k * A_KV_WIDTH * 4, tm * A_KV_WIDTH * 2], 3 * tm * A_KV_WIDTH * 4),
        name="cmp_proj",
    )(h, w_t)


def _kv_proj_kernel(a_ref, b_ref, k_ref, vt_ref):
    a = a_ref[...]
    k_ref[...] = _nt_dot(a, b_ref[0, :A_KV_WIDTH, :]).astype(k_ref.dtype)
    vt = _nt_dot(b_ref[0, A_KV_WIDTH:, :], a).astype(vt_ref.dtype)
    for g in range(A_KV_GROUPS):
        for s in range(vt_ref.shape[2]):
            vt_ref[0, g, s] = vt[g * HEAD_DIM:(g + 1) * HEAD_DIM, s * ATTN_TILE:(s + 1) * ATTN_TILE]


def _kv_proj(h, w_t, layer, row0, n_sets):
    t, k = h.shape
    tm = min(MM_TM // 2, t)
    tiles = tm // ATTN_TILE
    return pl.pallas_call(
        _kv_proj_kernel,
        grid=(n_sets, t // tm),
        in_specs=[pl.BlockSpec((tm, k), lambda j, i: (i, 0)), _w_rows_spec(layer, row0, 2 * A_KV_WIDTH, k)],
        out_specs=[pl.BlockSpec((tm, A_KV_WIDTH), lambda j, i: (i, j)),
                   pl.BlockSpec((1, A_KV_GROUPS, tiles, HEAD_DIM, ATTN_TILE), lambda j, i: (j, 0, i, 0, 0))],
        out_shape=[jax.ShapeDtypeStruct((t, n_sets * A_KV_WIDTH), BF16),
                   jax.ShapeDtypeStruct((n_sets, A_KV_GROUPS, t // ATTN_TILE, HEAD_DIM, ATTN_TILE), BF16)],
        compiler_params=_params(("arbitrary", "arbitrary"),
                                [tm * k * 2, k * 2 * A_KV_WIDTH * 4, tm * 2 * A_KV_WIDTH * 2],
                                3 * tm * 2 * A_KV_WIDTH * 4),
        name="kv_proj",
    )(h, w_t)


def _t5_bias_rel(dist, rb_ref, h):
    far = rb_ref[NUM_BUCKETS - 1, h]
    val = jnp.zeros(dist.shape, F32)
    for b in range(NUM_BUCKETS - 2, -1, -1):
        val = jnp.where(dist < _T5_THR[b], (rb_ref[b, h] - far) * LOG2_E, val)
    return val


def _bias_tile_kernel(rb_ref, o_ref):
    h = pl.program_id(0)
    sb = V7X_LANES
    key = lax.broadcasted_iota(jnp.int32, (sb, sb), 0)
    qry = lax.broadcasted_iota(jnp.int32, (sb, sb), 1)
    far_dist = _T5_THR[-1]
    for kind, (delta, limit) in enumerate(((0, None), (ATTN_TILE, None), (ATTN_TILE, WINDOW))):
        for bi in range(ATTN_TILE // sb):
            for bj in range(ATTN_TILE // sb):
                base = delta + sb * (bj - bi)
                lo, hi = base - (sb - 1), base + (sb - 1)
                if hi < 0 or (limit is not None and lo >= limit):
                    val = jnp.full((sb, sb), NEG_INF, F32)
                else:
                    dist = base + qry - key
                    val = _t5_bias_rel(dist, rb_ref, h) if lo < far_dist else jnp.zeros((sb, sb), F32)
                    if lo < 0:
                        val = jnp.where(dist >= 0, val, NEG_INF)
                    if limit is not None and hi >= limit:
                        val = jnp.where(dist < limit, val, NEG_INF)
                o_ref[0, kind, bi * sb:(bi + 1) * sb, bj * sb:(bj + 1) * sb] = val


def _bias_tiles(rel_bias):
    return pl.pallas_call(
        _bias_tile_kernel,
        grid=(A_HEADS,),
        in_specs=[pl.BlockSpec(memory_space=pltpu.SMEM)],
        out_specs=pl.BlockSpec((1, 3, ATTN_TILE, ATTN_TILE), lambda h: (h, 0, 0, 0)),
        out_shape=jax.ShapeDtypeStruct((A_HEADS, 3, ATTN_TILE, ATTN_TILE), F32),
        compiler_params=_params(("parallel",), [3 * ATTN_TILE * ATTN_TILE * 4]),
        name="bias_tiles",
    )(rel_bias)


def _bias_strip_kernel(rb_ref, o_ref):
    h = pl.program_id(0)
    shape = (2 * CMP_ROWS_PER_TQ, CMP_TQ)
    c = lax.broadcasted_iota(jnp.int32, shape, 0)
    a = lax.broadcasted_iota(jnp.int32, shape, 1)
    dist = a - STRIDE_CMP * c + (CMP_TQ - (L_CMP - 1))
    o_ref[0] = jnp.where(dist >= 0, _t5_bias_rel(dist, rb_ref, h), NEG_INF)


def _bias_strip(rel_bias):
    rows = 2 * CMP_ROWS_PER_TQ
    return pl.pallas_call(
        _bias_strip_kernel,
        grid=(A_HEADS,),
        in_specs=[pl.BlockSpec(memory_space=pltpu.SMEM)],
        out_specs=pl.BlockSpec((1, rows, CMP_TQ), lambda h: (h, 0, 0)),
        out_shape=jax.ShapeDtypeStruct((A_HEADS, rows, CMP_TQ), F32),
        compiler_params=_params(("parallel",), [rows * CMP_TQ * 4]),
        name="bias_strip",
    )(rel_bias)


def _compress_kernel(x_ref, pe_ref, w1_ref, w2_ref, o_ref, ot_ref):
    x = x_ref[0].astype(F32)
    half = x.shape[1]
    lo = (x + pe_ref[0, 0:1, :]).astype(BF16)
    hi = (x + pe_ref[0, 1:2, :]).astype(BF16)
    a = jnp.dot(lo, w1_ref[0, :half, :], preferred_element_type=F32)
    b = jnp.dot(hi, w1_ref[0, half:, :], preferred_element_type=F32)
    n_rows = x.shape[0]
    hidden = a + pltpu.roll(b, n_rows - 1, 0)
    out = jnp.dot(_silu(hidden).astype(BF16), w2_ref[0], preferred_element_type=F32)
    o_ref[0] = out.astype(o_ref.dtype)
    ot_ref[0] = out.T.astype(ot_ref.dtype)


def _compress(x_slabs, pe, w1, w2):
    ng, n_rows, half = x_slabs.shape
    g = A_KV_GROUPS
    return pl.pallas_call(
        _compress_kernel,
        grid=(2, g),
        in_specs=[pl.BlockSpec((1, n_rows, half), lambda s, i: (s * g + i, 0, 0)),
                  pl.BlockSpec((1, 2, half), lambda s, i: (s, 0, 0)),
                  pl.BlockSpec((1, 2 * half, HEAD_DIM), lambda s, i: (s, 0, 0)),
                  pl.BlockSpec((1, HEAD_DIM, HEAD_DIM), lambda s, i: (s, 0, 0))],
        out_specs=[pl.BlockSpec((1, n_rows, HEAD_DIM), lambda s, i: (s * g + i, 0, 0)),
                   pl.BlockSpec((1, HEAD_DIM, n_rows), lambda s, i: (s * g + i, 0, 0))],
        out_shape=[jax.ShapeDtypeStruct((ng, n_rows, HEAD_DIM), BF16),
                   jax.ShapeDtypeStruct((ng, HEAD_DIM, n_rows), BF16)],
        compiler_params=_params(("parallel", "parallel"),
                                [n_rows * half * 2, 2 * half * 4, 2 * half * HEAD_DIM * 2],
                                4 * n_rows * half * 4),
        name="compress",
    )(x_slabs, pe, w1, w2)


def _cmp_attn_kernel(q_ref, kc_ref, vct_ref, strip_ref, gate_ref, o_ref, sel_ref, s_scr, pg_scr):
    n_rows = kc_ref.shape[1]
    n_sel = n_rows // (L_SEL // STRIDE_CMP)
    tq = CMP_TQ
    rpt = CMP_ROWS_PER_TQ
    pad = V7X_SUBLANES
    lane_tiles = tq // V7X_LANES
    per_sel = L_SEL // STRIDE_CMP
    sub_qi = [pl.program_id(1) * CMP_SUB + sub for sub in range(CMP_SUB)]
    t_rows = [qi * tq + lax.broadcasted_iota(jnp.int32, (1, tq), 1) for qi in sub_qi]

    for sub, qi in enumerate(sub_qi):
        lanes = slice(sub * tq, (sub + 1) * tq)
        key_live = lax.broadcasted_iota(jnp.int32, (n_rows, tq), 0) < (qi + 1) * rpt
        strip_rows = pl.ds(pl.multiple_of(qi * rpt, rpt), 2 * rpt)
        for r in range(HEADS_PER_GROUP):
            st = jnp.dot(kc_ref[0], q_ref[r * HEAD_DIM:(r + 1) * HEAD_DIM, lanes], preferred_element_type=F32)
            s_scr[sub, r, 0:rpt, :] = jnp.zeros((rpt, tq), F32)
            s_scr[sub, r, rpt:rpt + n_rows, :] = jnp.where(key_live, st, NEG_INF)
            s_scr[sub, r, strip_rows, :] = s_scr[sub, r, strip_rows, :] + strip_ref[r]

    for sub in range(CMP_SUB):
        lanes = slice(sub * tq, (sub + 1) * tq)
        any_valid = jnp.where(t_rows[sub] >= L_CMP - 1, 1.0, 0.0)
        pg = jnp.zeros((n_rows, tq), F32)
        for r in range(HEADS_PER_GROUP):
            s = s_scr[sub, r, rpt:rpt + n_rows, :]
            m = jnp.max(s, axis=0, keepdims=True)
            e = jnp.exp2(s - m)
            p = e * (any_valid / jnp.sum(e, axis=0, keepdims=True))
            pg = pg + p
            ot = jnp.dot(vct_ref[0], p.astype(BF16), preferred_element_type=F32) * gate_ref[r:r + 1, lanes]
            o_ref[lanes, r * HEAD_DIM:(r + 1) * HEAD_DIM] = ot.T.astype(o_ref.dtype)
        for c in range(lane_tiles):
            pg_scr[sub, c, 0:pad, :] = jnp.zeros((pad, V7X_LANES), F32)
            pg_scr[sub, c, pad:pad + n_rows, :] = pg[:, c * V7X_LANES:(c + 1) * V7X_LANES]

    for sub in range(CMP_SUB):
        def rows(off, sub=sub):
            return jnp.concatenate(
                [pg_scr[sub, c, pl.ds(pad + off, n_sel, stride=per_sel), :] for c in range(lane_tiles)], axis=1)

        imp = (rows(-1) + rows(3)) + 2.0 * (rows(0) + rows(1) + rows(2))
        j = lax.broadcasted_iota(jnp.int32, (n_sel, tq), 0)
        jf = j.astype(F32)
        jt = jnp.right_shift(t_rows[sub], int(math.log2(L_SEL)))
        forced = (j == 0) | (j == jt) | (j == jt - 1)
        cand = (j >= 1) & (j <= jt - 2)
        score = jnp.where(cand, imp, -1.0)
        sel = jnp.where(forced, 1.0, 0.0)
        for _ in range(N_SEL - 3):
            mx = jnp.max(score, axis=0, keepdims=True)
            first_j = jnp.min(jnp.where(score == mx, jf, float(n_sel)), axis=0, keepdims=True)
            pick = (jf == first_j) & (mx >= 0.0)
            sel = jnp.where(pick, 1.0, sel)
            score = jnp.where(pick, -2.0, score)
        neg = jnp.where(sel > 0.5, 0.0, NEG_INF)
        if n_sel < V7X_LANES:
            neg = jnp.concatenate([neg, jnp.zeros((V7X_LANES - n_sel, tq), F32)], axis=0)
        sel_ref[0, :, sub * tq:(sub + 1) * tq] = neg.astype(sel_ref.dtype)


def _cmp_attn(q2, kc, vct, strip, gates_t, t):
    n_rows = kc.shape[1]
    g = A_KV_GROUPS
    tq = CMP_TQ
    tqs = CMP_SUB * tq
    return pl.pallas_call(
        _cmp_attn_kernel,
        grid=(g, t // tqs),
        in_specs=[pl.BlockSpec((GROUP_Q_WIDTH, tqs), lambda gi, qi: (gi, qi)),
                  pl.BlockSpec((1, n_rows, HEAD_DIM), lambda gi, qi: (gi, 0, 0)),
                  pl.BlockSpec((1, HEAD_DIM, n_rows), lambda gi, qi: (g + gi, 0, 0)),
                  pl.BlockSpec((HEADS_PER_GROUP, 2 * CMP_ROWS_PER_TQ, tq), lambda gi, qi: (gi, 0, 0)),
                  _gate_spec(0, tqs)],
        out_specs=[pl.BlockSpec((tqs, GROUP_Q_WIDTH), lambda gi, qi: (qi, gi)),
                   pl.BlockSpec((1, V7X_LANES, tqs), lambda gi, qi: (gi, 0, qi))],
        out_shape=[jax.ShapeDtypeStruct((t, A_WIDTH), BF16),
                   jax.ShapeDtypeStruct((g, V7X_LANES, t), BF16)],
        scratch_shapes=[pltpu.VMEM((CMP_SUB, HEADS_PER_GROUP, CMP_ROWS_PER_TQ + n_rows, tq), F32),
                        pltpu.VMEM((CMP_SUB, tq // V7X_LANES, n_rows + V7X_SUBLANES, V7X_LANES), F32)],
        compiler_params=_params(("parallel", "arbitrary"),
                                [tqs * GROUP_Q_WIDTH * 2 * 2, n_rows * HEAD_DIM * 2 * 2, tqs * V7X_LANES * 2],
                                CMP_SUB * 12 * n_rows * tq * 4),
        name="cmp_attn",
    )(q2, kc, vct, strip, gates_t)


def _gate_spec(branch, tq):
    return pl.BlockSpec((V7X_SUBLANES, tq), lambda gi, qi: (branch * A_KV_GROUPS + gi, qi))


def _softmax_step(r, z, z_max, vt, m_scr, l_scr, acc_scr):
    m_prev = m_scr[r]
    m_new = jnp.maximum(m_prev, z_max)
    p = jnp.exp2(z - m_new)
    alpha = jnp.exp2(m_prev - m_new)
    l_scr[r] = alpha * l_scr[r] + jnp.sum(p, axis=0, keepdims=True)
    acc_scr[r] = acc_scr[r] * alpha + jnp.dot(vt, p.astype(BF16), preferred_element_type=F32)
    m_scr[r] = m_new


def _pipelined_tile(kvj, kvj_next, logits_head, logits_next, softmax_head):
    last = HEADS_PER_GROUP - 1
    for r in range(HEADS_PER_GROUP):
        if r < last:
            logits_head(r + 1, kvj)
        else:
            logits_next(0, kvj_next)
        softmax_head(r, kvj)


def _softmax_init(m_scr, l_scr, acc_scr):
    m_scr[...] = jnp.full(m_scr.shape, NEG_INF, F32)
    l_scr[...] = jnp.zeros(l_scr.shape, F32)
    acc_scr[...] = jnp.zeros(acc_scr.shape, F32)


def _softmax_finish(gate_ref, o_ref, l_scr, acc_scr):
    for r in range(HEADS_PER_GROUP):
        ot = acc_scr[r] * (gate_ref[r:r + 1, :] / l_scr[r])
        o_ref[:, r * HEAD_DIM:(r + 1) * HEAD_DIM] = ot.T.astype(o_ref.dtype)


def _sel_attn_kernel(q_ref, k_ref, e_ref, vt_ref, sel_ref, bt_diag_ref, bt_prev_ref, gate_ref, o_ref,
                     qaug_scr, z_scr, zmax_scr, m_scr, l_scr, acc_scr):
    qi = pl.program_id(1)
    tk = ATTN_TILE
    _softmax_init(m_scr, l_scr, acc_scr)
    for r in range(HEADS_PER_GROUP):
        qaug_scr[r, 0:HEAD_DIM, :] = q_ref[r * HEAD_DIM:(r + 1) * HEAD_DIM, :]
        qaug_scr[r, HEAD_DIM:2 * HEAD_DIM, :] = sel_ref[0]

    def logits_with(bt_ref):
        def logits_head(r, kvj):
            rows = pl.ds(pl.multiple_of(kvj * tk, tk), tk)
            k_aug = jnp.concatenate([k_ref[rows, :], e_ref[rows, :]], axis=1)
            z = jnp.dot(k_aug, qaug_scr[r], preferred_element_type=F32)
            if bt_ref is not None:
                z = z + bt_ref[r, 0]
            z_scr[r] = z
            zmax_scr[r] = jnp.max(z, axis=0, keepdims=True)
        return logits_head

    logits_diag, logits_prev, logits_far = logits_with(bt_diag_ref), logits_with(bt_prev_ref), logits_with(None)

    def softmax_head(r, kvj):
        _softmax_step(r, z_scr[r], zmax_scr[r], vt_ref[0, 0, kvj], m_scr, l_scr, acc_scr)

    logits_diag(0, qi)
    _pipelined_tile(qi, jnp.maximum(qi - 1, 0), logits_diag, logits_prev, softmax_head)

    @pl.when(qi >= 1)
    def _():
        _pipelined_tile(qi - 1, 0, logits_prev, logits_far, softmax_head)

    n_far = jnp.maximum(qi - 1, 0)

    def far_tile(kvj, carry):
        _pipelined_tile(kvj, jnp.minimum(kvj + 1, n_far - 1), logits_far, logits_far, softmax_head)
        return carry

    lax.fori_loop(0, n_far, far_tile, 0)
    _softmax_finish(gate_ref, o_ref, l_scr, acc_scr)


def _win_attn_kernel(q_ref, k_ref, vt_ref, bt_diag_ref, bt_prev_ref, gate_ref, oc_ref, os_ref, za_ref, o_ref,
                     z_scr):
    qi = pl.program_id(1)
    tk = ATTN_TILE
    hk = tk // 2
    units = [(r, half) for r in range(HEADS_PER_GROUP) for half in range(2)]

    def run(key_row0, n_keys, bias, values_t):
        def logits(u):
            r, half = units[u]
            row0 = key_row0(half)
            row0 = row0 if isinstance(row0, int) else pl.multiple_of(row0, hk)
            k = k_ref[pl.ds(row0, n_keys(half)), :]
            qt = q_ref[r * HEAD_DIM:(r + 1) * HEAD_DIM, half * hk:(half + 1) * hk]
            z_scr[u, 0:n_keys(half), :] = jnp.dot(k, qt, preferred_element_type=F32)

        def softmax(u):
            r, half = units[u]
            rows = slice(half * hk, (half + 1) * hk)
            cols = slice(r * HEAD_DIM, (r + 1) * HEAD_DIM)
            z = z_scr[u, 0:n_keys(half), :] + bias(r, half)
            p = jnp.exp2(z - jnp.max(z, axis=0, keepdims=True))
            scale = gate_ref[r:r + 1, rows] / jnp.sum(p, axis=0, keepdims=True)
            o_win = (jnp.dot(values_t(half), p.astype(BF16), preferred_element_type=F32) * scale).T
            o_a = oc_ref[rows, cols].astype(F32) + os_ref[rows, cols].astype(F32) + o_win
            o_ref[rows, cols] = (o_a * za_ref[rows, cols].astype(F32)).astype(o_ref.dtype)

        logits(0)
        for u in range(len(units)):
            if u + 1 < len(units):
                logits(u + 1)
            softmax(u)

    @pl.when(qi >= 1)
    def _():
        def bias(r, half):
            lanes = slice(half * hk, (half + 1) * hk)
            if half == 0:
                parts = [bt_prev_ref[r, 0, 0:hk, lanes], bt_prev_ref[r, 0, hk:tk, lanes],
                         bt_diag_ref[r, 0, 0:hk, lanes]]
            else:
                parts = [bt_prev_ref[r, 0, hk:tk, lanes], bt_diag_ref[r, 0, 0:hk, lanes],
                         bt_diag_ref[r, 0, hk:tk, lanes]]
            return jnp.concatenate(parts, axis=0)

        def values_t(half):
            prev, diag = vt_ref[0, 0, qi - 1], vt_ref[0, 0, qi]
            if half == 0:
                return jnp.concatenate([prev, diag[:, 0:hk]], axis=1)
            return jnp.concatenate([prev[:, hk:tk], diag], axis=1)

        run(lambda half: (qi - 1) * tk + half * hk, lambda half: 3 * hk, bias, values_t)

    @pl.when(qi == 0)
    def _():
        def bias(r, half):
            return bt_diag_ref[r, 0, 0:(half + 1) * hk, half * hk:(half + 1) * hk]

        run(lambda half: 0, lambda half: (half + 1) * hk, bias, lambda half: vt_ref[0, 0, 0][:, 0:(half + 1) * hk])


def _attn_scratch(tq):
    return [pltpu.VMEM((HEADS_PER_GROUP, ATTN_TILE, tq), F32), pltpu.VMEM((HEADS_PER_GROUP, 1, tq), F32),
            pltpu.VMEM((HEADS_PER_GROUP, 1, tq), F32), pltpu.VMEM((HEADS_PER_GROUP, HEAD_DIM, tq), F32)]


def _attn_common_specs(t, k_set, branch):
    tq = ATTN_TILE
    q_spec = pl.BlockSpec((GROUP_Q_WIDTH, tq), lambda gi, qi: (gi, qi))
    k_spec = pl.BlockSpec((t, HEAD_DIM), lambda gi, qi: (0, k_set * A_KV_GROUPS + gi))
    vt_spec = pl.BlockSpec((1, 1, t // ATTN_TILE, HEAD_DIM, ATTN_TILE), lambda gi, qi: (k_set, gi, 0, 0, 0))
    out_spec = pl.BlockSpec((tq, GROUP_Q_WIDTH), lambda gi, qi: (qi, gi))
    return q_spec, k_spec, vt_spec, _gate_spec(branch, tq), out_spec


def _bt_spec(kind):
    return pl.BlockSpec((HEADS_PER_GROUP, 1, ATTN_TILE, ATTN_TILE), lambda gi, qi: (gi, kind, 0, 0))


def _attn_block_bytes(t):
    tq = ATTN_TILE
    return [tq * GROUP_Q_WIDTH * 2 * 2, t * HEAD_DIM * 2 * 2, 2 * HEADS_PER_GROUP * tq * tq * 4]


def _sel_attn(q2, k_plain, vt, sel_neg, bt, gates_t, t):
    tq = ATTN_TILE
    q_spec, k_spec, vt_spec, gate_spec, out_spec = _attn_common_specs(t, 0, 1)
    sel_spec = pl.BlockSpec((1, V7X_LANES, tq), lambda gi, qi: (gi, 0, qi))
    e_onehot = (jnp.arange(t, dtype=jnp.int32)[:, None] // L_SEL
                == jnp.arange(V7X_LANES, dtype=jnp.int32)[None, :]).astype(BF16)
    e_spec = pl.BlockSpec((t, V7X_LANES), lambda gi, qi: (0, 0))
    return pl.pallas_call(
        _sel_attn_kernel,
        grid=(A_KV_GROUPS, t // tq),
        in_specs=[q_spec, k_spec, e_spec, vt_spec, sel_spec, _bt_spec(0), _bt_spec(1), gate_spec],
        out_specs=out_spec,
        out_shape=jax.ShapeDtypeStruct((t, A_WIDTH), BF16),
        scratch_shapes=([pltpu.VMEM((HEADS_PER_GROUP, 2 * HEAD_DIM, tq), BF16)] + _attn_scratch(tq)[:1]
                        + [pltpu.VMEM((HEADS_PER_GROUP, 1, tq), F32)] + _attn_scratch(tq)[1:]),
        compiler_params=_params(("parallel", "arbitrary"),
                                _attn_block_bytes(t) + [t * V7X_LANES * 2, tq * V7X_LANES * 2],
                                (8 + HEADS_PER_GROUP) * tq * tq * 4),
        name="sel_attn",
    )(q2, k_plain, e_onehot, vt, sel_neg, bt, bt, gates_t)


def _win_attn(q2, k_plain, vt, bt, gates_t, o_cmp, o_sel, za, t):
    tq = ATTN_TILE
    q_spec, k_spec, vt_spec, gate_spec, out_spec = _attn_common_specs(t, 1, 2)
    return pl.pallas_call(
        _win_attn_kernel,
        grid=(A_KV_GROUPS, t // tq),
        in_specs=[q_spec, k_spec, vt_spec, _bt_spec(0), _bt_spec(2), gate_spec, out_spec, out_spec, out_spec],
        out_specs=out_spec,
        out_shape=jax.ShapeDtypeStruct((t, A_WIDTH), BF16),
        scratch_shapes=[pltpu.VMEM((2 * HEADS_PER_GROUP, 3 * tq // 2, tq // 2), F32)],
        compiler_params=_params(("parallel", "arbitrary"),
                                _attn_block_bytes(t) + [3 * tq * GROUP_Q_WIDTH * 2],
                                (8 + HEADS_PER_GROUP) * tq * tq * 4),
        name="win_attn",
    )(q2, k_plain, vt, bt, bt, gates_t, o_cmp, o_sel, za)


def _gmlp_kernel(u_ref, v_ref, z_ref, ng_ref, w_ref, bt_ref, o_ref):
    v = v_ref[...].astype(F32)
    ms = jnp.mean(v * v, axis=-1, keepdims=True)
    vn = (v * lax.rsqrt(ms + NORM_EPS) * ng_ref[...]).astype(BF16)
    p_idx = lax.broadcasted_iota(jnp.int32, (CHUNK, CHUNK), 0)
    q_idx = lax.broadcasted_iota(jnp.int32, (CHUNK, CHUNK), 1)
    causal = q_idx <= p_idx
    gd = B_WIDTH // B_GROUPS
    for gg in range(B_GROUPS):
        cols = slice(gg * gd, (gg + 1) * gd)
        w = jnp.where(causal, w_ref[gg], jnp.zeros((), w_ref.dtype))
        for c in range(GMLP_CHUNKS):
            rows = slice(c * CHUNK, (c + 1) * CHUNK)
            f = jnp.dot(w, vn[rows, cols], preferred_element_type=F32) + bt_ref[:, gg:gg + 1]
            o_ref[rows, cols] = (u_ref[rows, cols].astype(F32) * f
                                 * z_ref[rows, cols].astype(F32)).astype(o_ref.dtype)


def _gmlp(uv, zab, norm_g, w_s, b_t, t):
    assert A_WIDTH == B_WIDTH
    bw = B_WIDTH
    rows = GMLP_CHUNKS * CHUNK

    def col_spec(col):
        return pl.BlockSpec((rows, bw), lambda i: (i, col))

    return pl.pallas_call(
        _gmlp_kernel,
        grid=(t // rows,),
        in_specs=[col_spec(0), col_spec(1), col_spec(1),
                  pl.BlockSpec((1, bw), lambda i: (0, 0)),
                  pl.BlockSpec((B_GROUPS, CHUNK, CHUNK), lambda i: (0, 0, 0)),
                  pl.BlockSpec((CHUNK, B_GROUPS), lambda i: (0, 0))],
        out_specs=pl.BlockSpec((rows, bw), lambda i: (i, 0)),
        out_shape=jax.ShapeDtypeStruct((t, bw), BF16),
        compiler_params=_params(("parallel",), [rows * bw * 2] * 4 + [B_GROUPS * CHUNK * CHUNK * 2],
                                4 * rows * bw * 4),
        name="gmlp",
    )(uv, uv, zab, norm_g.reshape(1, bw), w_s, b_t)


def _merge_kernel(oa_ref, ob_ref, wa_ref, wb_ref, ma_ref, mb_ref, o_ref):
    ya = _nn_dot(oa_ref[...], wa_ref[0])
    yb = _nn_dot(ob_ref[...], wb_ref[0])
    o_ref[...] = (ma_ref[...].astype(F32) * ya + mb_ref[...].astype(F32) * yb).astype(o_ref.dtype)


def _merge(o_a, o_b, m_gates, w_a, w_b, layer, t, d):
    tm = min(MM_TM, t)
    tn = min(W32_TN, d)
    aw = A_WIDTH
    bw = B_WIDTH
    return pl.pallas_call(
        _merge_kernel,
        grid=(d // tn, t // tm),
        in_specs=[pl.BlockSpec((tm, aw), lambda j, i: (i, 0)),
                  pl.BlockSpec((tm, bw), lambda j, i: (i, 0)),
                  pl.BlockSpec((1, aw, tn), lambda j, i: (layer, 0, j)),
                  pl.BlockSpec((1, bw, tn), lambda j, i: (layer, 0, j)),
                  pl.BlockSpec((tm, tn), lambda j, i: (i, j)),
                  pl.BlockSpec((tm, tn), lambda j, i: (i, d // tn + j))],
        out_specs=pl.BlockSpec((tm, tn), lambda j, i: (i, j)),
        out_shape=jax.ShapeDtypeStruct((t, d), BF16),
        compiler_params=_params(("arbitrary", "arbitrary"),
                                [tm * aw * 2] * 2 + [aw * tn * 4, bw * tn * 4] + [tm * tn * 2] * 3,
                                4 * tm * tn * 4),
        name="merge",
    )(o_a, o_b, w_a, w_b, m_gates, m_gates)


def _gate_weight_columns():
    src = np.full((V7X_LANES,), -1, np.int64)
    for br in range(3):
        for g in range(A_KV_GROUPS):
            for r in range(HEADS_PER_GROUP):
                src[(br * A_KV_GROUPS + g) * V7X_SUBLANES + r] = (g * HEADS_PER_GROUP + r) * 3 + br
    return src


def kernel(x, rel_bias, pre_norm, w_in, cmp_pe_k, cmp_w1_k, cmp_w2_k, cmp_pe_v, cmp_w1_v, cmp_w2_v,
           w_out_a, sgu_norm, sgu_w, sgu_b, w_out_b, w_out, post_norm):
    batch, t, d = x.shape
    assert batch == 1 and t % MM_TM == 0 and N_SEL <= t // L_SEL <= V7X_LANES
    depth = w_in.shape[0]
    g = A_KV_GROUPS
    xs = x.reshape(t, d)

    w_in_t = jnp.swapaxes(w_in, 1, 2)
    gate_src = _gate_weight_columns()
    w_gate = jnp.take(w_in_t[:, SRC_GATES:SRC_REST, :], jnp.asarray(np.maximum(gate_src, 0)), axis=1)
    w_gate = jnp.where(jnp.asarray(gate_src >= 0)[None, :, None], w_gate, 0.0).astype(BF16)
    w1 = jnp.stack([cmp_w1_k, cmp_w1_v], axis=1).astype(BF16)
    w2 = jnp.stack([cmp_w2_k, cmp_w2_v], axis=1).astype(BF16)
    slab = STRIDE_CMP * HEAD_DIM
    pe = jnp.stack([cmp_pe_k, cmp_pe_v], axis=1).reshape(depth, 2, 2, slab)
    w_s = sgu_w.astype(BF16)

    bias_tiles = _bias_tiles(rel_bias)
    bias_strip = _bias_strip(rel_bias)

    h = _rmsnorm(xs, pre_norm[0], BF16)
    for layer in range(depth):
        q2 = _q_proj(h, w_in_t, layer)
        x_slabs = _cmp_proj(h, w_in_t, layer)
        k_plain, vt = _kv_proj(h, w_in_t, layer, SRC_KV + 2 * A_KV_WIDTH, 2)
        gates_t = _gate_proj(h, w_gate, layer)
        zab = _mm_w32(h, w_in_t, layer, SRC_REST + COL_ZA, A_WIDTH, BF16, "silu", "z_proj",
                      row1=SRC_REST + COL_ZB)
        uv = _mm_w32(h, w_in_t, layer, SRC_REST + COL_U, 2 * B_WIDTH, BF16, "gelu", "uv_proj")
        m_gates = _mm_w32(h, w_in_t, layer, SRC_REST + COL_MA, 2 * d, BF16, "sigmoid", "m_proj")

        kc, kct = _compress(x_slabs, pe[layer], w1[layer], w2[layer])
        o_cmp, sel_neg = _cmp_attn(q2, kc, kct, bias_strip, gates_t, t)
        o_sel = _sel_attn(q2, k_plain, vt, sel_neg, bias_tiles, gates_t, t)
        o_a = _win_attn(q2, k_plain, vt, bias_tiles, gates_t, o_cmp, o_sel, zab, t)

        o_b = _gmlp(uv, zab, sgu_norm[layer], w_s[layer], sgu_b[layer].T, t)
        merged = _merge(o_a, o_b, m_gates, w_out_a, w_out_b, layer, t, d)
        y = _mm(merged, w_out, layer, BF16, name="out_proj")
        g_next = pre_norm[layer + 1] if layer + 1 < depth else pre_norm[layer]
        xs, h = _post_norm_residual(xs, y, post_norm[layer], g_next)
    del g
    return xs.reshape(batch, t, d)
```

```python
import functools
import math

import numpy as np
import jax
import jax.numpy as jnp
from jax import lax
from jax.experimental import pallas as pl
from jax.experimental.pallas import tpu as pltpu

F32 = jnp.float32
BF16 = jnp.bfloat16

A_HEADS = 16
A_KV_GROUPS = 4
HEADS_PER_GROUP = A_HEADS // A_KV_GROUPS
HEAD_DIM = 128
A_WIDTH = A_HEADS * HEAD_DIM
A_KV_WIDTH = A_KV_GROUPS * HEAD_DIM
GROUP_Q_WIDTH = HEADS_PER_GROUP * HEAD_DIM
L_CMP = 32
STRIDE_CMP = 16
L_SEL = 64
N_SEL = 16
WINDOW = 512
B_GROUPS = 16
CHUNK = 128
B_WIDTH = 2048
NUM_BUCKETS = 32
MAX_DISTANCE = 128
NORM_EPS = 1e-6
NEG_INF = -1e30
SCALE = HEAD_DIM ** -0.5
LOG2_E = math.log2(math.e)
SCALE_LOG2 = SCALE * LOG2_E
N_GATE_COLS = 3 * A_HEADS

V7X_LANES = 128
V7X_SUBLANES = 8
V7X_SCOPED_VMEM_CAP_BYTES = 60000 * 1024

ATTN_TILE = 512
CMP_TQ = 256
CMP_ROWS_PER_TQ = CMP_TQ // STRIDE_CMP
CMP_SUB = 2
MM_TM = 1024
MM_TN = 1024
W32_TN = 512
W32_TM = 2048
MM_ROW_CHUNK = 512
GMLP_CHUNKS = 4
NORM_ROWS = 512

SRC_KV = A_WIDTH
SRC_GATES = SRC_KV + 6 * A_KV_WIDTH
SRC_REST = SRC_GATES + N_GATE_COLS
COL_ZA = 0
COL_U = COL_ZA + A_WIDTH
COL_V = COL_U + B_WIDTH
COL_ZB = COL_V + B_WIDTH
COL_MA = COL_ZB + B_WIDTH


def _vmem_limit(block_bytes, temp_bytes=0):
    need = 2 * sum(block_bytes) + temp_bytes + (4 << 20)
    return int(min(max(need, 16 << 20), V7X_SCOPED_VMEM_CAP_BYTES))


def _params(sem, block_bytes, temp_bytes=0, flags=None):
    return pltpu.CompilerParams(dimension_semantics=sem,
                                vmem_limit_bytes=_vmem_limit(block_bytes, temp_bytes), flags=flags)


def _t5_thresholds():
    n = np.arange(0, 4 * MAX_DISTANCE)
    max_exact = NUM_BUCKETS // 2
    nf = np.maximum(n, 1).astype(np.float32)
    large = max_exact + (np.log(nf / np.float32(max_exact)) / np.float32(math.log(MAX_DISTANCE / max_exact))
                         * np.float32(NUM_BUCKETS - max_exact)).astype(np.int32)
    bucket = np.where(n < max_exact, n, np.minimum(large, NUM_BUCKETS - 1))
    assert np.all(np.diff(bucket) >= 0) and bucket[-1] == NUM_BUCKETS - 1
    return [int(np.argmax(bucket >= b)) for b in range(1, NUM_BUCKETS)]


_T5_THR = _t5_thresholds()


def _sigmoid(x):
    return 1.0 / (1.0 + jnp.exp(-x))


def _silu(x):
    return x * _sigmoid(x)


def _gelu(x):
    return jax.nn.gelu(x, approximate=True)


_ACTS = {"none": lambda v: v, "silu": _silu, "gelu": _gelu, "sigmoid": _sigmoid}


def _rmsnorm_kernel(x_ref, g_ref, o_ref):
    x = x_ref[...]
    ms = jnp.mean(x * x, axis=-1, keepdims=True)
    o_ref[...] = (x * lax.rsqrt(ms + NORM_EPS) * g_ref[...]).astype(o_ref.dtype)


def _rmsnorm(x, g, out_dtype):
    t, d = x.shape
    tm = min(NORM_ROWS, t)
    return pl.pallas_call(
        _rmsnorm_kernel,
        grid=(t // tm,),
        in_specs=[pl.BlockSpec((tm, d), lambda i: (i, 0)), pl.BlockSpec((1, d), lambda i: (0, 0))],
        out_specs=pl.BlockSpec((tm, d), lambda i: (i, 0)),
        out_shape=jax.ShapeDtypeStruct((t, d), out_dtype),
        compiler_params=_params(("parallel",), [tm * d * 4, tm * d * 4]),
        name="rmsnorm",
    )(x, g.reshape(1, d))


def _post_kernel(x_ref, y_ref, g_ref, gn_ref, o_ref, h_ref):
    y = y_ref[...].astype(F32)
    ms = jnp.mean(y * y, axis=-1, keepdims=True)
    x = x_ref[...] + y * lax.rsqrt(ms + NORM_EPS) * g_ref[...]
    o_ref[...] = x
    ms_x = jnp.mean(x * x, axis=-1, keepdims=True)
    h_ref[...] = (x * lax.rsqrt(ms_x + NORM_EPS) * gn_ref[...]).astype(h_ref.dtype)


def _post_norm_residual(x, y, g, g_next):
    t, d = x.shape
    tm = min(NORM_ROWS, t)
    row = pl.BlockSpec((tm, d), lambda i: (i, 0))
    vec = pl.BlockSpec((1, d), lambda i: (0, 0))
    return pl.pallas_call(
        _post_kernel,
        grid=(t // tm,),
        in_specs=[row, row, vec, vec],
        out_specs=[row, row],
        out_shape=[jax.ShapeDtypeStruct((t, d), F32), jax.ShapeDtypeStruct((t, d), BF16)],
        compiler_params=_params(("parallel",), [tm * d * 4] * 4),
        name="post_norm_residual",
    )(x, y, g.reshape(1, d), g_next.reshape(1, d))


def _nt_dot(a, b):
    return lax.dot_general(a, b, (((1,), (1,)), ((), ())), preferred_element_type=F32)


def _nn_dot(a, b):
    return lax.dot_general(a, b, (((1,), (0,)), ((), ())), preferred_element_type=F32)


def _mm_kernel(a_ref, b_ref, o_ref, *, act, nt, n_chunks):
    cw = o_ref.shape[1] // n_chunks
    tm = o_ref.shape[0]
    rh = MM_ROW_CHUNK if n_chunks > 1 and tm % MM_ROW_CHUNK == 0 else tm
    for rc in range(o_ref.shape[0] // rh):
        rows = slice(rc * rh, (rc + 1) * rh)
        a = a_ref[rows, :]
        for c in range(n_chunks):
            cols = slice(c * cw, (c + 1) * cw)
            if nt:
                acc = _nt_dot(a, b_ref[0, cols, :])
            else:
                acc = _nn_dot(a, b_ref[0, :, cols])
            o_ref[rows, cols] = _ACTS[act](acc).astype(o_ref.dtype)


def _mm(a, b, layer, out_dtype, name):
    m, k = a.shape
    n = b.shape[2]
    tm = min(W32_TM, m)
    tn = min(W32_TN, n)
    osz = jnp.dtype(out_dtype).itemsize
    return pl.pallas_call(
        functools.partial(_mm_kernel, act="none", nt=False, n_chunks=1),
        grid=(n // tn, m // tm),
        in_specs=[pl.BlockSpec((tm, k), lambda j, i: (i, 0)),
                  pl.BlockSpec((1, k, tn), lambda j, i: (layer, 0, j))],
        out_specs=pl.BlockSpec((tm, tn), lambda j, i: (i, j)),
        out_shape=jax.ShapeDtypeStruct((m, n), out_dtype),
        compiler_params=_params(("arbitrary", "arbitrary"), [tm * k * 2, k * tn * 4, tm * tn * osz],
                                2 * tm * tn * 4),
        name=name,
    )(a, b)


def _w_rows_spec(layer, row0, rows, k, split=None):
    assert row0 % V7X_SUBLANES == 0 and (split is None or split[1] % V7X_SUBLANES == 0)

    def index(j, i):
        row = row0 + j * rows
        if split is not None:
            row = jnp.where(j < split[0], row, split[1] + (j - split[0]) * rows)
        return layer, pl.multiple_of(row, V7X_SUBLANES), 0

    return pl.BlockSpec((pl.Element(1), pl.Element(rows), pl.Element(k)), index)


def _mm_w32(a, w_t, layer, row0, n, out_dtype, act, name, row1=None):
    m, k = a.shape
    tm = min(W32_TM, m)
    tn = next(c for c in (W32_TN, 256, 128) if n % c == 0)
    n_chunks = 1 if act == "none" else max(1, tn // 256)
    osz = jnp.dtype(out_dtype).itemsize
    split = None if row1 is None else (n // tn, row1)
    n = n if row1 is None else 2 * n
    return pl.pallas_call(
        functools.partial(_mm_kernel, act=act, nt=True, n_chunks=n_chunks),
        grid=(n // tn, m // tm),
        in_specs=[pl.BlockSpec((tm, k), lambda j, i: (i, 0)), _w_rows_spec(layer, row0, tn, k, split)],
        out_specs=pl.BlockSpec((tm, tn), lambda j, i: (i, j)),
        out_shape=jax.ShapeDtypeStruct((m, n), out_dtype),
        compiler_params=_params(("arbitrary", "arbitrary"), [tm * k * 2, k * tn * 4, tm * tn * osz],
                                2 * tm * tn * 4),
        name=name,
    )(a, w_t)


def _q_proj_kernel(a_ref, b_ref, o_ref):
    o_ref[...] = (_nt_dot(b_ref[0], a_ref[...]) * SCALE_LOG2).astype(o_ref.dtype)


def _q_proj(h, w_t, layer):
    t, k = h.shape
    n = A_WIDTH
    tm = min(W32_TM, t)
    tn = W32_TN
    return pl.pallas_call(
        _q_proj_kernel,
        grid=(n // tn, t // tm),
        in_specs=[pl.BlockSpec((tm, k), lambda j, i: (i, 0)), _w_rows_spec(layer, 0, tn, k)],
        out_specs=pl.BlockSpec((tn, tm), lambda j, i: (j, i)),
        out_shape=jax.ShapeDtypeStruct((n, t), BF16),
        compiler_params=_params(("arbitrary", "arbitrary"), [tm * k * 2, k * tn * 4, tm * tn * 2],
                                2 * tm * tn * 4),
        name="q_proj",
    )(h, w_t)


def _gate_proj_kernel(a_ref, b_ref, o_ref):
    o_ref[...] = _sigmoid(_nt_dot(b_ref[0], a_ref[...]))


def _gate_proj(h, w_gate, layer):
    t, k = h.shape
    tm = min(MM_TM, t)
    return pl.pallas_call(
        _gate_proj_kernel,
        grid=(t // tm,),
        in_specs=[pl.BlockSpec((tm, k), lambda i: (i, 0)),
                  pl.BlockSpec((1, V7X_LANES, k), lambda i: (layer, 0, 0))],
        out_specs=pl.BlockSpec((V7X_LANES, tm), lambda i: (0, i)),
        out_shape=jax.ShapeDtypeStruct((V7X_LANES, t), F32),
        compiler_params=_params(("parallel",), [tm * k * 2, k * V7X_LANES * 2, tm * V7X_LANES * 4]),
        name="gate_proj",
    )(h, w_gate)


def _cmp_proj_kernel(a_ref, b_ref, o_ref, scr):
    n_sg = scr.shape[0]
    tm = a_ref.shape[0]
    halves = 2 if tm % (2 * STRIDE_CMP * V7X_SUBLANES * 2) == 0 else 1
    hr = tm // halves
    hs = hr // STRIDE_CMP
    for half in range(halves):
        acc = _nt_dot(a_ref[half * hr:(half + 1) * hr, :], b_ref[0])
        for sg in range(n_sg):
            scr[sg, half * hr:(half + 1) * hr, :] = acc[:, sg * HEAD_DIM:(sg + 1) * HEAD_DIM]
        for sg in range(n_sg):
            for l in range(STRIDE_CMP):
                o_ref[sg, half * hs:(half + 1) * hs, l * HEAD_DIM:(l + 1) * HEAD_DIM] = (
                    scr[sg, pl.ds(half * hr + l, hs, stride=STRIDE_CMP), :].astype(o_ref.dtype))


def _cmp_proj(h, w_t, layer):
    t, k = h.shape
    tm = min(MM_TM, t)
    g = A_KV_GROUPS
    slab = STRIDE_CMP * HEAD_DIM
    return pl.pallas_call(
        _cmp_proj_kernel,
        grid=(2, t // tm),
        in_specs=[pl.BlockSpec((tm, k), lambda j, i: (i, 0)), _w_rows_spec(layer, SRC_KV, A_KV_WIDTH, k)],
        out_specs=pl.BlockSpec((g, tm // STRIDE_CMP, slab), lambda j, i: (j, i, 0)),
        out_shape=jax.ShapeDtypeStruct((2 * g, t // STRIDE_CMP, slab), BF16),
        scratch_shapes=[pltpu.VMEM((g, tm, HEAD_DIM), F32)],
        compiler_params=_params(("arbitrary", "arbitrary"),
                                [tm * k * 2, k * A_KV_WIDTH * 4, tm * A_KV_WIDTH * 2], 3 * tm * A_KV_WIDTH * 4),
        name="cmp_proj",
    )(h, w_t)


def _kv_proj_kernel(a_ref, b_ref, k_ref, vt_ref):
    a = a_ref[...]
    k_ref[...] = _nt_dot(a, b_ref[0, :A_KV_WIDTH, :]).astype(k_ref.dtype)
    vt = _nt_dot(b_ref[0, A_KV_WIDTH:, :], a).astype(vt_ref.dtype)
    for g in range(A_KV_GROUPS):
        for s in range(vt_ref.shape[2]):
            vt_ref[0, g, s] = vt[g * HEAD_DIM:(g + 1) * HEAD_DIM, s * ATTN_TILE:(s + 1) * ATTN_TILE]


def _kv_proj(h, w_t, layer, row0, n_sets):
    t, k = h.shape
    tm = min(MM_TM // 2, t)
    tiles = tm // ATTN_TILE
    return pl.pallas_call(
        _kv_proj_kernel,
        grid=(n_sets, t // tm),
        in_specs=[pl.BlockSpec((tm, k), lambda j, i: (i, 0)), _w_rows_spec(layer, row0, 2 * A_KV_WIDTH, k)],
        out_specs=[pl.BlockSpec((tm, A_KV_WIDTH), lambda j, i: (i, j)),
                   pl.BlockSpec((1, A_KV_GROUPS, tiles, HEAD_DIM, ATTN_TILE), lambda j, i: (j, 0, i, 0, 0))],
        out_shape=[jax.ShapeDtypeStruct((t, n_sets * A_KV_WIDTH), BF16),
                   jax.ShapeDtypeStruct((n_sets, A_KV_GROUPS, t // ATTN_TILE, HEAD_DIM, ATTN_TILE), BF16)],
        compiler_params=_params(("arbitrary", "arbitrary"),
                                [tm * k * 2, k * 2 * A_KV_WIDTH * 4, tm * 2 * A_KV_WIDTH * 2],
                                3 * tm * 2 * A_KV_WIDTH * 4),
        name="kv_proj",
    )(h, w_t)


def _t5_bias_rel(dist, rb_ref, h):
    far = rb_ref[NUM_BUCKETS - 1, h]
    val = jnp.zeros(dist.shape, F32)
    for b in range(NUM_BUCKETS - 2, -1, -1):
        val = jnp.where(dist < _T5_THR[b], (rb_ref[b, h] - far) * LOG2_E, val)
    return val


def _bias_tile_kernel(rb_ref, o_ref):
    h = pl.program_id(0)
    sb = V7X_LANES
    key = lax.broadcasted_iota(jnp.int32, (sb, sb), 0)
    qry = lax.broadcasted_iota(jnp.int32, (sb, sb), 1)
    far_dist = _T5_THR[-1]
    for kind, (delta, limit) in enumerate(((0, None), (ATTN_TILE, None), (ATTN_TILE, WINDOW))):
        for bi in range(ATTN_TILE // sb):
            for bj in range(ATTN_TILE // sb):
                base = delta + sb * (bj - bi)
                lo, hi = base - (sb - 1), base + (sb - 1)
                if hi < 0 or (limit is not None and lo >= limit):
                    val = jnp.full((sb, sb), NEG_INF, F32)
                else:
                    dist = base + qry - key
                    val = _t5_bias_rel(dist, rb_ref, h) if lo < far_dist else jnp.zeros((sb, sb), F32)
                    if lo < 0:
                        val = jnp.where(dist >= 0, val, NEG_INF)
                    if limit is not None and hi >= limit:
                        val = jnp.where(dist < limit, val, NEG_INF)
                o_ref[0, kind, bi * sb:(bi + 1) * sb, bj * sb:(bj + 1) * sb] = val


def _bias_tiles(rel_bias):
    return pl.pallas_call(
        _bias_tile_kernel,
        grid=(A_HEADS,),
        in_specs=[pl.BlockSpec(memory_space=pltpu.SMEM)],
        out_specs=pl.BlockSpec((1, 3, ATTN_TILE, ATTN_TILE), lambda h: (h, 0, 0, 0)),
        out_shape=jax.ShapeDtypeStruct((A_HEADS, 3, ATTN_TILE, ATTN_TILE), F32),
        compiler_params=_params(("parallel",), [3 * ATTN_TILE * ATTN_TILE * 4]),
        name="bias_tiles",
    )(rel_bias)


def _bias_strip_kernel(rb_ref, o_ref):
    h = pl.program_id(0)
    shape = (2 * CMP_ROWS_PER_TQ, CMP_TQ)
    c = lax.broadcasted_iota(jnp.int32, shape, 0)
    a = lax.broadcasted_iota(jnp.int32, shape, 1)
    dist = a - STRIDE_CMP * c + (CMP_TQ - (L_CMP - 1))
    o_ref[0] = jnp.where(dist >= 0, _t5_bias_rel(dist, rb_ref, h), NEG_INF)


def _bias_strip(rel_bias):
    rows = 2 * CMP_ROWS_PER_TQ
    return pl.pallas_call(
        _bias_strip_kernel,
        grid=(A_HEADS,),
        in_specs=[pl.BlockSpec(memory_space=pltpu.SMEM)],
        out_specs=pl.BlockSpec((1, rows, CMP_TQ), lambda h: (h, 0, 0)),
        out_shape=jax.ShapeDtypeStruct((A_HEADS, rows, CMP_TQ), F32),
        compiler_params=_params(("parallel",), [rows * CMP_TQ * 4]),
        name="bias_strip",
    )(rel_bias)


def _compress_kernel(x_ref, pe_ref, w1_ref, w2_ref, o_ref, ot_ref):
    x = x_ref[0].astype(F32)
    half = x.shape[1]
    lo = (x + pe_ref[0, 0:1, :]).astype(BF16)
    hi = (x + pe_ref[0, 1:2, :]).astype(BF16)
    a = jnp.dot(lo, w1_ref[0, :half, :], preferred_element_type=F32)
    b = jnp.dot(hi, w1_ref[0, half:, :], preferred_element_type=F32)
    n_rows = x.shape[0]
    hidden = a + pltpu.roll(b, n_rows - 1, 0)
    out = jnp.dot(_silu(hidden).astype(BF16), w2_ref[0], preferred_element_type=F32)
    o_ref[0] = out.astype(o_ref.dtype)
    ot_ref[0] = out.T.astype(ot_ref.dtype)


def _compress(x_slabs, pe, w1, w2):
    ng, n_rows, half = x_slabs.shape
    g = A_KV_GROUPS
    return pl.pallas_call(
        _compress_kernel,
        grid=(2, g),
        in_specs=[pl.BlockSpec((1, n_rows, half), lambda s, i: (s * g + i, 0, 0)),
                  pl.BlockSpec((1, 2, half), lambda s, i: (s, 0, 0)),
                  pl.BlockSpec((1, 2 * half, HEAD_DIM), lambda s, i: (s, 0, 0)),
                  pl.BlockSpec((1, HEAD_DIM, HEAD_DIM), lambda s, i: (s, 0, 0))],
        out_specs=[pl.BlockSpec((1, n_rows, HEAD_DIM), lambda s, i: (s * g + i, 0, 0)),
                   pl.BlockSpec((1, HEAD_DIM, n_rows), lambda s, i: (s * g + i, 0, 0))],
        out_shape=[jax.ShapeDtypeStruct((ng, n_rows, HEAD_DIM), BF16),
                   jax.ShapeDtypeStruct((ng, HEAD_DIM, n_rows), BF16)],
        compiler_params=_params(("parallel", "parallel"),
                                [n_rows * half * 2, 2 * half * 4, 2 * half * HEAD_DIM * 2],
                                4 * n_rows * half * 4),
        name="compress",
    )(x_slabs, pe, w1, w2)


def _cmp_attn_kernel(q_ref, kc_ref, vct_ref, strip_ref, gate_ref, o_ref, sel_ref, s_scr, pg_scr):
    n_rows = kc_ref.shape[1]
    n_sel = n_rows // (L_SEL // STRIDE_CMP)
    tq = CMP_TQ
    rpt = CMP_ROWS_PER_TQ
    pad = V7X_SUBLANES
    lane_tiles = tq // V7X_LANES
    per_sel = L_SEL // STRIDE_CMP
    sub_qi = [pl.program_id(1) * CMP_SUB + sub for sub in range(CMP_SUB)]
    t_rows = [qi * tq + lax.broadcasted_iota(jnp.int32, (1, tq), 1) for qi in sub_qi]

    for sub, qi in enumerate(sub_qi):
        lanes = slice(sub * tq, (sub + 1) * tq)
        key_live = lax.broadcasted_iota(jnp.int32, (n_rows, tq), 0) < (qi + 1) * rpt
        strip_rows = pl.ds(pl.multiple_of(qi * rpt, rpt), 2 * rpt)
        for r in range(HEADS_PER_GROUP):
            st = jnp.dot(kc_ref[0], q_ref[r * HEAD_DIM:(r + 1) * HEAD_DIM, lanes], preferred_element_type=F32)
            s_scr[sub, r, 0:rpt, :] = jnp.zeros((rpt, tq), F32)
            s_scr[sub, r, rpt:rpt + n_rows, :] = jnp.where(key_live, st, NEG_INF)
            s_scr[sub, r, strip_rows, :] = s_scr[sub, r, strip_rows, :] + strip_ref[r]

    for sub in range(CMP_SUB):
        lanes = slice(sub * tq, (sub + 1) * tq)
        any_valid = jnp.where(t_rows[sub] >= L_CMP - 1, 1.0, 0.0)
        pg = jnp.zeros((n_rows, tq), F32)
        for r in range(HEADS_PER_GROUP):
            s = s_scr[sub, r, rpt:rpt + n_rows, :]
            m = jnp.max(s, axis=0, keepdims=True)
            e = jnp.exp2(s - m)
            p = e * (any_valid / jnp.sum(e, axis=0, keepdims=True))
            pg = pg + p
            ot = jnp.dot(vct_ref[0], p.astype(BF16), preferred_element_type=F32) * gate_ref[r:r + 1, lanes]
            o_ref[lanes, r * HEAD_DIM:(r + 1) * HEAD_DIM] = ot.T.astype(o_ref.dtype)
        for c in range(lane_tiles):
            pg_scr[sub, c, 0:pad, :] = jnp.zeros((pad, V7X_LANES), F32)
            pg_scr[sub, c, pad:pad + n_rows, :] = pg[:, c * V7X_LANES:(c + 1) * V7X_LANES]

    for sub in range(CMP_SUB):
        def rows(off, sub=sub):
            return jnp.concatenate(
                [pg_scr[sub, c, pl.ds(pad + off, n_sel, stride=per_sel), :] for c in range(lane_tiles)], axis=1)

        imp = (rows(-1) + rows(3)) + 2.0 * (rows(0) + rows(1) + rows(2))
        j = lax.broadcasted_iota(jnp.int32, (n_sel, tq), 0)
        jf = j.astype(F32)
        jt = jnp.right_shift(t_rows[sub], int(math.log2(L_SEL)))
        forced = (j == 0) | (j == jt) | (j == jt - 1)
        cand = (j >= 1) & (j <= jt - 2)
        score = jnp.where(cand, imp, -1.0)
        sel = jnp.where(forced, 1.0, 0.0)
        for _ in range(N_SEL - 3):
            mx = jnp.max(score, axis=0, keepdims=True)
            first_j = jnp.min(jnp.where(score == mx, jf, float(n_sel)), axis=0, keepdims=True)
            pick = (jf == first_j) & (mx >= 0.0)
            sel = jnp.where(pick, 1.0, sel)
            score = jnp.where(pick, -2.0, score)
        neg = jnp.where(sel > 0.5, 0.0, NEG_INF)
        if n_sel < V7X_LANES:
            neg = jnp.concatenate([neg, jnp.zeros((V7X_LANES - n_sel, tq), F32)], axis=0)
        sel_ref[0, :, sub * tq:(sub + 1) * tq] = neg.astype(sel_ref.dtype)


def _cmp_attn(q2, kc, vct, strip, gates_t, t):
    n_rows = kc.shape[1]
    g = A_KV_GROUPS
    tq = CMP_TQ
    tqs = CMP_SUB * tq
    return pl.pallas_call(
        _cmp_attn_kernel,
        grid=(g, t // tqs),
        in_specs=[pl.BlockSpec((GROUP_Q_WIDTH, tqs), lambda gi, qi: (gi, qi)),
                  pl.BlockSpec((1, n_rows, HEAD_DIM), lambda gi, qi: (gi, 0, 0)),
                  pl.BlockSpec((1, HEAD_DIM, n_rows), lambda gi, qi: (g + gi, 0, 0)),
                  pl.BlockSpec((HEADS_PER_GROUP, 2 * CMP_ROWS_PER_TQ, tq), lambda gi, qi: (gi, 0, 0)),
                  _gate_spec(0, tqs)],
        out_specs=[pl.BlockSpec((tqs, GROUP_Q_WIDTH), lambda gi, qi: (qi, gi)),
                   pl.BlockSpec((1, V7X_LANES, tqs), lambda gi, qi: (gi, 0, qi))],
        out_shape=[jax.ShapeDtypeStruct((t, A_WIDTH), BF16),
                   jax.ShapeDtypeStruct((g, V7X_LANES, t), BF16)],
        scratch_shapes=[pltpu.VMEM((CMP_SUB, HEADS_PER_GROUP, CMP_ROWS_PER_TQ + n_rows, tq), F32),
                        pltpu.VMEM((CMP_SUB, tq // V7X_LANES, n_rows + V7X_SUBLANES, V7X_LANES), F32)],
        compiler_params=_params(("parallel", "arbitrary"),
                                [tqs * GROUP_Q_WIDTH * 2 * 2, n_rows * HEAD_DIM * 2 * 2, tqs * V7X_LANES * 2],
                                CMP_SUB * 12 * n_rows * tq * 4),
        name="cmp_attn",
    )(q2, kc, vct, strip, gates_t)


def _gate_spec(branch, tq):
    return pl.BlockSpec((V7X_SUBLANES, tq), lambda gi, qi: (branch * A_KV_GROUPS + gi, qi))


def _softmax_step(r, z, z_max, vt, m_scr, l_scr, acc_scr):
    m_prev = m_scr[r]
    m_new = jnp.maximum(m_prev, z_max)
    p = jnp.exp2(z - m_new)
    alpha = jnp.exp2(m_prev - m_new)
    l_scr[r] = alpha * l_scr[r] + jnp.sum(p, axis=0, keepdims=True)
    acc_scr[r] = acc_scr[r] * alpha + jnp.dot(vt, p.astype(BF16), preferred_element_type=F32)
    m_scr[r] = m_new


def _pipelined_tile(kvj, kvj_next, logits_head, logits_next, softmax_head):
    last = HEADS_PER_GROUP - 1
    for r in range(HEADS_PER_GROUP):
        if r < last:
            logits_head(r + 1, kvj)
        else:
            logits_next(0, kvj_next)
        softmax_head(r, kvj)


def _softmax_init(m_scr, l_scr, acc_scr):
    m_scr[...] = jnp.full(m_scr.shape, NEG_INF, F32)
    l_scr[...] = jnp.zeros(l_scr.shape, F32)
    acc_scr[...] = jnp.zeros(acc_scr.shape, F32)


def _softmax_finish(gate_ref, o_ref, l_scr, acc_scr):
    for r in range(HEADS_PER_GROUP):
        ot = acc_scr[r] * (gate_ref[r:r + 1, :] / l_scr[r])
        o_ref[:, r * HEAD_DIM:(r + 1) * HEAD_DIM] = ot.T.astype(o_ref.dtype)


def _sel_attn_kernel(q_ref, k_ref, e_ref, vt_ref, sel_ref, bt_diag_ref, bt_prev_ref, gate_ref, o_ref,
                     qaug_scr, z_scr, zmax_scr, m_scr, l_scr, acc_scr):
    qi = pl.program_id(1)
    tk = ATTN_TILE
    _softmax_init(m_scr, l_scr, acc_scr)
    for r in range(HEADS_PER_GROUP):
        qaug_scr[r, 0:HEAD_DIM, :] = q_ref[r * HEAD_DIM:(r + 1) * HEAD_DIM, :]
        qaug_scr[r, HEAD_DIM:2 * HEAD_DIM, :] = sel_ref[0]

    def logits_with(bt_ref):
        def logits_head(r, kvj):
            rows = pl.ds(pl.multiple_of(kvj * tk, tk), tk)
            k_aug = jnp.concatenate([k_ref[rows, :], e_ref[rows, :]], axis=1)
            z = jnp.dot(k_aug, qaug_scr[r], preferred_element_type=F32)
            if bt_ref is not None:
                z = z + bt_ref[r, 0]
            z_scr[r] = z
            zmax_scr[r] = jnp.max(z, axis=0, keepdims=True)
        return logits_head

    logits_diag, logits_prev, logits_far = logits_with(bt_diag_ref), logits_with(bt_prev_ref), logits_with(None)

    def softmax_head(r, kvj):
        _softmax_step(r, z_scr[r], zmax_scr[r], vt_ref[0, 0, kvj], m_scr, l_scr, acc_scr)

    logits_diag(0, qi)
    _pipelined_tile(qi, jnp.maximum(qi - 1, 0), logits_diag, logits_prev, softmax_head)

    @pl.when(qi >= 1)
    def _():
        _pipelined_tile(qi - 1, 0, logits_prev, logits_far, softmax_head)

    n_far = jnp.maximum(qi - 1, 0)

    def far_tile(kvj, carry):
        _pipelined_tile(kvj, jnp.minimum(kvj + 1, n_far - 1), logits_far, logits_far, softmax_head)
        return carry

    lax.fori_loop(0, n_far, far_tile, 0)
    _softmax_finish(gate_ref, o_ref, l_scr, acc_scr)


def _win_attn_kernel(q_ref, k_ref, vt_ref, bt_diag_ref, bt_prev_ref, gate_ref, oc_ref, os_ref, za_ref, o_ref,
                     z_scr):
    qi = pl.program_id(1)
    tk = ATTN_TILE
    hk = tk // 2
    units = [(r, half) for r in range(HEADS_PER_GROUP) for half in range(2)]

    def run(key_row0, n_keys, bias, values_t):
        def logits(u):
            r, half = units[u]
            row0 = key_row0(half)
            row0 = row0 if isinstance(row0, int) else pl.multiple_of(row0, hk)
            k = k_ref[pl.ds(row0, n_keys(half)), :]
            qt = q_ref[r * HEAD_DIM:(r + 1) * HEAD_DIM, half * hk:(half + 1) * hk]
            z_scr[u, 0:n_keys(half), :] = jnp.dot(k, qt, preferred_element_type=F32)

        def softmax(u):
            r, half = units[u]
            rows = slice(half * hk, (half + 1) * hk)
            cols = slice(r * HEAD_DIM, (r + 1) * HEAD_DIM)
            z = z_scr[u, 0:n_keys(half), :] + bias(r, half)
            p = jnp.exp2(z - jnp.max(z, axis=0, keepdims=True))
            scale = gate_ref[r:r + 1, rows] / jnp.sum(p, axis=0, keepdims=True)
            o_win = (jnp.dot(values_t(half), p.astype(BF16), preferred_element_type=F32) * scale).T
            o_a = oc_ref[rows, cols].astype(F32) + os_ref[rows, cols].astype(F32) + o_win
            o_ref[rows, cols] = (o_a * za_ref[rows, cols].astype(F32)).astype(o_ref.dtype)

        logits(0)
        for u in range(len(units)):
            if u + 1 < len(units):
                logits(u + 1)
            softmax(u)

    @pl.when(qi >= 1)
    def _():
        def bias(r, half):
            lanes = slice(half * hk, (half + 1) * hk)
            if half == 0:
                parts = [bt_prev_ref[r, 0, 0:hk, lanes], bt_prev_ref[r, 0, hk:tk, lanes],
                         bt_diag_ref[r, 0, 0:hk, lanes]]
            else:
                parts = [bt_prev_ref[r, 0, hk:tk, lanes], bt_diag_ref[r, 0, 0:hk, lanes],
                         bt_diag_ref[r, 0, hk:tk, lanes]]
            return jnp.concatenate(parts, axis=0)

        def values_t(half):
            prev, diag = vt_ref[0, 0, qi - 1], vt_ref[0, 0, qi]
            if half == 0:
                return jnp.concatenate([prev, diag[:, 0:hk]], axis=1)
            return jnp.concatenate([prev[:, hk:tk], diag], axis=1)

        run(lambda half: (qi - 1) * tk + half * hk, lambda half: 3 * hk, bias, values_t)

    @pl.when(qi == 0)
    def _():
        def bias(r, half):
            return bt_diag_ref[r, 0, 0:(half + 1) * hk, half * hk:(half + 1) * hk]

        run(lambda half: 0, lambda half: (half + 1) * hk, bias, lambda half: vt_ref[0, 0, 0][:, 0:(half + 1) * hk])


def _attn_scratch(tq):
    return [pltpu.VMEM((HEADS_PER_GROUP, ATTN_TILE, tq), F32), pltpu.VMEM((HEADS_PER_GROUP, 1, tq), F32),
            pltpu.VMEM((HEADS_PER_GROUP, 1, tq), F32), pltpu.VMEM((HEADS_PER_GROUP, HEAD_DIM, tq), F32)]


def _attn_common_specs(t, k_set, branch):
    tq = ATTN_TILE
    q_spec = pl.BlockSpec((GROUP_Q_WIDTH, tq), lambda gi, qi: (gi, qi))
    k_spec = pl.BlockSpec((t, HEAD_DIM), lambda gi, qi: (0, k_set * A_KV_GROUPS + gi))
    vt_spec = pl.BlockSpec((1, 1, t // ATTN_TILE, HEAD_DIM, ATTN_TILE), lambda gi, qi: (k_set, gi, 0, 0, 0))
    out_spec = pl.BlockSpec((tq, GROUP_Q_WIDTH), lambda gi, qi: (qi, gi))
    return q_spec, k_spec, vt_spec, _gate_spec(branch, tq), out_spec


def _bt_spec(kind):
    return pl.BlockSpec((HEADS_PER_GROUP, 1, ATTN_TILE, ATTN_TILE), lambda gi, qi: (gi, kind, 0, 0))


def _attn_block_bytes(t):
    tq = ATTN_TILE
    return [tq * GROUP_Q_WIDTH * 2 * 2, t * HEAD_DIM * 2 * 2, 2 * HEADS_PER_GROUP * tq * tq * 4]


def _sel_attn(q2, k_plain, vt, sel_neg, bt, gates_t, t):
    tq = ATTN_TILE
    q_spec, k_spec, vt_spec, gate_spec, out_spec = _attn_common_specs(t, 0, 1)
    sel_spec = pl.BlockSpec((1, V7X_LANES, tq), lambda gi, qi: (gi, 0, qi))
    e_onehot = (jnp.arange(t, dtype=jnp.int32)[:, None] // L_SEL
                == jnp.arange(V7X_LANES, dtype=jnp.int32)[None, :]).astype(BF16)
    e_spec = pl.BlockSpec((t, V7X_LANES), lambda gi, qi: (0, 0))
    return pl.pallas_call(
        _sel_attn_kernel,
        grid=(A_KV_GROUPS, t // tq),
        in_specs=[q_spec, k_spec, e_spec, vt_spec, sel_spec, _bt_spec(0), _bt_spec(1), gate_spec],
        out_specs=out_spec,
        out_shape=jax.ShapeDtypeStruct((t, A_WIDTH), BF16),
        scratch_shapes=([pltpu.VMEM((HEADS_PER_GROUP, 2 * HEAD_DIM, tq), BF16)] + _attn_scratch(tq)[:1]
                        + [pltpu.VMEM((HEADS_PER_GROUP, 1, tq), F32)] + _attn_scratch(tq)[1:]),
        compiler_params=_params(("parallel", "arbitrary"),
                                _attn_block_bytes(t) + [t * V7X_LANES * 2, tq * V7X_LANES * 2],
                                (8 + HEADS_PER_GROUP) * tq * tq * 4),
        name="sel_attn",
    )(q2, k_plain, e_onehot, vt, sel_neg, bt, bt, gates_t)


def _win_attn(q2, k_plain, vt, bt, gates_t, o_cmp, o_sel, za, t):
    tq = ATTN_TILE
    q_spec, k_spec, vt_spec, gate_spec, out_spec = _attn_common_specs(t, 1, 2)
    return pl.pallas_call(
        _win_attn_kernel,
        grid=(A_KV_GROUPS, t // tq),
        in_specs=[q_spec, k_spec, vt_spec, _bt_spec(0), _bt_spec(2), gate_spec, out_spec, out_spec, out_spec],
        out_specs=out_spec,
        out_shape=jax.ShapeDtypeStruct((t, A_WIDTH), BF16),
        scratch_shapes=[pltpu.VMEM((2 * HEADS_PER_GROUP, 3 * tq // 2, tq // 2), F32)],
        compiler_params=_params(("parallel", "arbitrary"),
                                _attn_block_bytes(t) + [3 * tq * GROUP_Q_WIDTH * 2],
                                (8 + HEADS_PER_GROUP) * tq * tq * 4),
        name="win_attn",
    )(q2, k_plain, vt, bt, bt, gates_t, o_cmp, o_sel, za)


def _gmlp_kernel(u_ref, v_ref, z_ref, ng_ref, w_ref, bt_ref, o_ref):
    v = v_ref[...].astype(F32)
    ms = jnp.mean(v * v, axis=-1, keepdims=True)
    vn = (v * lax.rsqrt(ms + NORM_EPS) * ng_ref[...]).astype(BF16)
    p_idx = lax.broadcasted_iota(jnp.int32, (CHUNK, CHUNK), 0)
    q_idx = lax.broadcasted_iota(jnp.int32, (CHUNK, CHUNK), 1)
    causal = q_idx <= p_idx
    gd = B_WIDTH // B_GROUPS
    for gg in range(B_GROUPS):
        cols = slice(gg * gd, (gg + 1) * gd)
        w = jnp.where(causal, w_ref[gg], jnp.zeros((), w_ref.dtype))
        for c in range(GMLP_CHUNKS):
            rows = slice(c * CHUNK, (c + 1) * CHUNK)
            f = jnp.dot(w, vn[rows, cols], preferred_element_type=F32) + bt_ref[:, gg:gg + 1]
            o_ref[rows, cols] = (u_ref[rows, cols].astype(F32) * f
                                 * z_ref[rows, cols].astype(F32)).astype(o_ref.dtype)


def _gmlp(uv, zab, norm_g, w_s, b_t, t):
    assert A_WIDTH == B_WIDTH
    bw = B_WIDTH
    rows = GMLP_CHUNKS * CHUNK

    def col_spec(col):
        return pl.BlockSpec((rows, bw), lambda i: (i, col))

    return pl.pallas_call(
        _gmlp_kernel,
        grid=(t // rows,),
        in_specs=[col_spec(0), col_spec(1), col_spec(1),
                  pl.BlockSpec((1, bw), lambda i: (0, 0)),
                  pl.BlockSpec((B_GROUPS, CHUNK, CHUNK), lambda i: (0, 0, 0)),
                  pl.BlockSpec((CHUNK, B_GROUPS), lambda i: (0, 0))],
        out_specs=pl.BlockSpec((rows, bw), lambda i: (i, 0)),
        out_shape=jax.ShapeDtypeStruct((t, bw), BF16),
        compiler_params=_params(("parallel",), [rows * bw * 2] * 4 + [B_GROUPS * CHUNK * CHUNK * 2],
                                4 * rows * bw * 4),
        name="gmlp",
    )(uv, uv, zab, norm_g.reshape(1, bw), w_s, b_t)


def _merge_kernel(oa_ref, ob_ref, wa_ref, wb_ref, ma_ref, mb_ref, o_ref):
    ya = _nn_dot(oa_ref[...], wa_ref[0])
    yb = _nn_dot(ob_ref[...], wb_ref[0])
    o_ref[...] = (ma_ref[...].astype(F32) * ya + mb_ref[...].astype(F32) * yb).astype(o_ref.dtype)


def _merge(o_a, o_b, m_gates, w_a, w_b, layer, t, d):
    tm = min(MM_TM, t)
    tn = min(W32_TN, d)
    aw = A_WIDTH
    bw = B_WIDTH
    return pl.pallas_call(
        _merge_kernel,
        grid=(d // tn, t // tm),
        in_specs=[pl.BlockSpec((tm, aw), lambda j, i: (i, 0)),
                  pl.BlockSpec((tm, bw), lambda j, i: (i, 0)),
                  pl.BlockSpec((1, aw, tn), lambda j, i: (layer, 0, j)),
                  pl.BlockSpec((1, bw, tn), lambda j, i: (layer, 0, j)),
                  pl.BlockSpec((tm, tn), lambda j, i: (i, j)),
                  pl.BlockSpec((tm, tn), lambda j, i: (i, d // tn + j))],
        out_specs=pl.BlockSpec((tm, tn), lambda j, i: (i, j)),
        out_shape=jax.ShapeDtypeStruct((t, d), BF16),
        compiler_params=_params(("arbitrary", "arbitrary"),
                                [tm * aw * 2] * 2 + [aw * tn * 4, bw * tn * 4] + [tm * tn * 2] * 3,
                                4 * tm * tn * 4),
        name="merge",
    )(o_a, o_b, w_a, w_b, m_gates, m_gates)


def _gate_weight_columns():
    src = np.full((V7X_LANES,), -1, np.int64)
    for br in range(3):
        for g in range(A_KV_GROUPS):
            for r in range(HEADS_PER_GROUP):
                src[(br * A_KV_GROUPS + g) * V7X_SUBLANES + r] = (g * HEADS_PER_GROUP + r) * 3 + br
    return src


def kernel(x, rel_bias, pre_norm, w_in, cmp_pe_k, cmp_w1_k, cmp_w2_k, cmp_pe_v, cmp_w1_v, cmp_w2_v,
           w_out_a, sgu_norm, sgu_w, sgu_b, w_out_b, w_out, post_norm):
    batch, t, d = x.shape
    assert batch == 1 and t % MM_TM == 0 and N_SEL <= t // L_SEL <= V7X_LANES
    depth = w_in.shape[0]
    g = A_KV_GROUPS
    xs = x.reshape(t, d)

    w_in_t = jnp.swapaxes(w_in, 1, 2)
    gate_src = _gate_weight_columns()
    w_gate = jnp.take(w_in_t[:, SRC_GATES:SRC_REST, :], jnp.asarray(np.maximum(gate_src, 0)), axis=1)
    w_gate = jnp.where(jnp.asarray(gate_src >= 0)[None, :, None], w_gate, 0.0).astype(BF16)
    w1 = jnp.stack([cmp_w1_k, cmp_w1_v], axis=1).astype(BF16)
    w2 = jnp.stack([cmp_w2_k, cmp_w2_v], axis=1).astype(BF16)
    slab = STRIDE_CMP * HEAD_DIM
    pe = jnp.stack([cmp_pe_k, cmp_pe_v], axis=1).reshape(depth, 2, 2, slab)
    w_s = sgu_w.astype(BF16)

    bias_tiles = _bias_tiles(rel_bias)
    bias_strip = _bias_strip(rel_bias)

    h = _rmsnorm(xs, pre_norm[0], BF16)
    for layer in range(depth):
        q2 = _q_proj(h, w_in_t, layer)
        x_slabs = _cmp_proj(h, w_in_t, layer)
        k_plain, vt = _kv_proj(h, w_in_t, layer, SRC_KV + 2 * A_KV_WIDTH, 2)
        gates_t = _gate_proj(h, w_gate, layer)
        zab = _mm_w32(h, w_in_t, layer, SRC_REST + COL_ZA, A_WIDTH, BF16, "silu", "z_proj",
                      row1=SRC_REST + COL_ZB)
        uv = _mm_w32(h, w_in_t, layer, SRC_REST + COL_U, 2 * B_WIDTH, BF16, "gelu", "uv_proj")
        m_gates = _mm_w32(h, w_in_t, layer, SRC_REST + COL_MA, 2 * d, BF16, "sigmoid", "m_proj")

        kc, kct = _compress(x_slabs, pe[layer], w1[layer], w2[layer])
        o_cmp, sel_neg = _cmp_attn(q2, kc, kct, bias_strip, gates_t, t)
        o_sel = _sel_attn(q2, k_plain, vt, sel_neg, bias_tiles, gates_t, t)
        o_a = _win_attn(q2, k_plain, vt, bias_tiles, gates_t, o_cmp, o_sel, zab, t)

        o_b = _gmlp(uv, zab, sgu_norm[layer], w_s[layer], sgu_b[layer].T, t)
        merged = _merge(o_a, o_b, m_gates, w_out_a, w_out_b, layer, t, d)
        y = _mm(merged, w_out, layer, BF16, name="out_proj")
        g_next = pre_norm[layer + 1] if layer + 1 < depth else pre_norm[layer]
        xs, h = _post_norm_residual(xs, y, post_norm[layer], g_next)
    del g
    return xs.reshape(batch, t, d)
```
